```python
import math
import jax, jax.numpy as jnp
from jax import lax
import numpy as np

D_MODEL = 1024
BATCH = 8
SEQ = 2048
DEPTH = 2

ATT_HEADS = 8
ATT_QK_DIM = 64
ATT_V_DIM = 2 * ATT_QK_DIM
ATT_QK_WIDTH = ATT_HEADS * 2 * ATT_QK_DIM
ATT_WIDTH = ATT_HEADS * ATT_V_DIM
ROPE_THETA = 10000.0
Q_BLOCK = 128
CONV_WIDTH = 1024
CONV_KERNEL = 31
DN_HEADS = 8
DN_HEAD_DIM = 128
DN_WIDTH = DN_HEADS * DN_HEAD_DIM
DN_SHORT_CONV = 4
DN_CHUNK = 64
N_BRANCHES = 3
RMS_EPS = 1e-6
LN_EPS = 1e-5
IN_WIDTHS = (ATT_QK_WIDTH, ATT_QK_WIDTH, ATT_WIDTH, ATT_WIDTH,
             2 * CONV_WIDTH, CONV_WIDTH,
             DN_WIDTH, DN_WIDTH, DN_WIDTH, DN_WIDTH, DN_HEADS, DN_HEADS,
             N_BRANCHES * D_MODEL)
IN_DIM = sum(IN_WIDTHS)

kernel_name = 'hybrid_diffattn_conformer_gdn_gated_merge'


def rms_norm(x, w, eps=RMS_EPS):
    xf = x.astype(jnp.float32)
    y = xf * lax.rsqrt(jnp.mean(xf * xf, axis=-1, keepdims=True) + eps)
    return (y * w.astype(jnp.float32)).astype(x.dtype)


def layer_norm(x, w, b, eps=LN_EPS):
    xf = x.astype(jnp.float32)
    mu = jnp.mean(xf, axis=-1, keepdims=True)
    xc = xf - mu
    y = xc * lax.rsqrt(jnp.mean(xc * xc, axis=-1, keepdims=True) + eps)
    return (y * w.astype(jnp.float32) + b.astype(jnp.float32)).astype(x.dtype)


def l2_normalize(x, eps=1e-6):
    xf = x.astype(jnp.float32)
    return xf * lax.rsqrt(jnp.sum(xf * xf, axis=-1, keepdims=True) + eps)


def causal_depthwise_conv(x, w):
    K, C = w.shape
    return lax.conv_general_dilated(x, w[:, None, :].astype(x.dtype), window_strides=(1,),
                                    padding=[(K - 1, 0)],
                                    dimension_numbers=('NWC', 'WIO', 'NWC'),
                                    feature_group_count=C)


def rope_tables(positions):
    inv_freq = ROPE_THETA ** (-jnp.arange(0, ATT_QK_DIM, 2, dtype=jnp.float32) / ATT_QK_DIM)
    ang = positions.astype(jnp.float32)[..., None] * inv_freq
    return jnp.cos(ang)[:, :, None, None, :], jnp.sin(ang)[:, :, None, None, :]


def apply_rope(t, cos, sin):
    tf = t.astype(jnp.float32)
    t1, t2 = jnp.split(tf, 2, axis=-1)
    return jnp.concatenate([t1 * cos - t2 * sin, t2 * cos + t1 * sin], axis=-1).astype(t.dtype)


def diff_attention_branch(q_in, k_in, v_in, z, cos, sin, lam_qk, subln_w, w_o, lambda_init):
    B, L, _ = q_in.shape
    q = apply_rope(q_in.reshape(B, L, ATT_HEADS, 2, ATT_QK_DIM), cos, sin).transpose(0, 2, 3, 1, 4)
    k = apply_rope(k_in.reshape(B, L, ATT_HEADS, 2, ATT_QK_DIM), cos, sin).transpose(0, 2, 3, 1, 4)
    v = v_in.reshape(B, L, ATT_HEADS, ATT_V_DIM).transpose(0, 2, 1, 3)
    lq = lam_qk.astype(jnp.float32)
    lam = jnp.exp(jnp.sum(lq[0] * lq[1])) - jnp.exp(jnp.sum(lq[2] * lq[3])) + lambda_init
    scale = ATT_QK_DIM ** -0.5
    outs = []
    for blk in range(L // Q_BLOCK):
        s0 = blk * Q_BLOCK
        e = s0 + Q_BLOCK
        s = jnp.einsum('bhmqd,bhmkd->bhmqk', q[:, :, :, s0:e], k[:, :, :, :e]).astype(jnp.float32) * scale
        mask = (s0 + jnp.arange(Q_BLOCK))[:, None] >= jnp.arange(e)[None, :]
        p = jax.nn.softmax(jnp.where(mask, s, -jnp.inf), axis=-1)
        a = p[:, :, 0] - lam * p[:, :, 1]
        outs.append(jnp.einsum('bhqk,bhkd->bhqd', a.astype(v.dtype), v[:, :, :e]))
    o = jnp.concatenate(outs, axis=2)
    o = rms_norm(o, subln_w) * (1.0 - lambda_init)
    o = o.transpose(0, 2, 1, 3).reshape(B, L, ATT_WIDTH)
    return (o * jax.nn.silu(z)) @ w_o


def conformer_conv_branch(glu_in, z, dw_w, dw_b, ln_w, ln_b, w_o):
    a, g = jnp.split(glu_in, 2, axis=-1)
    u = a * jax.nn.sigmoid(g)
    u = causal_depthwise_conv(u, dw_w) + dw_b
    u = layer_norm(u, ln_w, ln_b)
    u = jax.nn.silu(u) * jax.nn.silu(z)
    return u @ w_o


def chunked_gated_delta_rule(q, k, v, g, beta):
    B, H, L, dk = q.shape
    dv = v.shape[-1]
    C = DN_CHUNK
    N = L // C
    f32 = jnp.float32
    q = q.astype(f32) * dk ** -0.5
    k = k.astype(f32)
    v = v.astype(f32)
    chunk = lambda t: t.reshape(B, H, N, C, *t.shape[3:])
    q, k, v, g, beta = chunk(q), chunk(k), chunk(v), chunk(g.astype(f32)), chunk(beta.astype(f32))
    g = jnp.cumsum(g, axis=-1)
    idx = jnp.arange(C)
    causal = idx[:, None] >= idx[None, :]
    strict = idx[:, None] > idx[None, :]
    decay = jnp.exp(jnp.where(causal, g[..., :, None] - g[..., None, :], -jnp.inf))
    kb = k * beta[..., None]
    lower = jnp.where(strict, jnp.einsum('bhnid,bhnjd->bhnij', kb, k) * decay, 0.0)
    tri = lower + jnp.eye(C, dtype=f32)
    rhs = jnp.concatenate([v * beta[..., None], kb * jnp.exp(g)[..., None]], axis=-1)
    sol = lax.linalg.triangular_solve(tri, rhs, left_side=True, lower=True, unit_diagonal=True)
    u, w = sol[..., :dv], sol[..., dv:]
    intra = jnp.einsum('bhnid,bhnjd->bhnij', q, k) * decay

    def step(state, inp):
        q_n, k_n, u_n, w_n, g_n, a_n = inp
        v_new = u_n - jnp.einsum('bhck,bhkv->bhcv', w_n, state)
        o_n = (jnp.einsum('bhck,bhkv->bhcv', q_n * jnp.exp(g_n)[..., None], state)
               + jnp.einsum('bhij,bhjv->bhiv', a_n, v_new))
        g_last = g_n[..., -1:]
        k_dec = k_n * jnp.exp(g_last - g_n)[..., None]
        state = state * jnp.exp(g_last)[..., None] + jnp.einsum('bhck,bhcv->bhkv', k_dec, v_new)
        return state, o_n

    xs = tuple(jnp.moveaxis(t, 2, 0) for t in (q, k, u, w, g, intra))
    state0 = jnp.zeros((B, H, dk, dv), f32)
    _, o = lax.scan(step, state0, xs)
    return jnp.moveaxis(o, 0, 2).reshape(B, H, L, dv)


def gated_deltanet_branch(q_in, k_in, v_in, z, b_in, a_in, conv_w, a_log, dt_bias, norm_w, w_o):
    B, L, _ = q_in.shape
    qkv = jax.nn.silu(causal_depthwise_conv(jnp.concatenate([q_in, k_in, v_in], axis=-1), conv_w))
    q, k, v = jnp.split(qkv, 3, axis=-1)
    heads = lambda t: t.reshape(B, L, DN_HEADS, DN_HEAD_DIM).transpose(0, 2, 1, 3)
    q = l2_normalize(heads(q))
    k = l2_normalize(heads(k))
    v = heads(v)
    beta = jax.nn.sigmoid(b_in.astype(jnp.float32)).transpose(0, 2, 1)
    g = (-jnp.exp(a_log.astype(jnp.float32))
         * jax.nn.softplus(a_in.astype(jnp.float32) + dt_bias.astype(jnp.float32))).transpose(0, 2, 1)
    o = chunked_gated_delta_rule(q, k, v, g, beta).transpose(0, 2, 1, 3)
    zh = z.reshape(B, L, DN_HEADS, DN_HEAD_DIM).astype(jnp.float32)
    o = rms_norm(o, norm_w) * jax.nn.silu(zh)
    return o.reshape(B, L, DN_WIDTH).astype(z.dtype) @ w_o


def hybrid_layer(x, cos, sin, layer_idx, norm_w, w_in, lam_qk, attn_subln_w, w_attn_out,
                 conv_dw_w, conv_dw_b, conv_ln_w, conv_ln_b, w_conv_out,
                 dn_conv_w, dn_a_log, dn_dt_bias, dn_norm_w, w_dn_out, w_out):
    h = rms_norm(x, norm_w)
    proj = h @ w_in
    split_points = np.cumsum(np.array(IN_WIDTHS))[:-1].tolist()
    (aq, ak, av, az, c_glu, cz, dq, dk, dvv, dz, db, da, gate_logits) = jnp.split(proj, split_points, axis=-1)
    lambda_init = 0.8 - 0.6 * math.exp(-0.3 * layer_idx)
    y_a = diff_attention_branch(aq, ak, av, az, cos, sin, lam_qk, attn_subln_w, w_attn_out, lambda_init)
    y_c = conformer_conv_branch(c_glu, cz, conv_dw_w, conv_dw_b, conv_ln_w, conv_ln_b, w_conv_out)
    y_d = gated_deltanet_branch(dq, dk, dvv, dz, db, da, dn_conv_w, dn_a_log, dn_dt_bias, dn_norm_w, w_dn_out)
    g_a, g_c, g_d = jnp.split(jax.nn.sigmoid(gate_logits), N_BRANCHES, axis=-1)
    merged = g_a * y_a + g_c * y_c + g_d * y_d
    return x + merged @ w_out


def setup_inputs(seed: int = 0) -> dict:
    key = jax.random.key(seed)
    ks = jax.random.split(key, 20)
    f32 = jnp.float32
    nrm = lambda k, shape, s: jax.random.normal(k, shape, f32) * s
    x = nrm(ks[0], (BATCH, SEQ, D_MODEL), 1.0)
    positions = jnp.tile(jnp.arange(SEQ, dtype=jnp.int32)[None, :], (BATCH, 1))
    norm_w = 1.0 + nrm(ks[1], (DEPTH, D_MODEL), 0.02)
    w_in = nrm(ks[2], (DEPTH, D_MODEL, IN_DIM), D_MODEL ** -0.5)
    lam_qk = nrm(ks[3], (DEPTH, 4, ATT_QK_DIM), 0.1)
    attn_subln_w = 1.0 + nrm(ks[4], (DEPTH, ATT_V_DIM), 0.02)
    w_attn_out = nrm(ks[5], (DEPTH, ATT_WIDTH, D_MODEL), ATT_WIDTH ** -0.5)
    conv_dw_w = nrm(ks[6], (DEPTH, CONV_KERNEL, CONV_WIDTH), CONV_KERNEL ** -0.5)
    conv_dw_b = nrm(ks[7], (DEPTH, CONV_WIDTH), 0.02)
    conv_ln_w = 1.0 + nrm(ks[8], (DEPTH, CONV_WIDTH), 0.02)
    conv_ln_b = nrm(ks[9], (DEPTH, CONV_WIDTH), 0.02)
    w_conv_out = nrm(ks[10], (DEPTH, CONV_WIDTH, D_MODEL), CONV_WIDTH ** -0.5)
    dn_conv_w = nrm(ks[11], (DEPTH, DN_SHORT_CONV, 3 * DN_WIDTH), DN_SHORT_CONV ** -0.5)
    dn_a_log = jnp.log(jax.random.uniform(ks[12], (DEPTH, DN_HEADS), f32, 1.0, 16.0))
    dt = jnp.exp(jax.random.uniform(ks[13], (DEPTH, DN_HEADS), f32, math.log(1e-3), math.log(1e-1)))
    dn_dt_bias = dt + jnp.log(-jnp.expm1(-dt))
    dn_norm_w = 1.0 + nrm(ks[14], (DEPTH, DN_HEAD_DIM), 0.02)
    w_dn_out = nrm(ks[15], (DEPTH, DN_WIDTH, D_MODEL), DN_WIDTH ** -0.5)
    w_out = nrm(ks[16], (DEPTH, D_MODEL, D_MODEL), D_MODEL ** -0.5)
    final_norm_w = 1.0 + nrm(ks[17], (D_MODEL,), 0.02)
    return {'x': x, 'positions': positions, 'norm_w': norm_w, 'w_in': w_in, 'lam_qk': lam_qk,
            'attn_subln_w': attn_subln_w, 'w_attn_out': w_attn_out, 'conv_dw_w': conv_dw_w,
            'conv_dw_b': conv_dw_b, 'conv_ln_w': conv_ln_w, 'conv_ln_b': conv_ln_b,
            'w_conv_out': w_conv_out, 'dn_conv_w': dn_conv_w, 'dn_a_log': dn_a_log,
            'dn_dt_bias': dn_dt_bias, 'dn_norm_w': dn_norm_w, 'w_dn_out': w_dn_out,
            'w_out': w_out, 'final_norm_w': final_norm_w}


def reference(x, positions, norm_w, w_in, lam_qk, attn_subln_w, w_attn_out, conv_dw_w, conv_dw_b,
              conv_ln_w, conv_ln_b, w_conv_out, dn_conv_w, dn_a_log, dn_dt_bias, dn_norm_w,
              w_dn_out, w_out, final_norm_w):
    cos, sin = rope_tables(positions)
    for l in range(DEPTH):
        x = hybrid_layer(x, cos, sin, l, norm_w[l], w_in[l], lam_qk[l], attn_subln_w[l], w_attn_out[l],
                         conv_dw_w[l], conv_dw_b[l], conv_ln_w[l], conv_ln_b[l], w_conv_out[l],
                         dn_conv_w[l], dn_a_log[l], dn_dt_bias[l], dn_norm_w[l], w_dn_out[l], w_out[l])
    return rms_norm(x, final_norm_w)
```

```python
import functools
import math

import jax
import jax.numpy as jnp
from jax import lax
from jax.experimental import pallas as pl
from jax.experimental.pallas import tpu as pltpu

F32 = jnp.float32
BF16 = jnp.bfloat16

D_MODEL = 1024
ATT_HEADS = 8
ATT_QK_DIM = 64
ATT_V_DIM = 128
ROPE_THETA = 10000.0
CONV_WIDTH = 1024
CONV_KERNEL = 31
DN_HEADS = 8
DN_HEAD_DIM = 128
DN_WIDTH = DN_HEADS * DN_HEAD_DIM
DN_SHORT_CONV = 4
DN_CHUNK = 64
RMS_EPS = 1e-6
LN_EPS = 1e-5
L2_EPS = 1e-6

LANES = 128
SUBLANES = 8
VMEM_LIMIT = 56 * 1024 * 1024

COL_AQ, COL_AK, COL_AV, COL_AZ = 0, 1, 2, 3
COL_CA, COL_CG, COL_CZ = 4, 5, 6
COL_DQ, COL_DK, COL_DV, COL_DZ = 7, 8, 9, 10
COL_GATE = 11
N_MAIN = 14 * D_MODEL
ORIG_DB = 11 * D_MODEL
ORIG_GATE = ORIG_DB + 2 * DN_HEADS


def _silu(x):
    return x * jax.nn.sigmoid(x)


def _inproj_body(x_ref, nw_ref, w_ref, wba_ref, o_ref, ba_ref, h_ref):
    @pl.when(pl.program_id(1) == 0)
    def _():
        x = x_ref[...]
        ms = jnp.mean(x * x, axis=-1, keepdims=True)
        h = (x * lax.rsqrt(ms + RMS_EPS) * nw_ref[...]).astype(BF16)
        h_ref[...] = h
        ba_ref[...] = jnp.dot(h, wba_ref[...], preferred_element_type=F32)

    o_ref[...] = jnp.dot(h_ref[...], w_ref[...], preferred_element_type=F32).astype(BF16)


def _inproj(x2d, norm_w, w_main, w_ba, *, tm=1024, tn=1024):
    T = x2d.shape[0]
    grid = (T // tm, N_MAIN // tn)
    return pl.pallas_call(
        _inproj_body,
        grid=grid,
        in_specs=[
            pl.BlockSpec((tm, D_MODEL), lambda i, j: (i, 0)),
            pl.BlockSpec((1, D_MODEL), lambda i, j: (0, 0)),
            pl.BlockSpec((D_MODEL, tn), lambda i, j: (0, j)),
            pl.BlockSpec((D_MODEL, LANES), lambda i, j: (0, 0)),
        ],
        out_specs=[
            pl.BlockSpec((tm, tn), lambda i, j: (i, j)),
            pl.BlockSpec((tm, LANES), lambda i, j: (i, 0)),
        ],
        out_shape=[
            jax.ShapeDtypeStruct((T, N_MAIN), BF16),
            jax.ShapeDtypeStruct((T, LANES), F32),
        ],
        scratch_shapes=[pltpu.VMEM((tm, D_MODEL), BF16)],
        compiler_params=pltpu.CompilerParams(
            dimension_semantics=("parallel", "arbitrary"), vmem_limit_bytes=VMEM_LIMIT),
        name="inproj",
    )(x2d, norm_w, w_main, w_ba)


def _merge_body(x_ref, a_ref, c_ref, d_ref, ga_ref, gc_ref, gd_ref,
                wa_ref, wc_ref, wd_ref, wo_ref, fw_ref, o_ref, *, final_norm):
    ya = jnp.dot(a_ref[...], wa_ref[...], preferred_element_type=F32)
    yc = jnp.dot(c_ref[...], wc_ref[...], preferred_element_type=F32)
    yd = jnp.dot(d_ref[...], wd_ref[...], preferred_element_type=F32)
    merged = (jax.nn.sigmoid(ga_ref[...].astype(F32)) * ya
              + jax.nn.sigmoid(gc_ref[...].astype(F32)) * yc
              + jax.nn.sigmoid(gd_ref[...].astype(F32)) * yd)
    y = x_ref[...] + jnp.dot(merged.astype(BF16), wo_ref[...], preferred_element_type=F32)
    if final_norm:
        ms = jnp.mean(y * y, axis=-1, keepdims=True)
        y = y * lax.rsqrt(ms + RMS_EPS) * fw_ref[...]
    o_ref[...] = y


def _merge(x2d, ya, yc, yd, proj, wa, wc, wd, wo, final_w, *, final_norm, tm=512):
    T = x2d.shape[0]
    row = lambda i: (i, 0)
    const = lambda i: (0, 0)
    wspec = pl.BlockSpec((D_MODEL, D_MODEL), const)
    return pl.pallas_call(
        functools.partial(_merge_body, final_norm=final_norm),
        grid=(T // tm,),
        in_specs=[
            pl.BlockSpec((tm, D_MODEL), row),
            pl.BlockSpec((tm, D_MODEL), row),
            pl.BlockSpec((tm, D_MODEL), row),
            pl.BlockSpec((tm, D_MODEL), row),
            pl.BlockSpec((tm, D_MODEL), lambda i: (i, COL_GATE)),
            pl.BlockSpec((tm, D_MODEL), lambda i: (i, COL_GATE + 1)),
            pl.BlockSpec((tm, D_MODEL), lambda i: (i, COL_GATE + 2)),
            wspec, wspec, wspec, wspec,
            pl.BlockSpec((1, D_MODEL), const),
        ],
        out_specs=pl.BlockSpec((tm, D_MODEL), row),
        out_shape=jax.ShapeDtypeStruct((T, D_MODEL), F32),
        compiler_params=pltpu.CompilerParams(
            dimension_semantics=("parallel",), vmem_limit_bytes=VMEM_LIMIT),
        name="merge",
    )(x2d, ya, yc, yd, proj, proj, proj, wa, wc, wd, wo, final_w)


CONV_HALO = 32
CONV_ROWS = 64
CONV_STRIP = 256


def _causal_taps(buf_ref, w_ref, r0, rows, lanes, n_taps, halo):
    assert halo >= SUBLANES * ((n_taps - 1) // SUBLANES + 1)
    acc = None
    for r in range(min(SUBLANES, n_taps)):
        part = None
        for a in range((n_taps - 1 - r) // SUBLANES + 1):
            s = SUBLANES * a + r
            start = pl.multiple_of(r0 + (halo - SUBLANES * (a + 1)), SUBLANES)
            term = (w_ref[n_taps - 1 - s:n_taps - s, lanes]
                    * buf_ref[pl.ds(start, rows + SUBLANES), lanes])
            part = term if part is None else part + term
        if r:
            part = pltpu.roll(part, r, axis=0)
        acc = part if acc is None else acc + part
    return acc[SUBLANES:, :]


def _conv_body(a_ref, g_ref, z_ref, w_ref, b_ref, lnw_ref, lnb_ref, o_ref, ubuf, cbuf, *, tl):
    @pl.when(pl.program_id(1) == 0)
    def _():
        ubuf[0:CONV_HALO, :] = jnp.zeros((CONV_HALO, CONV_WIDTH), F32)

    a = a_ref[...].astype(F32)
    g = g_ref[...].astype(F32)
    ubuf[CONV_HALO:CONV_HALO + tl, :] = a * jax.nn.sigmoid(g)

    bias = b_ref[...]
    lnw = lnw_ref[...]
    lnb = lnb_ref[...]

    def step(r, carry):
        r0 = pl.multiple_of(r * CONV_ROWS, CONV_ROWS)
        for c in range(CONV_WIDTH // CONV_STRIP):
            lanes = slice(c * CONV_STRIP, (c + 1) * CONV_STRIP)
            cbuf[:, lanes] = _causal_taps(ubuf, w_ref, r0, CONV_ROWS, lanes, CONV_KERNEL, CONV_HALO)
        u = cbuf[...] + bias
        mu = jnp.mean(u, axis=-1, keepdims=True)
        uc = u - mu
        var = jnp.mean(uc * uc, axis=-1, keepdims=True)
        y = uc * lax.rsqrt(var + LN_EPS) * lnw + lnb
        zz = z_ref[pl.ds(r0, CONV_ROWS), :].astype(F32)
        o_ref[pl.ds(r0, CONV_ROWS), :] = (_silu(y) * _silu(zz)).astype(BF16)
        return carry

    lax.fori_loop(0, tl // CONV_ROWS, step, 0)
    ubuf[0:CONV_HALO, :] = ubuf[tl:tl + CONV_HALO, :]


def _conv_branch(proj3, dw_w, dw_b, ln_w, ln_b, *, tl=512):
    B, L, _ = proj3.shape
    vec = pl.BlockSpec((1, CONV_WIDTH), lambda b, t: (0, 0))
    return pl.pallas_call(
        functools.partial(_conv_body, tl=tl),
        grid=(B, L // tl),
        in_specs=[
            pl.BlockSpec((None, tl, CONV_WIDTH), lambda b, t: (b, t, COL_CA)),
            pl.BlockSpec((None, tl, CONV_WIDTH), lambda b, t: (b, t, COL_CG)),
            pl.BlockSpec((None, tl, CONV_WIDTH), lambda b, t: (b, t, COL_CZ)),
            pl.BlockSpec((CONV_KERNEL, CONV_WIDTH), lambda b, t: (0, 0)),
            vec, vec, vec,
        ],
        out_specs=pl.BlockSpec((None, tl, CONV_WIDTH), lambda b, t: (b, t, 0)),
        out_shape=jax.ShapeDtypeStruct((B, L, CONV_WIDTH), BF16),
        scratch_shapes=[pltpu.VMEM((CONV_HALO + tl, CONV_WIDTH), F32),
                        pltpu.VMEM((CONV_ROWS, CONV_WIDTH), F32)],
        compiler_params=pltpu.CompilerParams(
            dimension_semantics=("parallel", "arbitrary"), vmem_limit_bytes=VMEM_LIMIT),
        name="conv_branch",
    )(proj3, proj3, proj3, dw_w, dw_b, ln_w, ln_b)


def _nt_dot(a, b):
    return lax.dot_general(a, b, (((1,), (1,)), ((), ())), preferred_element_type=F32)


def _rope(t, cos, sin_signed):
    lane = lax.broadcasted_iota(jnp.int32, t.shape, 1)
    first_half = (lane % ATT_QK_DIM) < (ATT_QK_DIM // 2)
    half = ATT_QK_DIM // 2
    swapped = jnp.where(first_half, pltpu.roll(t, LANES - half, axis=1), pltpu.roll(t, half, axis=1))
    return t * cos + swapped * sin_signed


def _attn_body(q_ref, k_ref, v_ref, z_ref, cq_ref, sq_ref, ck_ref, sk_ref, lam_ref, sw_ref, o_ref,
               kr_ref, m_ref, l_ref, acc_ref, *, lambda_init, tq):
    qi = pl.program_id(2)

    @pl.when(qi == 0)
    def _():
        kr_ref[...] = _rope(k_ref[...].astype(F32), ck_ref[...], sk_ref[...]).astype(BF16)

    q = _rope(q_ref[...].astype(F32), cq_ref[...], sq_ref[...]) * (ATT_QK_DIM ** -0.5)
    lane = lax.broadcasted_iota(jnp.int32, q.shape, 1)
    qs = jnp.concatenate([jnp.where(lane < ATT_QK_DIM, q, 0.0),
                          jnp.where(lane >= ATT_QK_DIM, q, 0.0)], axis=0).astype(BF16)

    m_ref[...] = jnp.full(m_ref.shape, -jnp.inf, F32)
    l_ref[...] = jnp.zeros(l_ref.shape, F32)
    acc_ref[...] = jnp.zeros(acc_ref.shape, F32)

    def kv_step(j, masked):
        k0 = pl.multiple_of(j * tq, tq)
        s = _nt_dot(qs, kr_ref[pl.ds(k0, tq), :])
        if masked:
            row = lax.broadcasted_iota(jnp.int32, s.shape, 0) % tq
            col = lax.broadcasted_iota(jnp.int32, s.shape, 1)
            s = jnp.where(col <= row, s, -jnp.inf)
        m_prev = m_ref[...]
        m_new = jnp.maximum(m_prev, jnp.max(s, axis=-1, keepdims=True))
        alpha = jnp.exp(m_prev - m_new)
        p = jnp.exp(s - m_new)
        l_ref[...] = alpha * l_ref[...] + jnp.sum(p, axis=-1, keepdims=True)
        acc_ref[...] = alpha * acc_ref[...] + jnp.dot(
            p.astype(BF16), v_ref[pl.ds(k0, tq), :], preferred_element_type=F32)
        m_ref[...] = m_new

    def body(j, carry):
        kv_step(j, False)
        return carry

    lax.fori_loop(0, qi, body, 0)
    kv_step(qi, True)

    lq = lam_ref[...]
    lam = (jnp.exp(jnp.sum(lq[0:1, :] * lq[1:2, :], axis=-1, keepdims=True))
           - jnp.exp(jnp.sum(lq[2:3, :] * lq[3:4, :], axis=-1, keepdims=True)) + lambda_init)
    o_all = acc_ref[...] / l_ref[...]
    o = o_all[:tq, :] - lam * o_all[tq:, :]
    ms = jnp.mean(o * o, axis=-1, keepdims=True)
    on = o * lax.rsqrt(ms + RMS_EPS) * sw_ref[...] * (1.0 - lambda_init)
    o_ref[...] = (on * _silu(z_ref[...].astype(F32))).astype(BF16)


def _attention(proj3, cosf, sinf, lam_qk, subln_w, *, lambda_init, tq=256):
    B, L, _ = proj3.shape
    hpb = D_MODEL // LANES
    qblk = lambda col: pl.BlockSpec((None, tq, LANES), lambda b, h, i: (b, i, col * hpb + h))
    full = lambda col: pl.BlockSpec((None, L, LANES), lambda b, h, i: (b, 0, col * hpb + h))
    return pl.pallas_call(
        functools.partial(_attn_body, lambda_init=lambda_init, tq=tq),
        grid=(B, ATT_HEADS, L // tq),
        in_specs=[
            qblk(COL_AQ), full(COL_AK), full(COL_AV), qblk(COL_AZ),
            pl.BlockSpec((None, tq, LANES), lambda b, h, i: (b, i, 0)),
            pl.BlockSpec((None, tq, LANES), lambda b, h, i: (b, i, 0)),
            pl.BlockSpec((None, L, LANES), lambda b, h, i: (b, 0, 0)),
            pl.BlockSpec((None, L, LANES), lambda b, h, i: (b, 0, 0)),
            pl.BlockSpec((4, ATT_QK_DIM), lambda b, h, i: (0, 0)),
            pl.BlockSpec((1, ATT_V_DIM), lambda b, h, i: (0, 0)),
        ],
        out_specs=pl.BlockSpec((None, tq, LANES), lambda b, h, i: (b, i, h)),
        out_shape=jax.ShapeDtypeStruct((B, L, ATT_HEADS * ATT_V_DIM), BF16),
        scratch_shapes=[
            pltpu.VMEM((L, LANES), BF16),
            pltpu.VMEM((2 * tq, 1), F32),
            pltpu.VMEM((2 * tq, 1), F32),
            pltpu.VMEM((2 * tq, ATT_V_DIM), F32),
        ],
        compiler_params=pltpu.CompilerParams(
            dimension_semantics=("parallel", "parallel", "arbitrary"), vmem_limit_bytes=VMEM_LIMIT),
        name="diff_attention",
    )(proj3, proj3, proj3, proj3, cosf, sinf, cosf, sinf, lam_qk, subln_w)


DN_HALO = 8
DN_STRIP = 256
DN_B_LANE = 0
DN_A_LANE = DN_HEADS


def _split3(x):
    hi = x.astype(BF16).astype(F32)
    r1 = x - hi
    mid = r1.astype(BF16).astype(F32)
    lo = (r1 - mid).astype(BF16).astype(F32)
    return hi, mid, lo


def _dn_body(q_ref, k_ref, v_ref, z_ref, ba_ref, cw_ref, gp_ref, nw_ref, o_ref,
             xbuf, qkv, s_ref, *, tb):
    C = DN_CHUNK
    W = DN_WIDTH
    hd = DN_HEAD_DIM

    @pl.when(pl.program_id(1) == 0)
    def _():
        xbuf[0:DN_HALO, :] = jnp.zeros((DN_HALO, 3 * W), F32)
        s_ref[...] = jnp.zeros(s_ref.shape, F32)

    xbuf[DN_HALO:DN_HALO + tb, 0:W] = q_ref[...].astype(F32)
    xbuf[DN_HALO:DN_HALO + tb, W:2 * W] = k_ref[...].astype(F32)
    xbuf[DN_HALO:DN_HALO + tb, 2 * W:3 * W] = v_ref[...].astype(F32)

    def conv_step(i, carry):
        r0 = pl.multiple_of(i * C, C)
        for c in range(3 * W // DN_STRIP):
            lanes = slice(c * DN_STRIP, (c + 1) * DN_STRIP)
            qkv[pl.ds(r0, C), lanes] = _silu(
                _causal_taps(xbuf, cw_ref, r0, C, lanes, DN_SHORT_CONV, DN_HALO))
        return carry

    lax.fori_loop(0, tb // C, conv_step, 0)
    xbuf[0:DN_HALO, :] = xbuf[tb:tb + DN_HALO, :]

    row = lax.broadcasted_iota(jnp.int32, (C, C), 0)
    col = lax.broadcasted_iota(jnp.int32, (C, C), 1)
    tri = (row >= col).astype(BF16)
    eye = (row == col).astype(F32)
    lane = lax.broadcasted_iota(jnp.int32, (C, LANES), 1)
    neg_a = -jnp.exp(gp_ref[0:1, :])
    dt_bias = gp_ref[1:2, :]
    norm_w = nw_ref[...]

    def chunk_step(ci, carry):
        r0 = pl.multiple_of(ci * C, C)
        ba = ba_ref[pl.ds(r0, C), :]
        beta_all = jax.nn.sigmoid(ba)
        xa = ba + dt_bias
        softplus = jnp.maximum(xa, 0.0) + jnp.log1p(jnp.exp(-jnp.abs(xa)))
        g = jnp.where((lane >= DN_A_LANE) & (lane < DN_A_LANE + DN_HEADS), neg_a * softplus, 0.0)
        g_hi, g_mid, g_lo = _split3(g)
        gc = (jnp.dot(tri, g_hi.astype(BF16), preferred_element_type=F32)
              + jnp.dot(tri, g_mid.astype(BF16), preferred_element_type=F32)
              + jnp.dot(tri, g_lo.astype(BF16), preferred_element_type=F32))
        g_last_all = gc[C - 1:C, :]

        for h in range(DN_HEADS):
            hl = slice(h * hd, (h + 1) * hd)
            qh = qkv[pl.ds(r0, C), h * hd:(h + 1) * hd]
            kh = qkv[pl.ds(r0, C), W + h * hd:W + (h + 1) * hd]
            vh = qkv[pl.ds(r0, C), 2 * W + h * hd:2 * W + (h + 1) * hd]
            qn = qh * lax.rsqrt(jnp.sum(qh * qh, axis=-1, keepdims=True) + L2_EPS) * (hd ** -0.5)
            kn = kh * lax.rsqrt(jnp.sum(kh * kh, axis=-1, keepdims=True) + L2_EPS)
            bcol = beta_all[:, DN_B_LANE + h:DN_B_LANE + h + 1]
            gcol = gc[:, DN_A_LANE + h:DN_A_LANE + h + 1]
            glast = g_last_all[:, DN_A_LANE + h:DN_A_LANE + h + 1]
            kb = kn * bcol

            p_hi, p_mid, p_lo = _split3(jnp.broadcast_to(gcol, (C, LANES)))
            lhs = jnp.where(lane == 0, p_hi, jnp.where(lane == 1, p_mid, jnp.where(
                lane == 2, p_lo, jnp.where(lane < 6, 1.0, 0.0))))
            rhs = jnp.where(lane < 3, 1.0, jnp.where(lane == 3, -p_hi, jnp.where(
                lane == 4, -p_mid, jnp.where(lane == 5, -p_lo, 0.0))))
            gdiff = _nt_dot(lhs.astype(BF16), rhs.astype(BF16))
            decay = jnp.exp(jnp.where(row >= col, gdiff, -jnp.inf))

            knb = kn.astype(BF16)
            a_mat = jnp.where(row > col, _nt_dot(kb.astype(BF16), knb) * decay, 0.0)
            intra = _nt_dot(qn.astype(BF16), knb) * decay

            m = -a_mat
            p = eye + m
            for _ in range(int(math.log2(C)) - 1):
                mb = m.astype(BF16)
                m = jnp.dot(mb, mb, preferred_element_type=F32)
                p = p + jnp.dot(p.astype(BF16), m.astype(BF16), preferred_element_type=F32)

            eg = jnp.exp(gcol)
            rhs_cat = jnp.concatenate([vh * bcol, kb * eg], axis=1).astype(BF16)
            sol = jnp.dot(p.astype(BF16), rhs_cat, preferred_element_type=F32)
            u = sol[:, :hd]
            w = sol[:, hd:]

            state = s_ref[h]
            sb = state.astype(BF16)
            v_new = u - jnp.dot(w.astype(BF16), sb, preferred_element_type=F32)
            vnb = v_new.astype(BF16)
            o = (jnp.dot((qn * eg).astype(BF16), sb, preferred_element_type=F32)
                 + jnp.dot(intra.astype(BF16), vnb, preferred_element_type=F32))
            k_dec = (kn * jnp.exp(glast - gcol)).astype(BF16)
            s_ref[h] = state * jnp.exp(glast) + lax.dot_general(
                k_dec, vnb, (((0,), (0,)), ((), ())), preferred_element_type=F32)

            ms = jnp.mean(o * o, axis=-1, keepdims=True)
            on = o * lax.rsqrt(ms + RMS_EPS) * norm_w
            zz = z_ref[pl.ds(r0, C), hl].astype(F32)
            o_ref[pl.ds(r0, C), hl] = (on * _silu(zz)).astype(BF16)
        return carry

    lax.fori_loop(0, tb // C, chunk_step, 0)


def _deltanet(proj3, ba3, conv_w, gate_params, norm_w, *, tb=256):
    B, L, _ = proj3.shape
    blk = lambda col: pl.BlockSpec((None, tb, DN_WIDTH), lambda b, t: (b, t, col))
    return pl.pallas_call(
        functools.partial(_dn_body, tb=tb),
        grid=(B, L // tb),
        in_specs=[
            blk(COL_DQ), blk(COL_DK), blk(COL_DV), blk(COL_DZ),
            pl.BlockSpec((None, tb, LANES), lambda b, t: (b, t, 0)),
            pl.BlockSpec((DN_SHORT_CONV, 3 * DN_WIDTH), lambda b, t: (0, 0)),
            pl.BlockSpec((2, LANES), lambda b, t: (0, 0)),
            pl.BlockSpec((1, DN_HEAD_DIM), lambda b, t: (0, 0)),
        ],
        out_specs=pl.BlockSpec((None, tb, DN_WIDTH), lambda b, t: (b, t, 0)),
        out_shape=jax.ShapeDtypeStruct((B, L, DN_WIDTH), BF16),
        scratch_shapes=[
            pltpu.VMEM((DN_HALO + tb, 3 * DN_WIDTH), F32),
            pltpu.VMEM((tb, 3 * DN_WIDTH), F32),
            pltpu.VMEM((DN_HEADS, DN_HEAD_DIM, DN_HEAD_DIM), F32),
        ],
        compiler_params=pltpu.CompilerParams(
            dimension_semantics=("parallel", "arbitrary"), vmem_limit_bytes=VMEM_LIMIT),
        name="gated_deltanet",
    )(proj3, proj3, proj3, proj3, ba3, conv_w, gate_params, norm_w)


def _rope_tables(positions):
    half = ATT_QK_DIM // 2
    inv_freq = ROPE_THETA ** (-jnp.arange(0, ATT_QK_DIM, 2, dtype=F32) / ATT_QK_DIM)
    ang = positions.astype(F32)[..., None] * inv_freq
    cos, sin = jnp.cos(ang), jnp.sin(ang)
    reps = LANES // half
    cosf = jnp.tile(cos, (1, 1, reps))
    sinf = jnp.concatenate([-sin, sin] * (reps // 2), axis=-1)
    return cosf, sinf


def kernel(x, positions, norm_w, w_in, lam_qk, attn_subln_w, w_attn_out, conv_dw_w, conv_dw_b,
           conv_ln_w, conv_ln_b, w_conv_out, dn_conv_w, dn_a_log, dn_dt_bias, dn_norm_w,
           w_dn_out, w_out, final_norm_w):
    B, L, _ = x.shape
    T = B * L
    x2d = x.reshape(T, D_MODEL)
    depth = norm_w.shape[0]
    cosf, sinf = _rope_tables(positions)
    pad_heads = lambda v: jnp.pad(v, (DN_A_LANE, LANES - DN_A_LANE - DN_HEADS))
    for l in range(depth):
        w = w_in[l]
        w_main = jnp.concatenate([w[:, :ORIG_DB], w[:, ORIG_GATE:]], axis=1).astype(BF16)
        w_ba = jnp.pad(w[:, ORIG_DB:ORIG_GATE], ((0, 0), (0, LANES - 2 * DN_HEADS))).astype(BF16)
        proj, ba = _inproj(x2d, norm_w[l][None, :], w_main, w_ba)
        proj3 = proj.reshape(B, L, N_MAIN)
        lambda_init = 0.8 - 0.6 * math.exp(-0.3 * l)
        ya = _attention(proj3, cosf, sinf, lam_qk[l], attn_subln_w[l][None, :],
                        lambda_init=lambda_init).reshape(T, ATT_HEADS * ATT_V_DIM)
        yc = _conv_branch(proj3, conv_dw_w[l], conv_dw_b[l][None, :], conv_ln_w[l][None, :],
                          conv_ln_b[l][None, :]).reshape(T, CONV_WIDTH)
        gate_params = jnp.stack([pad_heads(dn_a_log[l]), pad_heads(dn_dt_bias[l])])
        yd = _deltanet(proj3, ba.reshape(B, L, LANES), dn_conv_w[l], gate_params,
                       dn_norm_w[l][None, :]).reshape(T, DN_WIDTH)
        x2d = _merge(x2d, ya, yc, yd, proj, w_attn_out[l].astype(BF16), w_conv_out[l].astype(BF16),
                     w_dn_out[l].astype(BF16), w_out[l].astype(BF16), final_norm_w[None, :],
                     final_norm=(l == depth - 1))
    return x2d.reshape(B, L, D_MODEL)
```

```python
import functools
import math

import jax
import jax.numpy as jnp
from jax import lax
from jax.experimental import pallas as pl
from jax.experimental.pallas import tpu as pltpu

F32 = jnp.float32
BF16 = jnp.bfloat16

D_MODEL = 1024
ATT_HEADS = 8
ATT_QK_DIM = 64
ATT_V_DIM = 128
ROPE_THETA = 10000.0
CONV_WIDTH = 1024
CONV_KERNEL = 31
DN_HEADS = 8
DN_HEAD_DIM = 128
DN_WIDTH = DN_HEADS * DN_HEAD_DIM
DN_SHORT_CONV = 4
DN_CHUNK = 64
RMS_EPS = 1e-6
LN_EPS = 1e-5
L2_EPS = 1e-6

LANES = 128
SUBLANES = 8
VMEM_LIMIT = 56 * 1024 * 1024

COL_AQ, COL_AK, COL_AV, COL_AZ = 0, 1, 2, 3
COL_CA, COL_CG, COL_CZ = 4, 5, 6
COL_DQ, COL_DK, COL_DV, COL_DZ = 7, 8, 9, 10
COL_GATE = 11
N_MAIN = 14 * D_MODEL
ORIG_DB = 11 * D_MODEL
ORIG_GATE = ORIG_DB + 2 * DN_HEADS


def _silu(x):
    return x * jax.nn.sigmoid(x)


def _inproj_body(x_ref, nw_ref, w_ref, wba_ref, o_ref, ba_ref, h_ref):
    @pl.when(pl.program_id(1) == 0)
    def _():
        x = x_ref[...]
        ms = jnp.mean(x * x, axis=-1, keepdims=True)
        h = (x * lax.rsqrt(ms + RMS_EPS) * nw_ref[...]).astype(BF16)
        h_ref[...] = h
        ba_ref[...] = jnp.dot(h, wba_ref[...], preferred_element_type=F32)

    o_ref[...] = jnp.dot(h_ref[...], w_ref[...], preferred_element_type=F32).astype(BF16)


def _inproj(x2d, norm_w, w_main, w_ba, *, tm=1024, tn=1024):
    T = x2d.shape[0]
    grid = (T // tm, N_MAIN // tn)
    return pl.pallas_call(
        _inproj_body,
        grid=grid,
        in_specs=[
            pl.BlockSpec((tm, D_MODEL), lambda i, j: (i, 0)),
            pl.BlockSpec((1, D_MODEL), lambda i, j: (0, 0)),
            pl.BlockSpec((D_MODEL, tn), lambda i, j: (0, j)),
            pl.BlockSpec((D_MODEL, LANES), lambda i, j: (0, 0)),
        ],
        out_specs=[
            pl.BlockSpec((tm, tn), lambda i, j: (i, j)),
            pl.BlockSpec((tm, LANES), lambda i, j: (i, 0)),
        ],
        out_shape=[
            jax.ShapeDtypeStruct((T, N_MAIN), BF16),
            jax.ShapeDtypeStruct((T, LANES), F32),
        ],
        scratch_shapes=[pltpu.VMEM((tm, D_MODEL), BF16)],
        compiler_params=pltpu.CompilerParams(
            dimension_semantics=("parallel", "arbitrary"), vmem_limit_bytes=VMEM_LIMIT),
        name="inproj",
    )(x2d, norm_w, w_main, w_ba)


def _merge_body(x_ref, a_ref, c_ref, d_ref, ga_ref, gc_ref, gd_ref,
                wa_ref, wc_ref, wd_ref, wo_ref, fw_ref, o_ref, *, final_norm):
    ya = jnp.dot(a_ref[...], wa_ref[...], preferred_element_type=F32)
    yc = jnp.dot(c_ref[...], wc_ref[...], preferred_element_type=F32)
    yd = jnp.dot(d_ref[...], wd_ref[...], preferred_element_type=F32)
    merged = (jax.nn.sigmoid(ga_ref[...].astype(F32)) * ya
              + jax.nn.sigmoid(gc_ref[...].astype(F32)) * yc
              + jax.nn.sigmoid(gd_ref[...].astype(F32)) * yd)
    y = x_ref[...] + jnp.dot(merged.astype(BF16), wo_ref[...], preferred_element_type=F32)
    if final_norm:
        ms = jnp.mean(y * y, axis=-1, keepdims=True)
        y = y * lax.rsqrt(ms + RMS_EPS) * fw_ref[...]
    o_ref[...] = y


def _merge(x2d, ya, yc, yd, proj, wa, wc, wd, wo, final_w, *, final_norm, tm=512):
    T = x2d.shape[0]
    row = lambda i: (i, 0)
    const = lambda i: (0, 0)
    wspec = pl.BlockSpec((D_MODEL, D_MODEL), const)
    return pl.pallas_call(
        functools.partial(_merge_body, final_norm=final_norm),
        grid=(T // tm,),
        in_specs=[
            pl.BlockSpec((tm, D_MODEL), row),
            pl.BlockSpec((tm, D_MODEL), row),
            pl.BlockSpec((tm, D_MODEL), row),
            pl.BlockSpec((tm, D_MODEL), row),
            pl.BlockSpec((tm, D_MODEL), lambda i: (i, COL_GATE)),
            pl.BlockSpec((tm, D_MODEL), lambda i: (i, COL_GATE + 1)),
            pl.BlockSpec((tm, D_MODEL), lambda i: (i, COL_GATE + 2)),
            wspec, wspec, wspec, wspec,
            pl.BlockSpec((1, D_MODEL), const),
        ],
        out_specs=pl.BlockSpec((tm, D_MODEL), row),
        out_shape=jax.ShapeDtypeStruct((T, D_MODEL), F32),
        compiler_params=pltpu.CompilerParams(
            dimension_semantics=("parallel",), vmem_limit_bytes=VMEM_LIMIT),
        name="merge",
    )(x2d, ya, yc, yd, proj, proj, proj, wa, wc, wd, wo, final_w)


CONV_HALO = 32
CONV_ROWS = 64
CONV_STRIP = 256


def _causal_taps(buf_ref, w_ref, r0, rows, lanes, n_taps, halo):
    assert halo >= SUBLANES * ((n_taps - 1) // SUBLANES + 1)
    acc = None
    for r in range(min(SUBLANES, n_taps)):
        part = None
        for a in range((n_taps - 1 - r) // SUBLANES + 1):
            s = SUBLANES * a + r
            start = pl.multiple_of(r0 + (halo - SUBLANES * (a + 1)), SUBLANES)
            term = (w_ref[n_taps - 1 - s:n_taps - s, lanes]
                    * buf_ref[pl.ds(start, rows + SUBLANES), lanes])
            part = term if part is None else part + term
        if r:
            part = pltpu.roll(part, r, axis=0)
        acc = part if acc is None else acc + part
    return acc[SUBLANES:, :]


def _conv_body(a_ref, g_ref, z_ref, w_ref, b_ref, lnw_ref, lnb_ref, o_ref, ubuf, cbuf, *, tl):
    @pl.when(pl.program_id(1) == 0)
    def _():
        ubuf[0:CONV_HALO, :] = jnp.zeros((CONV_HALO, CONV_WIDTH), F32)

    a = a_ref[...].astype(F32)
    g = g_ref[...].astype(F32)
    ubuf[CONV_HALO:CONV_HALO + tl, :] = a * jax.nn.sigmoid(g)

    bias = b_ref[...]
    lnw = lnw_ref[...]
    lnb = lnb_ref[...]

    def step(r, carry):
        r0 = pl.multiple_of(r * CONV_ROWS, CONV_ROWS)
        for c in range(CONV_WIDTH // CONV_STRIP):
            lanes = slice(c * CONV_STRIP, (c + 1) * CONV_STRIP)
            cbuf[:, lanes] = _causal_taps(ubuf, w_ref, r0, CONV_ROWS, lanes, CONV_KERNEL, CONV_HALO)
        u = cbuf[...] + bias
        mu = jnp.mean(u, axis=-1, keepdims=True)
        uc = u - mu
        var = jnp.mean(uc * uc, axis=-1, keepdims=True)
        y = uc * lax.rsqrt(var + LN_EPS) * lnw + lnb
        zz = z_ref[pl.ds(r0, CONV_ROWS), :].astype(F32)
        o_ref[pl.ds(r0, CONV_ROWS), :] = (_silu(y) * _silu(zz)).astype(BF16)
        return carry

    lax.fori_loop(0, tl // CONV_ROWS, step, 0)
    ubuf[0:CONV_HALO, :] = ubuf[tl:tl + CONV_HALO, :]


def _conv_branch(proj3, dw_w, dw_b, ln_w, ln_b, *, tl=512):
    B, L, _ = proj3.shape
    vec = pl.BlockSpec((1, CONV_WIDTH), lambda b, t: (0, 0))
    return pl.pallas_call(
        functools.partial(_conv_body, tl=tl),
        grid=(B, L // tl),
        in_specs=[
            pl.BlockSpec((None, tl, CONV_WIDTH), lambda b, t: (b, t, COL_CA)),
            pl.BlockSpec((None, tl, CONV_WIDTH), lambda b, t: (b, t, COL_CG)),
            pl.BlockSpec((None, tl, CONV_WIDTH), lambda b, t: (b, t, COL_CZ)),
            pl.BlockSpec((CONV_KERNEL, CONV_WIDTH), lambda b, t: (0, 0)),
            vec, vec, vec,
        ],
        out_specs=pl.BlockSpec((None, tl, CONV_WIDTH), lambda b, t: (b, t, 0)),
        out_shape=jax.ShapeDtypeStruct((B, L, CONV_WIDTH), BF16),
        scratch_shapes=[pltpu.VMEM((CONV_HALO + tl, CONV_WIDTH), F32),
                        pltpu.VMEM((CONV_ROWS, CONV_WIDTH), F32)],
        compiler_params=pltpu.CompilerParams(
            dimension_semantics=("parallel", "arbitrary"), vmem_limit_bytes=VMEM_LIMIT),
        name="conv_branch",
    )(proj3, proj3, proj3, dw_w, dw_b, ln_w, ln_b)


def _nt_dot(a, b):
    return lax.dot_general(a, b, (((1,), (1,)), ((), ())), preferred_element_type=F32)


def _rope(t, cos, sin_signed):
    lane = lax.broadcasted_iota(jnp.int32, t.shape, 1)
    first_half = (lane % ATT_QK_DIM) < (ATT_QK_DIM // 2)
    half = ATT_QK_DIM // 2
    swapped = jnp.where(first_half, pltpu.roll(t, LANES - half, axis=1), pltpu.roll(t, half, axis=1))
    return t * cos + swapped * sin_signed


def _attn_body(q_ref, k_ref, v_ref, z_ref, cq_ref, sq_ref, ck_ref, sk_ref, lam_ref, sw_ref, o_ref,
               kr_ref, vt_ref, m_ref, l_ref, alpha_ref, acc_ref, s_ref, p_ref, *, lambda_init, tq):
    qi = pl.program_id(2)

    @pl.when(qi == 0)
    def _():
        kr_ref[...] = _rope(k_ref[...].astype(F32), ck_ref[...], sk_ref[...]).astype(BF16)

        ri = lax.broadcasted_iota(jnp.int32, (ATT_V_DIM, ATT_V_DIM), 0)
        ci = lax.broadcasted_iota(jnp.int32, (ATT_V_DIM, ATT_V_DIM), 1)
        ident = (ri == ci).astype(BF16)
        for jb in range(vt_ref.shape[0]):
            vt_ref[jb] = _nt_dot(ident, v_ref[jb * tq:(jb + 1) * tq, :]).astype(BF16)

    q = _rope(q_ref[...].astype(F32), cq_ref[...], sq_ref[...]) * (ATT_QK_DIM ** -0.5)
    lane = lax.broadcasted_iota(jnp.int32, q.shape, 1)
    qs = jnp.concatenate([jnp.where(lane < ATT_QK_DIM, q, 0.0),
                          jnp.where(lane >= ATT_QK_DIM, q, 0.0)], axis=0).astype(BF16)

    def scores(j):
        k0 = pl.multiple_of(j * tq, tq)
        return _nt_dot(kr_ref[pl.ds(k0, tq), :], qs)

    def softmax_step(s):
        m_prev = m_ref[...]
        m_new = jnp.maximum(m_prev, jnp.max(s, axis=0, keepdims=True))
        alpha = jnp.exp(m_prev - m_new)
        p = jnp.exp(s - m_new)
        l_ref[...] = alpha * l_ref[...] + jnp.sum(p, axis=0, keepdims=True)
        m_ref[...] = m_new
        return alpha, p.astype(BF16)

    def pv(j, p):
        return jnp.dot(vt_ref[j], p, preferred_element_type=F32)

    m_ref[...] = jnp.full(m_ref.shape, -jnp.inf, F32)
    l_ref[...] = jnp.zeros(l_ref.shape, F32)
    acc_ref[...] = jnp.zeros(acc_ref.shape, F32)
    alpha_ref[...] = jnp.ones(alpha_ref.shape, F32)
    p_ref[1] = jnp.zeros(p_ref.shape[1:], BF16)
    s_ref[0] = scores(0)

    def trip(t, carry):
        slot = t % 2
        pv_prev = pv(jnp.maximum(t - 1, 0), p_ref[1 - slot])
        s_next = scores(t + 1)
        alpha, p = softmax_step(s_ref[slot])
        p_ref[slot] = p
        acc_ref[...] = alpha_ref[...] * acc_ref[...] + pv_prev
        alpha_ref[...] = alpha
        s_ref[1 - slot] = s_next
        return carry

    lax.fori_loop(0, qi, trip, 0)

    slot = qi % 2
    pv_prev = pv(jnp.maximum(qi - 1, 0), p_ref[1 - slot])
    s = s_ref[slot]
    key = lax.broadcasted_iota(jnp.int32, s.shape, 0)
    qry = lax.broadcasted_iota(jnp.int32, s.shape, 1) % tq
    alpha, p = softmax_step(jnp.where(key <= qry, s, -jnp.inf))
    acc = alpha_ref[...] * acc_ref[...] + pv_prev
    acc_ref[...] = alpha * acc + pv(qi, p)

    lq = lam_ref[...]
    lam = (jnp.exp(jnp.sum(lq[0:1, :] * lq[1:2, :], axis=-1, keepdims=True))
           - jnp.exp(jnp.sum(lq[2:3, :] * lq[3:4, :], axis=-1, keepdims=True)) + lambda_init)
    o_all = acc_ref[...] / l_ref[...]
    ot = o_all[:, :tq] - lam * o_all[:, tq:]
    ms = jnp.mean(ot * ot, axis=0, keepdims=True)
    on = (ot * lax.rsqrt(ms + RMS_EPS) * sw_ref[...] * (1.0 - lambda_init)).T
    o_ref[...] = (on * _silu(z_ref[...].astype(F32))).astype(BF16)


def _attention(proj3, cosf, sinf, lam_qk, subln_w, *, lambda_init, tq=256):
    B, L, _ = proj3.shape
    hpb = D_MODEL // LANES
    qblk = lambda col: pl.BlockSpec((None, tq, LANES), lambda b, h, i: (b, i, col * hpb + h))
    full = lambda col: pl.BlockSpec((None, L, LANES), lambda b, h, i: (b, 0, col * hpb + h))
    return pl.pallas_call(
        functools.partial(_attn_body, lambda_init=lambda_init, tq=tq),
        grid=(B, ATT_HEADS, L // tq),
        in_specs=[
            qblk(COL_AQ), full(COL_AK), full(COL_AV), qblk(COL_AZ),
            pl.BlockSpec((None, tq, LANES), lambda b, h, i: (b, i, 0)),
            pl.BlockSpec((None, tq, LANES), lambda b, h, i: (b, i, 0)),
            pl.BlockSpec((None, L, LANES), lambda b, h, i: (b, 0, 0)),
            pl.BlockSpec((None, L, LANES), lambda b, h, i: (b, 0, 0)),
            pl.BlockSpec((4, ATT_QK_DIM), lambda b, h, i: (0, 0)),
            pl.BlockSpec((ATT_V_DIM, 1), lambda b, h, i: (0, 0)),
        ],
        out_specs=pl.BlockSpec((None, tq, LANES), lambda b, h, i: (b, i, h)),
        out_shape=jax.ShapeDtypeStruct((B, L, ATT_HEADS * ATT_V_DIM), BF16),
        scratch_shapes=[
            pltpu.VMEM((L, LANES), BF16),
            pltpu.VMEM((L // tq, ATT_V_DIM, tq), BF16),
            pltpu.VMEM((1, 2 * tq), F32),
            pltpu.VMEM((1, 2 * tq), F32),
            pltpu.VMEM((1, 2 * tq), F32),
            pltpu.VMEM((ATT_V_DIM, 2 * tq), F32),
            pltpu.VMEM((2, tq, 2 * tq), F32),
            pltpu.VMEM((2, tq, 2 * tq), BF16),
        ],
        compiler_params=pltpu.CompilerParams(
            dimension_semantics=("parallel", "parallel", "arbitrary"), vmem_limit_bytes=VMEM_LIMIT),
        name="diff_attention",
    )(proj3, proj3, proj3, proj3, cosf, sinf, cosf, sinf, lam_qk, subln_w)


DN_HALO = 8
DN_STRIP = 256
DN_B_LANE = 0
DN_A_LANE = DN_HEADS
DN_ONE_LANE = DN_A_LANE + 3 * DN_HEADS
DN_GROUP = 4


def _split3(x):
    hi = x.astype(BF16).astype(F32)
    r1 = x - hi
    mid = r1.astype(BF16).astype(F32)
    lo = (r1 - mid).astype(BF16).astype(F32)
    return hi, mid, lo


def _dn_body(q_ref, k_ref, v_ref, z_ref, ba_ref, cw_ref, gp_ref, nw_ref, o_ref,
             xbuf, qkv, u_s, w_s, qg_s, kd_s, in_s, dec_s, s_ref, *, tb):
    C = DN_CHUNK
    W = DN_WIDTH
    hd = DN_HEAD_DIM

    @pl.when(pl.program_id(1) == 0)
    def _():
        xbuf[0:DN_HALO, :] = jnp.zeros((DN_HALO, 3 * W), F32)
        s_ref[...] = jnp.zeros(s_ref.shape, F32)

    xbuf[DN_HALO:DN_HALO + tb, 0:W] = q_ref[...].astype(F32)
    xbuf[DN_HALO:DN_HALO + tb, W:2 * W] = k_ref[...].astype(F32)
    xbuf[DN_HALO:DN_HALO + tb, 2 * W:3 * W] = v_ref[...].astype(F32)

    def conv_step(i, carry):
        r0 = pl.multiple_of(i * C, C)
        for c in range(3 * W // DN_STRIP):
            lanes = slice(c * DN_STRIP, (c + 1) * DN_STRIP)
            qkv[pl.ds(r0, C), lanes] = _silu(
                _causal_taps(xbuf, cw_ref, r0, C, lanes, DN_SHORT_CONV, DN_HALO))
        return carry

    lax.fori_loop(0, tb // C, conv_step, 0)
    xbuf[0:DN_HALO, :] = xbuf[tb:tb + DN_HALO, :]

    n_chunks = tb // C
    G = DN_GROUP
    n_groups = DN_HEADS // G
    gw = G * hd
    gc_w = G * C

    row = lax.broadcasted_iota(jnp.int32, (C, C), 0)
    col = lax.broadcasted_iota(jnp.int32, (C, C), 1)
    tri = (row >= col).astype(BF16)
    lane = lax.broadcasted_iota(jnp.int32, (C, LANES), 1)
    ri = lax.broadcasted_iota(jnp.int32, (C, gc_w), 0)
    cj = lax.broadcasted_iota(jnp.int32, (C, gc_w), 1) % C
    eye_cat = (ri == cj).astype(F32)
    bd_sq = (lax.broadcasted_iota(jnp.int32, (gc_w, gc_w), 0) // C
             == lax.broadcasted_iota(jnp.int32, (gc_w, gc_w), 1) // C)
    bd_wide = (lax.broadcasted_iota(jnp.int32, (gc_w, gw), 0) // C
               == lax.broadcasted_iota(jnp.int32, (gc_w, gw), 1) // hd)
    neg_a = -jnp.exp(gp_ref[0:1, :])
    dt_bias = gp_ref[1:2, :]
    norm_w = nw_ref[...]

    def block_diag(x, mask):
        return jnp.where(mask, jnp.tile(x, (G, 1)), 0.0).astype(BF16)

    def bdot(a, b):
        return jnp.dot(a, b, preferred_element_type=F32)

    probs = [(ci, g) for ci in range(n_chunks) for g in range(n_groups)]

    gd_lhs, gcb_all, kq_lhs, kn_grp, vb_grp, kbe_grp = {}, {}, {}, {}, {}, {}
    for ci in range(n_chunks):
        r0 = ci * C
        ba = ba_ref[r0:r0 + C, :]
        beta_all = jax.nn.sigmoid(ba)
        xa = ba + dt_bias
        softplus = jnp.maximum(xa, 0.0) + jnp.log1p(jnp.exp(-jnp.abs(xa)))
        g = jnp.where((lane >= DN_A_LANE) & (lane < DN_A_LANE + DN_HEADS), neg_a * softplus, 0.0)
        g_hi, g_mid, g_lo = _split3(g)
        gc = (bdot(tri, g_hi.astype(BF16)) + bdot(tri, g_mid.astype(BF16))
              + bdot(tri, g_lo.astype(BF16)))
        c_hi, c_mid, c_lo = _split3(gc)
        gd_lhs[ci] = (c_hi + pltpu.roll(c_mid, DN_HEADS, axis=1) + pltpu.roll(c_lo, 2 * DN_HEADS, axis=1)
                      + jnp.where((lane >= DN_ONE_LANE) & (lane < DN_ONE_LANE + 3), 1.0, 0.0)).astype(BF16)

        beta_b = jnp.concatenate(
            [jnp.broadcast_to(beta_all[:, DN_B_LANE + h:DN_B_LANE + h + 1], (C, hd))
             for h in range(DN_HEADS)], axis=1)
        gcb = jnp.concatenate(
            [jnp.broadcast_to(gc[:, DN_A_LANE + h:DN_A_LANE + h + 1], (C, hd))
             for h in range(DN_HEADS)], axis=1)
        gcb_all[ci] = gcb
        eg = jnp.exp(gcb)
        glast = gcb[C - 1:C, :]
        dec_s[ci] = jnp.exp(glast)

        q_all = qkv[r0:r0 + C, 0:W]
        k_all = qkv[r0:r0 + C, W:2 * W]
        v_all = qkv[r0:r0 + C, 2 * W:3 * W]
        qn = jnp.concatenate(
            [q_all[:, h * hd:(h + 1) * hd] * lax.rsqrt(jnp.sum(
                q_all[:, h * hd:(h + 1) * hd] ** 2, axis=-1, keepdims=True) + L2_EPS)
             for h in range(DN_HEADS)], axis=1) * (hd ** -0.5)
        kn = jnp.concatenate(
            [k_all[:, h * hd:(h + 1) * hd] * lax.rsqrt(jnp.sum(
                k_all[:, h * hd:(h + 1) * hd] ** 2, axis=-1, keepdims=True) + L2_EPS)
             for h in range(DN_HEADS)], axis=1)
        kb = kn * beta_b
        qg_s[r0:r0 + C, :] = (qn * eg).astype(BF16)
        kd_s[r0:r0 + C, :] = (kn * jnp.exp(glast - gcb)).astype(BF16)
        for gi in range(n_groups):
            gl = slice(gi * gw, (gi + 1) * gw)
            kq_lhs[ci, gi] = jnp.concatenate([kb[:, gl], qn[:, gl]], axis=0).astype(BF16)
            kn_grp[ci, gi] = kn[:, gl]
            vb_grp[ci, gi] = (v_all * beta_b)[:, gl]
            kbe_grp[ci, gi] = (kb * eg)[:, gl]

    gdiff, kq = {}, {}
    for ci, gi in probs:
        blocks = []
        for h in range(gi * G, (gi + 1) * G):
            p_hi, p_mid, p_lo = _split3(gcb_all[ci][:, h * hd:(h + 1) * hd])
            onehot = ((lane == DN_A_LANE + h) | (lane == DN_A_LANE + DN_HEADS + h)
                      | (lane == DN_A_LANE + 2 * DN_HEADS + h))
            blocks.append(jnp.where(lane == DN_ONE_LANE, -p_hi, jnp.where(
                lane == DN_ONE_LANE + 1, -p_mid, jnp.where(
                    lane == DN_ONE_LANE + 2, -p_lo, jnp.where(onehot, 1.0, 0.0)))))
        gd_rhs = jnp.concatenate(blocks, axis=0).astype(BF16)
        gdiff[ci, gi] = _nt_dot(gd_lhs[ci], gd_rhs)
        kq[ci, gi] = _nt_dot(kq_lhs[ci, gi], block_diag(kn_grp[ci, gi], bd_wide))

    m, p = {}, {}
    for pr in probs:
        decay = jnp.exp(jnp.where(ri >= cj, gdiff[pr], -jnp.inf))
        m[pr] = -jnp.where(ri > cj, kq[pr][:C, :] * decay, 0.0)
        ci, gi = pr
        in_s[ci * C:(ci + 1) * C, gi * gc_w:(gi + 1) * gc_w] = (kq[pr][C:, :] * decay).astype(BF16)
        p[pr] = eye_cat + m[pr]

    n_levels = int(math.log2(C)) - 1
    for pr in probs:
        m[pr] = bdot(m[pr].astype(BF16), block_diag(m[pr], bd_sq))
    for lvl in range(n_levels):
        last = lvl == n_levels - 1
        res = {}
        for pr in probs:
            lhs = p[pr] if last else jnp.concatenate([p[pr], m[pr]], axis=0)
            res[pr] = bdot(lhs.astype(BF16), block_diag(m[pr], bd_sq))
        for pr in probs:
            p[pr] = p[pr] + res[pr][:C, :]
            if not last:
                m[pr] = res[pr][C:, :]

    for pr in probs:
        ci, gi = pr
        pb = p[pr].astype(BF16)
        rows = slice(ci * C, (ci + 1) * C)
        gl = slice(gi * gw, (gi + 1) * gw)
        u_s[rows, gl] = bdot(pb, block_diag(vb_grp[pr], bd_wide))
        w_s[rows, gl] = bdot(pb, block_diag(kbe_grp[pr], bd_wide)).astype(BF16)

    def chunk_step(ci, carry):
        r0 = pl.multiple_of(ci * C, C)
        rows = pl.ds(r0, C)
        sb = [s_ref[h].astype(BF16) for h in range(DN_HEADS)]
        ws = [bdot(jnp.concatenate([w_s[rows, h * hd:(h + 1) * hd], qg_s[rows, h * hd:(h + 1) * hd]],
                                   axis=0), sb[h]) for h in range(DN_HEADS)]
        v_new = u_s[rows, :] - jnp.concatenate([x[:C, :] for x in ws], axis=1)
        vnb = v_new.astype(BF16)
        o_intra = [bdot(in_s[rows, gi * gc_w:(gi + 1) * gc_w],
                        block_diag(v_new[:, gi * gw:(gi + 1) * gw], bd_wide))
                   for gi in range(n_groups)]
        dec = dec_s[ci]
        for h in range(DN_HEADS):
            hl = slice(h * hd, (h + 1) * hd)
            s_ref[h] = s_ref[h] * dec[:, hl] + lax.dot_general(
                kd_s[rows, hl], vnb[:, hl], (((0,), (0,)), ((), ())), preferred_element_type=F32)
        o = jnp.concatenate([x[C:, :] for x in ws], axis=1) + jnp.concatenate(o_intra, axis=1)
        for h in range(DN_HEADS):
            hl = slice(h * hd, (h + 1) * hd)
            oh = o[:, hl]
            ms = jnp.mean(oh * oh, axis=-1, keepdims=True)
            on = oh * lax.rsqrt(ms + RMS_EPS) * norm_w
            o_ref[rows, hl] = (on * _silu(z_ref[rows, hl].astype(F32))).astype(BF16)
        return carry

    lax.fori_loop(0, n_chunks, chunk_step, 0)


def _deltanet(proj3, ba3, conv_w, gate_params, norm_w, *, tb=256):
    B, L, _ = proj3.shape
    blk = lambda col: pl.BlockSpec((None, tb, DN_WIDTH), lambda b, t: (b, t, col))
    return pl.pallas_call(
        functools.partial(_dn_body, tb=tb),
        grid=(B, L // tb),
        in_specs=[
            blk(COL_DQ), blk(COL_DK), blk(COL_DV), blk(COL_DZ),
            pl.BlockSpec((None, tb, LANES), lambda b, t: (b, t, 0)),
            pl.BlockSpec((DN_SHORT_CONV, 3 * DN_WIDTH), lambda b, t: (0, 0)),
            pl.BlockSpec((2, LANES), lambda b, t: (0, 0)),
            pl.BlockSpec((1, DN_HEAD_DIM), lambda b, t: (0, 0)),
        ],
        out_specs=pl.BlockSpec((None, tb, DN_WIDTH), lambda b, t: (b, t, 0)),
        out_shape=jax.ShapeDtypeStruct((B, L, DN_WIDTH), BF16),
        scratch_shapes=[
            pltpu.VMEM((DN_HALO + tb, 3 * DN_WIDTH), F32),
            pltpu.VMEM((tb, 3 * DN_WIDTH), F32),
            pltpu.VMEM((tb, DN_WIDTH), F32),
            pltpu.VMEM((tb, DN_WIDTH), BF16),
            pltpu.VMEM((tb, DN_WIDTH), BF16),
            pltpu.VMEM((tb, DN_WIDTH), BF16),
            pltpu.VMEM((tb, DN_HEADS * DN_CHUNK), BF16),
            pltpu.VMEM((tb // DN_CHUNK, 1, DN_WIDTH), F32),
            pltpu.VMEM((DN_HEADS, DN_HEAD_DIM, DN_HEAD_DIM), F32),
        ],
        compiler_params=pltpu.CompilerParams(
            dimension_semantics=("parallel", "arbitrary"), vmem_limit_bytes=VMEM_LIMIT),
        name="gated_deltanet",
    )(proj3, proj3, proj3, proj3, ba3, conv_w, gate_params, norm_w)


def _rope_tables(positions):
    half = ATT_QK_DIM // 2
    inv_freq = ROPE_THETA ** (-jnp.arange(0, ATT_QK_DIM, 2, dtype=F32) / ATT_QK_DIM)
    ang = positions.astype(F32)[..., None] * inv_freq
    cos, sin = jnp.cos(ang), jnp.sin(ang)
    reps = LANES // half
    cosf = jnp.tile(cos, (1, 1, reps))
    sinf = jnp.concatenate([-sin, sin] * (reps // 2), axis=-1)
    return cosf, sinf


def kernel(x, positions, norm_w, w_in, lam_qk, attn_subln_w, w_attn_out, conv_dw_w, conv_dw_b,
           conv_ln_w, conv_ln_b, w_conv_out, dn_conv_w, dn_a_log, dn_dt_bias, dn_norm_w,
           w_dn_out, w_out, final_norm_w):
    B, L, _ = x.shape
    T = B * L
    x2d = x.reshape(T, D_MODEL)
    depth = norm_w.shape[0]
    cosf, sinf = _rope_tables(positions)
    pad_heads = lambda v: jnp.pad(v, (DN_A_LANE, LANES - DN_A_LANE - DN_HEADS))
    for l in range(depth):
        w = w_in[l]
        w_main = jnp.concatenate([w[:, :ORIG_DB], w[:, ORIG_GATE:]], axis=1).astype(BF16)
        w_ba = jnp.pad(w[:, ORIG_DB:ORIG_GATE], ((0, 0), (0, LANES - 2 * DN_HEADS))).astype(BF16)
        proj, ba = _inproj(x2d, norm_w[l][None, :], w_main, w_ba)
        proj3 = proj.reshape(B, L, N_MAIN)
        lambda_init = 0.8 - 0.6 * math.exp(-0.3 * l)
        ya = _attention(proj3, cosf, sinf, lam_qk[l], attn_subln_w[l][:, None],
                        lambda_init=lambda_init).reshape(T, ATT_HEADS * ATT_V_DIM)
        yc = _conv_branch(proj3, conv_dw_w[l], conv_dw_b[l][None, :], conv_ln_w[l][None, :],
                          conv_ln_b[l][None, :]).reshape(T, CONV_WIDTH)
        gate_params = jnp.stack([pad_heads(dn_a_log[l]), pad_heads(dn_dt_bias[l])])
        yd = _deltanet(proj3, ba.reshape(B, L, LANES), dn_conv_w[l], gate_params,
                       dn_norm_w[l][None, :]).reshape(T, DN_WIDTH)
        x2d = _merge(x2d, ya, yc, yd, proj, w_attn_out[l].astype(BF16), w_conv_out[l].astype(BF16),
                     w_dn_out[l].astype(BF16), w_out[l].astype(BF16), final_norm_w[None, :],
                     final_norm=(l == depth - 1))
    return x2d.reshape(B, L, D_MODEL)
```

```python
import functools
import math

import jax
import jax.numpy as jnp
from jax import lax
from jax.experimental import pallas as pl
from jax.experimental.pallas import tpu as pltpu

F32 = jnp.float32
BF16 = jnp.bfloat16

D_MODEL = 1024
ATT_HEADS = 8
ATT_QK_DIM = 64
ATT_V_DIM = 128
ROPE_THETA = 10000.0
CONV_WIDTH = 1024
CONV_KERNEL = 31
DN_HEADS = 8
DN_HEAD_DIM = 128
DN_WIDTH = DN_HEADS * DN_HEAD_DIM
DN_SHORT_CONV = 4
DN_CHUNK = 64
RMS_EPS = 1e-6
LN_EPS = 1e-5
L2_EPS = 1e-6

LANES = 128
SUBLANES = 8
VMEM_LIMIT = 56 * 1024 * 1024

COL_AQ, COL_AK, COL_AV, COL_AZ = 0, 1, 2, 3
COL_CA, COL_CG, COL_CZ = 4, 5, 6
COL_DQ, COL_DK, COL_DV, COL_DZ = 7, 8, 9, 10
COL_GATE = 11
N_MAIN = 14 * D_MODEL
ORIG_DB = 11 * D_MODEL
ORIG_GATE = ORIG_DB + 2 * DN_HEADS


def _silu(x):
    return x * jax.nn.sigmoid(x)


def _inproj_body(x_ref, nw_ref, w_ref, wba_ref, o_ref, ba_ref, h_ref):
    @pl.when(pl.program_id(1) == 0)
    def _():
        x = x_ref[...]
        ms = jnp.mean(x * x, axis=-1, keepdims=True)
        h = (x * lax.rsqrt(ms + RMS_EPS) * nw_ref[...]).astype(BF16)
        h_ref[...] = h
        ba_ref[...] = jnp.dot(h, wba_ref[...], preferred_element_type=F32)

    o_ref[...] = jnp.dot(h_ref[...], w_ref[...], preferred_element_type=F32).astype(BF16)


def _inproj(x2d, norm_w, w_main, w_ba, *, tm=2048, tn=1024):
    T = x2d.shape[0]
    grid = (T // tm, N_MAIN // tn)
    return pl.pallas_call(
        _inproj_body,
        grid=grid,
        in_specs=[
            pl.BlockSpec((tm, D_MODEL), lambda i, j: (i, 0)),
            pl.BlockSpec((1, D_MODEL), lambda i, j: (0, 0)),
            pl.BlockSpec((D_MODEL, tn), lambda i, j: (0, j)),
            pl.BlockSpec((D_MODEL, LANES), lambda i, j: (0, 0)),
        ],
        out_specs=[
            pl.BlockSpec((tm, tn), lambda i, j: (i, j)),
            pl.BlockSpec((tm, LANES), lambda i, j: (i, 0)),
        ],
        out_shape=[
            jax.ShapeDtypeStruct((T, N_MAIN), BF16),
            jax.ShapeDtypeStruct((T, LANES), F32),
        ],
        scratch_shapes=[pltpu.VMEM((tm, D_MODEL), BF16)],
        compiler_params=pltpu.CompilerParams(
            dimension_semantics=("parallel", "arbitrary"), vmem_limit_bytes=VMEM_LIMIT),
        name="inproj",
    )(x2d, norm_w, w_main, w_ba)


def _merge_body(x_ref, a_ref, c_ref, d_ref, ga_ref, gc_ref, gd_ref,
                wa_ref, wc_ref, wd_ref, wo_ref, fw_ref, o_ref, *, final_norm):
    ya = jnp.dot(a_ref[...], wa_ref[...], preferred_element_type=F32)
    yc = jnp.dot(c_ref[...], wc_ref[...], preferred_element_type=F32)
    yd = jnp.dot(d_ref[...], wd_ref[...], preferred_element_type=F32)
    merged = (jax.nn.sigmoid(ga_ref[...].astype(F32)) * ya
              + jax.nn.sigmoid(gc_ref[...].astype(F32)) * yc
              + jax.nn.sigmoid(gd_ref[...].astype(F32)) * yd)
    y = x_ref[...] + jnp.dot(merged.astype(BF16), wo_ref[...], preferred_element_type=F32)
    if final_norm:
        ms = jnp.mean(y * y, axis=-1, keepdims=True)
        y = y * lax.rsqrt(ms + RMS_EPS) * fw_ref[...]
    o_ref[...] = y


def _merge(x2d, ya, yc, yd, proj, wa, wc, wd, wo, final_w, *, final_norm, tm=512):
    T = x2d.shape[0]
    row = lambda i: (i, 0)
    const = lambda i: (0, 0)
    wspec = pl.BlockSpec((D_MODEL, D_MODEL), const)
    return pl.pallas_call(
        functools.partial(_merge_body, final_norm=final_norm),
        grid=(T // tm,),
        in_specs=[
            pl.BlockSpec((tm, D_MODEL), row),
            pl.BlockSpec((tm, D_MODEL), row),
            pl.BlockSpec((tm, D_MODEL), row),
            pl.BlockSpec((tm, D_MODEL), row),
            pl.BlockSpec((tm, D_MODEL), lambda i: (i, COL_GATE)),
            pl.BlockSpec((tm, D_MODEL), lambda i: (i, COL_GATE + 1)),
            pl.BlockSpec((tm, D_MODEL), lambda i: (i, COL_GATE + 2)),
            wspec, wspec, wspec, wspec,
            pl.BlockSpec((1, D_MODEL), const),
        ],
        out_specs=pl.BlockSpec((tm, D_MODEL), row),
        out_shape=jax.ShapeDtypeStruct((T, D_MODEL), F32),
        compiler_params=pltpu.CompilerParams(
            dimension_semantics=("parallel",), vmem_limit_bytes=VMEM_LIMIT),
        name="merge",
    )(x2d, ya, yc, yd, proj, proj, proj, wa, wc, wd, wo, final_w)


CONV_HALO = 32
CONV_ROWS = 64
CONV_STRIP = 256


def _causal_taps(buf_ref, w_ref, r0, rows, lanes, n_taps, halo):
    assert halo >= SUBLANES * ((n_taps - 1) // SUBLANES + 1)
    acc = None
    for r in range(min(SUBLANES, n_taps)):
        part = None
        for a in range((n_taps - 1 - r) // SUBLANES + 1):
            s = SUBLANES * a + r
            start = pl.multiple_of(r0 + (halo - SUBLANES * (a + 1)), SUBLANES)
            term = (w_ref[n_taps - 1 - s:n_taps - s, lanes]
                    * buf_ref[pl.ds(start, rows + SUBLANES), lanes])
            part = term if part is None else part + term
        if r:
            part = pltpu.roll(part, r, axis=0)
        acc = part if acc is None else acc + part
    return acc[SUBLANES:, :]


def _conv_body(a_ref, g_ref, z_ref, w_ref, b_ref, lnw_ref, lnb_ref, o_ref, ubuf, cbuf, *, tl):
    @pl.when(pl.program_id(1) == 0)
    def _():
        ubuf[0:CONV_HALO, :] = jnp.zeros((CONV_HALO, CONV_WIDTH), F32)

    a = a_ref[...].astype(F32)
    g = g_ref[...].astype(F32)
    ubuf[CONV_HALO:CONV_HALO + tl, :] = a * jax.nn.sigmoid(g)

    bias = b_ref[...]
    lnw = lnw_ref[...]
    lnb = lnb_ref[...]

    def step(r, carry):
        r0 = pl.multiple_of(r * CONV_ROWS, CONV_ROWS)
        for c in range(CONV_WIDTH // CONV_STRIP):
            lanes = slice(c * CONV_STRIP, (c + 1) * CONV_STRIP)
            cbuf[:, lanes] = _causal_taps(ubuf, w_ref, r0, CONV_ROWS, lanes, CONV_KERNEL, CONV_HALO)
        u = cbuf[...] + bias
        mu = jnp.mean(u, axis=-1, keepdims=True)
        uc = u - mu
        var = jnp.mean(uc * uc, axis=-1, keepdims=True)
        y = uc * lax.rsqrt(var + LN_EPS) * lnw + lnb
        zz = z_ref[pl.ds(r0, CONV_ROWS), :].astype(F32)
        o_ref[pl.ds(r0, CONV_ROWS), :] = (_silu(y) * _silu(zz)).astype(BF16)
        return carry

    lax.fori_loop(0, tl // CONV_ROWS, step, 0)
    ubuf[0:CONV_HALO, :] = ubuf[tl:tl + CONV_HALO, :]


def _conv_branch(proj3, dw_w, dw_b, ln_w, ln_b, *, tl=512):
    B, L, _ = proj3.shape
    vec = pl.BlockSpec((1, CONV_WIDTH), lambda b, t: (0, 0))
    return pl.pallas_call(
        functools.partial(_conv_body, tl=tl),
        grid=(B, L // tl),
        in_specs=[
            pl.BlockSpec((None, tl, CONV_WIDTH), lambda b, t: (b, t, COL_CA)),
            pl.BlockSpec((None, tl, CONV_WIDTH), lambda b, t: (b, t, COL_CG)),
            pl.BlockSpec((None, tl, CONV_WIDTH), lambda b, t: (b, t, COL_CZ)),
            pl.BlockSpec((CONV_KERNEL, CONV_WIDTH), lambda b, t: (0, 0)),
            vec, vec, vec,
        ],
        out_specs=pl.BlockSpec((None, tl, CONV_WIDTH), lambda b, t: (b, t, 0)),
        out_shape=jax.ShapeDtypeStruct((B, L, CONV_WIDTH), BF16),
        scratch_shapes=[pltpu.VMEM((CONV_HALO + tl, CONV_WIDTH), F32),
                        pltpu.VMEM((CONV_ROWS, CONV_WIDTH), F32)],
        compiler_params=pltpu.CompilerParams(
            dimension_semantics=("parallel", "arbitrary"), vmem_limit_bytes=VMEM_LIMIT),
        name="conv_branch",
    )(proj3, proj3, proj3, dw_w, dw_b, ln_w, ln_b)


def _nt_dot(a, b):
    return lax.dot_general(a, b, (((1,), (1,)), ((), ())), preferred_element_type=F32)


def _rope(t, cos, sin_signed):
    lane = lax.broadcasted_iota(jnp.int32, t.shape, 1)
    first_half = (lane % ATT_QK_DIM) < (ATT_QK_DIM // 2)
    half = ATT_QK_DIM // 2
    swapped = jnp.where(first_half, pltpu.roll(t, t.shape[1] - half, axis=1),
                        pltpu.roll(t, half, axis=1))
    return t * cos + swapped * sin_signed


ATT_HPS = 2
ATT_ONES_ROWS = 16


def _attn_body(q_ref, k_ref, v_ref, z_ref, cq_ref, sq_ref, ck_ref, sk_ref, lam_ref, sw_ref, o_ref,
               kr_ref, vt_ref, m_ref, alpha_ref, acc_ref, s_ref, p_ref, *, lambda_init, tq):
    qi = pl.program_id(2)
    heads = range(ATT_HPS)
    hl = [slice(h * LANES, (h + 1) * LANES) for h in heads]
    cos_q = jnp.concatenate([cq_ref[...]] * ATT_HPS, axis=1)
    sin_q = jnp.concatenate([sq_ref[...]] * ATT_HPS, axis=1)

    @pl.when(qi == 0)
    def _():
        cos_k = jnp.concatenate([ck_ref[...]] * ATT_HPS, axis=1)
        sin_k = jnp.concatenate([sk_ref[...]] * ATT_HPS, axis=1)
        kr_ref[...] = _rope(k_ref[...].astype(F32), cos_k, sin_k).astype(BF16)

        ri = lax.broadcasted_iota(jnp.int32, (ATT_V_DIM, ATT_V_DIM), 0)
        ci = lax.broadcasted_iota(jnp.int32, (ATT_V_DIM, ATT_V_DIM), 1)
        ident = (ri == ci).astype(BF16)
        for jb in range(vt_ref.shape[1]):
            for h in heads:
                vt_ref[h, jb, 0:ATT_V_DIM, :] = _nt_dot(
                    ident, v_ref[jb * tq:(jb + 1) * tq, hl[h]]).astype(BF16)
                vt_ref[h, jb, ATT_V_DIM:, :] = jnp.ones((ATT_ONES_ROWS, tq), BF16)

    q = _rope(q_ref[...].astype(F32), cos_q, sin_q) * (ATT_QK_DIM ** -0.5 * math.log2(math.e))
    lane = lax.broadcasted_iota(jnp.int32, q.shape, 1) % LANES
    q_maps = jnp.concatenate([jnp.where(lane < ATT_QK_DIM, q, 0.0),
                              jnp.where(lane >= ATT_QK_DIM, q, 0.0)], axis=0).astype(BF16)
    qs = [q_maps[:, hl[h]] for h in heads]

    def scores(h, j):
        k0 = pl.multiple_of(j * tq, tq)
        return _nt_dot(kr_ref[pl.ds(k0, tq), hl[h]], qs[h])

    def softmax_step(h, s):
        m_prev = m_ref[h]
        m_new = jnp.maximum(m_prev, jnp.max(s, axis=0, keepdims=True))
        alpha = jnp.exp2(m_prev - m_new)
        p = jnp.exp2(s - m_new)
        m_ref[h] = m_new
        return alpha, p.astype(BF16)

    def pv(h, j, p):
        return jnp.dot(vt_ref[h, j], p, preferred_element_type=F32)

    m_ref[...] = jnp.full(m_ref.shape, -jnp.inf, F32)
    acc_ref[...] = jnp.zeros(acc_ref.shape, F32)
    alpha_ref[...] = jnp.ones(alpha_ref.shape, F32)
    for h in heads:
        p_ref[h, 1] = jnp.zeros(p_ref.shape[2:], BF16)
        s_ref[h, 0] = scores(h, 0)

    def trip(t, carry):
        slot = t % 2
        pv_prev = [pv(h, jnp.maximum(t - 1, 0), p_ref[h, 1 - slot]) for h in heads]
        s_next = [scores(h, t + 1) for h in heads]
        for h in heads:
            alpha, p = softmax_step(h, s_ref[h, slot])
            p_ref[h, slot] = p
            acc_ref[h] = alpha_ref[h] * acc_ref[h] + pv_prev[h]
            alpha_ref[h] = alpha
        for h in heads:
            s_ref[h, 1 - slot] = s_next[h]
        return carry

    lax.fori_loop(0, qi, trip, 0)

    slot = qi % 2
    pv_prev = [pv(h, jnp.maximum(qi - 1, 0), p_ref[h, 1 - slot]) for h in heads]
    key = lax.broadcasted_iota(jnp.int32, (tq, 2 * tq), 0)
    qry = lax.broadcasted_iota(jnp.int32, (tq, 2 * tq), 1) % tq
    last = [softmax_step(h, jnp.where(key <= qry, s_ref[h, slot], -jnp.inf)) for h in heads]
    pv_last = [pv(h, qi, last[h][1]) for h in heads]

    lq = lam_ref[...]
    lam = (jnp.exp(jnp.sum(lq[0:1, :] * lq[1:2, :], axis=-1, keepdims=True))
           - jnp.exp(jnp.sum(lq[2:3, :] * lq[3:4, :], axis=-1, keepdims=True)) + lambda_init)
    outs = []
    for h in heads:
        acc = last[h][0] * (alpha_ref[h] * acc_ref[h] + pv_prev[h]) + pv_last[h]
        o_all = acc[:ATT_V_DIM, :] / acc[ATT_V_DIM:ATT_V_DIM + 1, :]
        ot = o_all[:, :tq] - lam * o_all[:, tq:]
        ms = jnp.mean(ot * ot, axis=0, keepdims=True)
        outs.append((ot * lax.rsqrt(ms + RMS_EPS) * sw_ref[...] * (1.0 - lambda_init)).T)
    on = jnp.concatenate(outs, axis=1)
    o_ref[...] = (on * _silu(z_ref[...].astype(F32))).astype(BF16)


def _attention(proj3, cosf, sinf, lam_qk, subln_w, *, lambda_init, tq=256):
    B, L, _ = proj3.shape
    hw = ATT_HPS * LANES
    hpb = D_MODEL // hw
    qblk = lambda col: pl.BlockSpec((None, tq, hw), lambda b, h, i: (b, i, col * hpb + h))
    full = lambda col: pl.BlockSpec((None, L, hw), lambda b, h, i: (b, 0, col * hpb + h))
    return pl.pallas_call(
        functools.partial(_attn_body, lambda_init=lambda_init, tq=tq),
        grid=(B, ATT_HEADS // ATT_HPS, L // tq),
        in_specs=[
            qblk(COL_AQ), full(COL_AK), full(COL_AV), qblk(COL_AZ),
            pl.BlockSpec((None, tq, LANES), lambda b, h, i: (b, i, 0)),
            pl.BlockSpec((None, tq, LANES), lambda b, h, i: (b, i, 0)),
            pl.BlockSpec((None, L, LANES), lambda b, h, i: (b, 0, 0)),
            pl.BlockSpec((None, L, LANES), lambda b, h, i: (b, 0, 0)),
            pl.BlockSpec((4, ATT_QK_DIM), lambda b, h, i: (0, 0)),
            pl.BlockSpec((ATT_V_DIM, 1), lambda b, h, i: (0, 0)),
        ],
        out_specs=pl.BlockSpec((None, tq, hw), lambda b, h, i: (b, i, h)),
        out_shape=jax.ShapeDtypeStruct((B, L, ATT_HEADS * ATT_V_DIM), BF16),
        scratch_shapes=[
            pltpu.VMEM((L, hw), BF16),
            pltpu.VMEM((ATT_HPS, L // tq, ATT_V_DIM + ATT_ONES_ROWS, tq), BF16),
            pltpu.VMEM((ATT_HPS, 1, 2 * tq), F32),
            pltpu.VMEM((ATT_HPS, 1, 2 * tq), F32),
            pltpu.VMEM((ATT_HPS, ATT_V_DIM + ATT_ONES_ROWS, 2 * tq), F32),
            pltpu.VMEM((ATT_HPS, 2, tq, 2 * tq), F32),
            pltpu.VMEM((ATT_HPS, 2, tq, 2 * tq), BF16),
        ],
        compiler_params=pltpu.CompilerParams(
            dimension_semantics=("parallel", "parallel", "arbitrary"), vmem_limit_bytes=VMEM_LIMIT),
        name="diff_attention",
    )(proj3, proj3, proj3, proj3, cosf, sinf, cosf, sinf, lam_qk, subln_w)


DN_HALO = 8
DN_STRIP = 256
DN_B_LANE = 0
DN_A_LANE = DN_HEADS
DN_ONE_LANE = DN_A_LANE + 3 * DN_HEADS
DN_GROUP = 4


def _split3(x):
    hi = x.astype(BF16).astype(F32)
    r1 = x - hi
    mid = r1.astype(BF16).astype(F32)
    lo = (r1 - mid).astype(BF16).astype(F32)
    return hi, mid, lo


def _dn_body(q_ref, k_ref, v_ref, z_ref, ba_ref, cw_ref, gp_ref, nw_ref, o_ref,
             xbuf, qkv, u_s, w_s, qg_s, kd_s, in_s, dec_s, s_ref, *, tb):
    C = DN_CHUNK
    W = DN_WIDTH
    hd = DN_HEAD_DIM

    @pl.when(pl.program_id(1) == 0)
    def _():
        xbuf[0:DN_HALO, :] = jnp.zeros((DN_HALO, 3 * W), F32)
        s_ref[...] = jnp.zeros(s_ref.shape, F32)

    xbuf[DN_HALO:DN_HALO + tb, 0:W] = q_ref[...].astype(F32)
    xbuf[DN_HALO:DN_HALO + tb, W:2 * W] = k_ref[...].astype(F32)
    xbuf[DN_HALO:DN_HALO + tb, 2 * W:3 * W] = v_ref[...].astype(F32)

    def conv_step(i, carry):
        r0 = pl.multiple_of(i * C, C)
        for c in range(3 * W // DN_STRIP):
            lanes = slice(c * DN_STRIP, (c + 1) * DN_STRIP)
            qkv[pl.ds(r0, C), lanes] = _silu(
                _causal_taps(xbuf, cw_ref, r0, C, lanes, DN_SHORT_CONV, DN_HALO))
        return carry

    lax.fori_loop(0, tb // C, conv_step, 0)
    xbuf[0:DN_HALO, :] = xbuf[tb:tb + DN_HALO, :]

    n_chunks = tb // C
    G = DN_GROUP
    n_groups = DN_HEADS // G
    gw = G * hd
    gc_w = G * C

    row = lax.broadcasted_iota(jnp.int32, (C, C), 0)
    col = lax.broadcasted_iota(jnp.int32, (C, C), 1)
    tri = (row >= col).astype(BF16)
    lane = lax.broadcasted_iota(jnp.int32, (C, LANES), 1)
    ri = lax.broadcasted_iota(jnp.int32, (C, gc_w), 0)
    cj = lax.broadcasted_iota(jnp.int32, (C, gc_w), 1) % C
    eye_cat = (ri == cj).astype(F32)
    bd_sq = (lax.broadcasted_iota(jnp.int32, (gc_w, gc_w), 0) // C
             == lax.broadcasted_iota(jnp.int32, (gc_w, gc_w), 1) // C)
    bd_wide = (lax.broadcasted_iota(jnp.int32, (gc_w, gw), 0) // C
               == lax.broadcasted_iota(jnp.int32, (gc_w, gw), 1) // hd)
    neg_a = -jnp.exp(gp_ref[0:1, :])
    dt_bias = gp_ref[1:2, :]
    norm_w = nw_ref[...]

    def block_diag(x, mask):
        return jnp.where(mask, jnp.tile(x, (G, 1)), 0.0).astype(BF16)

    def bdot(a, b):
        return jnp.dot(a, b, preferred_element_type=F32)

    probs = [(ci, g) for ci in range(n_chunks) for g in range(n_groups)]

    gd_lhs, gcb_all, kq_lhs, kn_grp, vb_grp, kbe_grp = {}, {}, {}, {}, {}, {}
    for ci in range(n_chunks):
        r0 = ci * C
        ba = ba_ref[r0:r0 + C, :]
        beta_all = jax.nn.sigmoid(ba)
        xa = ba + dt_bias
        softplus = jnp.maximum(xa, 0.0) + jnp.log1p(jnp.exp(-jnp.abs(xa)))
        g = jnp.where((lane >= DN_A_LANE) & (lane < DN_A_LANE + DN_HEADS), neg_a * softplus, 0.0)
        g_hi, g_mid, g_lo = _split3(g)
        gc = (bdot(tri, g_hi.astype(BF16)) + bdot(tri, g_mid.astype(BF16))
              + bdot(tri, g_lo.astype(BF16)))
        c_hi, c_mid, c_lo = _split3(gc)
        gd_lhs[ci] = (c_hi + pltpu.roll(c_mid, DN_HEADS, axis=1) + pltpu.roll(c_lo, 2 * DN_HEADS, axis=1)
                      + jnp.where((lane >= DN_ONE_LANE) & (lane < DN_ONE_LANE + 3), 1.0, 0.0)).astype(BF16)

        beta_b = jnp.concatenate(
            [jnp.broadcast_to(beta_all[:, DN_B_LANE + h:DN_B_LANE + h + 1], (C, hd))
             for h in range(DN_HEADS)], axis=1)
        gcb = jnp.concatenate(
            [jnp.broadcast_to(gc[:, DN_A_LANE + h:DN_A_LANE + h + 1], (C, hd))
             for h in range(DN_HEADS)], axis=1)
        gcb_all[ci] = gcb
        eg = jnp.exp(gcb)
        glast = gcb[C - 1:C, :]
        dec_s[ci] = jnp.exp(glast)

        q_all = qkv[r0:r0 + C, 0:W]
        k_all = qkv[r0:r0 + C, W:2 * W]
        v_all = qkv[r0:r0 + C, 2 * W:3 * W]
        qn = jnp.concatenate(
            [q_all[:, h * hd:(h + 1) * hd] * lax.rsqrt(jnp.sum(
                q_all[:, h * hd:(h + 1) * hd] ** 2, axis=-1, keepdims=True) + L2_EPS)
             for h in range(DN_HEADS)], axis=1) * (hd ** -0.5)
        kn = jnp.concatenate(
            [k_all[:, h * hd:(h + 1) * hd] * lax.rsqrt(jnp.sum(
                k_all[:, h * hd:(h + 1) * hd] ** 2, axis=-1, keepdims=True) + L2_EPS)
             for h in range(DN_HEADS)], axis=1)
        kb = kn * beta_b
        qg_s[r0:r0 + C, :] = (qn * eg).astype(BF16)
        kd_s[r0:r0 + C, :] = (kn * jnp.exp(glast - gcb)).astype(BF16)
        for gi in range(n_groups):
            gl = slice(gi * gw, (gi + 1) * gw)
            kq_lhs[ci, gi] = jnp.concatenate([kb[:, gl], qn[:, gl]], axis=0).astype(BF16)
            kn_grp[ci, gi] = kn[:, gl]
            vb_grp[ci, gi] = (v_all * beta_b)[:, gl]
            kbe_grp[ci, gi] = (kb * eg)[:, gl]

    gdiff, kq = {}, {}
    for ci, gi in probs:
        blocks = []
        for h in range(gi * G, (gi + 1) * G):
            p_hi, p_mid, p_lo = _split3(gcb_all[ci][:, h * hd:(h + 1) * hd])
            onehot = ((lane == DN_A_LANE + h) | (lane == DN_A_LANE + DN_HEADS + h)
                      | (lane == DN_A_LANE + 2 * DN_HEADS + h))
            blocks.append(jnp.where(lane == DN_ONE_LANE, -p_hi, jnp.where(
                lane == DN_ONE_LANE + 1, -p_mid, jnp.where(
                    lane == DN_ONE_LANE + 2, -p_lo, jnp.where(onehot, 1.0, 0.0)))))
        gd_rhs = jnp.concatenate(blocks, axis=0).astype(BF16)
        gdiff[ci, gi] = _nt_dot(gd_lhs[ci], gd_rhs)
        kq[ci, gi] = _nt_dot(kq_lhs[ci, gi], block_diag(kn_grp[ci, gi], bd_wide))

    m, p = {}, {}
    for pr in probs:
        decay = jnp.exp(jnp.where(ri >= cj, gdiff[pr], -jnp.inf))
        m[pr] = -jnp.where(ri > cj, kq[pr][:C, :] * decay, 0.0)
        ci, gi = pr
        in_s[ci * C:(ci + 1) * C, gi * gc_w:(gi + 1) * gc_w] = (kq[pr][C:, :] * decay).astype(BF16)
        p[pr] = eye_cat + m[pr]

    n_levels = int(math.log2(C)) - 1
    for pr in probs:
        m[pr] = bdot(m[pr].astype(BF16), block_diag(m[pr], bd_sq))
    for lvl in range(n_levels):
        last = lvl == n_levels - 1
        res = {}
        for pr in probs:
            lhs = p[pr] if last else jnp.concatenate([p[pr], m[pr]], axis=0)
            res[pr] = bdot(lhs.astype(BF16), block_diag(m[pr], bd_sq))
        for pr in probs:
            p[pr] = p[pr] + res[pr][:C, :]
            if not last:
                m[pr] = res[pr][C:, :]

    for pr in probs:
        ci, gi = pr
        pb = p[pr].astype(BF16)
        rows = slice(ci * C, (ci + 1) * C)
        gl = slice(gi * gw, (gi + 1) * gw)
        u_s[rows, gl] = bdot(pb, block_diag(vb_grp[pr], bd_wide))
        w_s[rows, gl] = bdot(pb, block_diag(kbe_grp[pr], bd_wide)).astype(BF16)

    def chunk_step(ci, carry):
        r0 = pl.multiple_of(ci * C, C)
        rows = pl.ds(r0, C)
        sb = [s_ref[h].astype(BF16) for h in range(DN_HEADS)]
        ws = [bdot(jnp.concatenate([w_s[rows, h * hd:(h + 1) * hd], qg_s[rows, h * hd:(h + 1) * hd]],
                                   axis=0), sb[h]) for h in range(DN_HEADS)]
        v_new = u_s[rows, :] - jnp.concatenate([x[:C, :] for x in ws], axis=1)
        vnb = v_new.astype(BF16)
        o_intra = [bdot(in_s[rows, gi * gc_w:(gi + 1) * gc_w],
                        block_diag(v_new[:, gi * gw:(gi + 1) * gw], bd_wide))
                   for gi in range(n_groups)]
        dec = dec_s[ci]
        for h in range(DN_HEADS):
            hl = slice(h * hd, (h + 1) * hd)
            s_ref[h] = s_ref[h] * dec[:, hl] + lax.dot_general(
                kd_s[rows, hl], vnb[:, hl], (((0,), (0,)), ((), ())), preferred_element_type=F32)
        o = jnp.concatenate([x[C:, :] for x in ws], axis=1) + jnp.concatenate(o_intra, axis=1)
        for h in range(DN_HEADS):
            hl = slice(h * hd, (h + 1) * hd)
            oh = o[:, hl]
            ms = jnp.mean(oh * oh, axis=-1, keepdims=True)
            on = oh * lax.rsqrt(ms + RMS_EPS) * norm_w
            o_ref[rows, hl] = (on * _silu(z_ref[rows, hl].astype(F32))).astype(BF16)
        return carry

    lax.fori_loop(0, n_chunks, chunk_step, 0)


def _deltanet(proj3, ba3, conv_w, gate_params, norm_w, *, tb=256):
    B, L, _ = proj3.shape
    blk = lambda col: pl.BlockSpec((None, tb, DN_WIDTH), lambda b, t: (b, t, col))
    return pl.pallas_call(
        functools.partial(_dn_body, tb=tb),
        grid=(B, L // tb),
        in_specs=[
            blk(COL_DQ), blk(COL_DK), blk(COL_DV), blk(COL_DZ),
            pl.BlockSpec((None, tb, LANES), lambda b, t: (b, t, 0)),
            pl.BlockSpec((DN_SHORT_CONV, 3 * DN_WIDTH), lambda b, t: (0, 0)),
            pl.BlockSpec((2, LANES), lambda b, t: (0, 0)),
            pl.BlockSpec((1, DN_HEAD_DIM), lambda b, t: (0, 0)),
        ],
        out_specs=pl.BlockSpec((None, tb, DN_WIDTH), lambda b, t: (b, t, 0)),
        out_shape=jax.ShapeDtypeStruct((B, L, DN_WIDTH), BF16),
        scratch_shapes=[
            pltpu.VMEM((DN_HALO + tb, 3 * DN_WIDTH), F32),
            pltpu.VMEM((tb, 3 * DN_WIDTH), F32),
            pltpu.VMEM((tb, DN_WIDTH), F32),
            pltpu.VMEM((tb, DN_WIDTH), BF16),
            pltpu.VMEM((tb, DN_WIDTH), BF16),
            pltpu.VMEM((tb, DN_WIDTH), BF16),
            pltpu.VMEM((tb, DN_HEADS * DN_CHUNK), BF16),
            pltpu.VMEM((tb // DN_CHUNK, 1, DN_WIDTH), F32),
            pltpu.VMEM((DN_HEADS, DN_HEAD_DIM, DN_HEAD_DIM), F32),
        ],
        compiler_params=pltpu.CompilerParams(
            dimension_semantics=("parallel", "arbitrary"), vmem_limit_bytes=VMEM_LIMIT),
        name="gated_deltanet",
    )(proj3, proj3, proj3, proj3, ba3, conv_w, gate_params, norm_w)


def _rope_tables(positions):
    half = ATT_QK_DIM // 2
    inv_freq = ROPE_THETA ** (-jnp.arange(0, ATT_QK_DIM, 2, dtype=F32) / ATT_QK_DIM)
    ang = positions.astype(F32)[..., None] * inv_freq
    cos, sin = jnp.cos(ang), jnp.sin(ang)
    reps = LANES // half
    cosf = jnp.tile(cos, (1, 1, reps))
    sinf = jnp.concatenate([-sin, sin] * (reps // 2), axis=-1)
    return cosf, sinf


def kernel(x, positions, norm_w, w_in, lam_qk, attn_subln_w, w_attn_out, conv_dw_w, conv_dw_b,
           conv_ln_w, conv_ln_b, w_conv_out, dn_conv_w, dn_a_log, dn_dt_bias, dn_norm_w,
           w_dn_out, w_out, final_norm_w):
    B, L, _ = x.shape
    T = B * L
    x2d = x.reshape(T, D_MODEL)
    depth = norm_w.shape[0]
    cosf, sinf = _rope_tables(positions)
    pad_heads = lambda v: jnp.pad(v, (DN_A_LANE, LANES - DN_A_LANE - DN_HEADS))
    for l in range(depth):
        w = w_in[l]
        w_main = jnp.concatenate([w[:, :ORIG_DB], w[:, ORIG_GATE:]], axis=1).astype(BF16)
        w_ba = jnp.pad(w[:, ORIG_DB:ORIG_GATE], ((0, 0), (0, LANES - 2 * DN_HEADS))).astype(BF16)
        proj, ba = _inproj(x2d, norm_w[l][None, :], w_main, w_ba)
        proj3 = proj.reshape(B, L, N_MAIN)
        lambda_init = 0.8 - 0.6 * math.exp(-0.3 * l)
        ya = _attention(proj3, cosf, sinf, lam_qk[l], attn_subln_w[l][:, None],
                        lambda_init=lambda_init).reshape(T, ATT_HEADS * ATT_V_DIM)
        yc = _conv_branch(proj3, conv_dw_w[l], conv_dw_b[l][None, :], conv_ln_w[l][None, :],
                          conv_ln_b[l][None, :]).reshape(T, CONV_WIDTH)
        gate_params = jnp.stack([pad_heads(dn_a_log[l]), pad_heads(dn_dt_bias[l])])
        yd = _deltanet(proj3, ba.reshape(B, L, LANES), dn_conv_w[l], gate_params,
                       dn_norm_w[l][None, :]).reshape(T, DN_WIDTH)
        x2d = _merge(x2d, ya, yc, yd, proj, w_attn_out[l].astype(BF16), w_conv_out[l].astype(BF16),
                     w_dn_out[l].astype(BF16), w_out[l].astype(BF16), final_norm_w[None, :],
                     final_norm=(l == depth - 1))
    return x2d.reshape(B, L, D_MODEL)
```

```python
import functools
import math

import jax
import jax.numpy as jnp
from jax import lax
from jax.experimental import pallas as pl
from jax.experimental.pallas import tpu as pltpu

F32 = jnp.float32
BF16 = jnp.bfloat16

D_MODEL = 1024
ATT_HEADS = 8
ATT_QK_DIM = 64
ATT_V_DIM = 128
ROPE_THETA = 10000.0
CONV_WIDTH = 1024
CONV_KERNEL = 31
DN_HEADS = 8
DN_HEAD_DIM = 128
DN_WIDTH = DN_HEADS * DN_HEAD_DIM
DN_SHORT_CONV = 4
DN_CHUNK = 64
RMS_EPS = 1e-6
LN_EPS = 1e-5
L2_EPS = 1e-6

LANES = 128
SUBLANES = 8
VMEM_LIMIT = 56 * 1024 * 1024
COL_AQ, COL_AK, COL_AV, COL_AZ = 0, 1, 2, 3
COL_CA, COL_CG, COL_CZ = 4, 5, 6
COL_DQ, COL_DK, COL_DV, COL_DZ = 7, 8, 9, 10
COL_GATE = 11
N_LEAD_TILES = 11
N_MAIN = 14 * D_MODEL
ORIG_DB = N_LEAD_TILES * D_MODEL
ORIG_GATE = ORIG_DB + 2 * DN_HEADS


def _silu(x):
    return x * jax.nn.sigmoid(x)


def _merge_body(x_ref, a_ref, c_ref, d_ref, ga_ref, gc_ref, gd_ref,
                wa_ref, wc_ref, wd_ref, wo_ref, fw_ref, o_ref, *, final_norm):
    ya = jnp.dot(a_ref[...], wa_ref[...], preferred_element_type=F32)
    yc = jnp.dot(c_ref[...], wc_ref[...], preferred_element_type=F32)
    yd = jnp.dot(d_ref[...], wd_ref[...], preferred_element_type=F32)
    merged = (jax.nn.sigmoid(ga_ref[...].astype(F32)) * ya
              + jax.nn.sigmoid(gc_ref[...].astype(F32)) * yc
              + jax.nn.sigmoid(gd_ref[...].astype(F32)) * yd)
    y = x_ref[...] + jnp.dot(merged.astype(BF16), wo_ref[...], preferred_element_type=F32)
    if final_norm:
        ms = jnp.mean(y * y, axis=-1, keepdims=True)
        y = y * lax.rsqrt(ms + RMS_EPS) * fw_ref[...]
    o_ref[...] = y


def _merge(x2d, ya, yc, yd, proj, wa, wc, wd, wo, final_w, *, final_norm, tm=512):
    T = x2d.shape[0]
    row = lambda i: (i, 0)
    const = lambda i: (0, 0)
    wspec = pl.BlockSpec((D_MODEL, D_MODEL), const)
    return pl.pallas_call(
        functools.partial(_merge_body, final_norm=final_norm),
        grid=(T // tm,),
        in_specs=[
            pl.BlockSpec((tm, D_MODEL), row),
            pl.BlockSpec((tm, D_MODEL), row),
            pl.BlockSpec((tm, D_MODEL), row),
            pl.BlockSpec((tm, D_MODEL), row),
            pl.BlockSpec((tm, D_MODEL), lambda i: (i, COL_GATE)),
            pl.BlockSpec((tm, D_MODEL), lambda i: (i, COL_GATE + 1)),
            pl.BlockSpec((tm, D_MODEL), lambda i: (i, COL_GATE + 2)),
            wspec, wspec, wspec, wspec,
            pl.BlockSpec((1, D_MODEL), const),
        ],
        out_specs=pl.BlockSpec((tm, D_MODEL), row),
        out_shape=jax.ShapeDtypeStruct((T, D_MODEL), F32),
        compiler_params=pltpu.CompilerParams(
            dimension_semantics=("parallel",), vmem_limit_bytes=VMEM_LIMIT),
        name="merge",
    )(x2d, ya, yc, yd, proj, proj, proj, wa, wc, wd, wo, final_w)


CONV_HALO = 32


def _causal_taps(buf_ref, w_ref, r0, rows, lanes, n_taps, halo):
    assert halo >= SUBLANES * ((n_taps - 1) // SUBLANES + 1)
    acc = None
    for r in range(min(SUBLANES, n_taps)):
        part = None
        for a in range((n_taps - 1 - r) // SUBLANES + 1):
            s = SUBLANES * a + r
            start = r0 + (halo - SUBLANES * (a + 1))
            if not isinstance(start, int):
                start = pl.multiple_of(start, SUBLANES)
            term = (w_ref[n_taps - 1 - s:n_taps - s, lanes]
                    * buf_ref[pl.ds(start, rows + SUBLANES), lanes])
            part = term if part is None else part + term
        if r:
            part = pltpu.roll(part, r, axis=0)
        acc = part if acc is None else acc + part
    return acc[SUBLANES:, :]


def _inproj_body(x_ref, nw_ref, w_ref, wg_ref, wba_ref, o_ref, ba_ref, h_ref):
    j = pl.program_id(1)

    @pl.when(j == 0)
    def _():
        x = x_ref[...]
        ms = jnp.mean(x * x, axis=-1, keepdims=True)
        h = (x * lax.rsqrt(ms + RMS_EPS) * nw_ref[...]).astype(BF16)
        h_ref[...] = h
        ba_ref[...] = jnp.dot(h, wba_ref[...].astype(BF16), preferred_element_type=F32)

    @pl.when(j < N_LEAD_TILES)
    def _():
        o_ref[...] = jnp.dot(h_ref[...], w_ref[...].astype(BF16),
                             preferred_element_type=F32).astype(BF16)

    @pl.when(j >= N_LEAD_TILES)
    def _():
        o_ref[...] = jnp.dot(h_ref[...], wg_ref[...].astype(BF16),
                             preferred_element_type=F32).astype(BF16)


def _inproj(x2d, norm_w, w_full, w_gate, w_ba, *, tm=2048):
    T = x2d.shape[0]
    tn = D_MODEL
    const = lambda i, j: (0, 0)
    return pl.pallas_call(
        _inproj_body,
        grid=(T // tm, N_MAIN // tn),
        in_specs=[
            pl.BlockSpec((tm, D_MODEL), lambda i, j: (i, 0)),
            pl.BlockSpec((1, D_MODEL), const),
            pl.BlockSpec((D_MODEL, tn), lambda i, j: (0, jnp.minimum(j, N_LEAD_TILES - 1))),
            pl.BlockSpec((D_MODEL, tn), lambda i, j: (0, jnp.maximum(j - N_LEAD_TILES, 0))),
            pl.BlockSpec((D_MODEL, LANES), const),
        ],
        out_specs=[
            pl.BlockSpec((tm, tn), lambda i, j: (i, j)),
            pl.BlockSpec((tm, LANES), lambda i, j: (i, 0)),
        ],
        out_shape=[
            jax.ShapeDtypeStruct((T, N_MAIN), BF16),
            jax.ShapeDtypeStruct((T, LANES), F32),
        ],
        scratch_shapes=[pltpu.VMEM((tm, D_MODEL), BF16)],
        compiler_params=pltpu.CompilerParams(
            dimension_semantics=("parallel", "arbitrary"), vmem_limit_bytes=VMEM_LIMIT),
        name="inproj",
    )(x2d, norm_w, w_full, w_gate, w_ba)


CONV_ROWS = 128
CONV_STRIP = 128


def _conv_body(a_ref, g_ref, z_ref, w_ref, b_ref, lnw_ref, lnb_ref, o_ref, ubuf, cbuf, *, tl):
    @pl.when(pl.program_id(1) == 0)
    def _():
        ubuf[0:CONV_HALO, :] = jnp.zeros((CONV_HALO, CONV_WIDTH), F32)

    a = a_ref[...].astype(F32)
    g = g_ref[...].astype(F32)
    ubuf[CONV_HALO:CONV_HALO + tl, :] = a * jax.nn.sigmoid(g)

    bias = b_ref[...]
    lnw = lnw_ref[...]
    lnb = lnb_ref[...]

    def step(r, carry):
        r0 = pl.multiple_of(r * CONV_ROWS, CONV_ROWS)
        for c in range(CONV_WIDTH // CONV_STRIP):
            lanes = slice(c * CONV_STRIP, (c + 1) * CONV_STRIP)
            cbuf[:, lanes] = _causal_taps(ubuf, w_ref, r0, CONV_ROWS, lanes, CONV_KERNEL, CONV_HALO)
        u = cbuf[...] + bias
        mu = jnp.mean(u, axis=-1, keepdims=True)
        uc = u - mu
        var = jnp.mean(uc * uc, axis=-1, keepdims=True)
        y = uc * lax.rsqrt(var + LN_EPS) * lnw + lnb
        zz = z_ref[pl.ds(r0, CONV_ROWS), :].astype(F32)
        o_ref[pl.ds(r0, CONV_ROWS), :] = (_silu(y) * _silu(zz)).astype(BF16)
        return carry

    lax.fori_loop(0, tl // CONV_ROWS, step, 0)
    ubuf[0:CONV_HALO, :] = ubuf[tl:tl + CONV_HALO, :]


def _conv_branch(proj3, dw_w, dw_b, ln_w, ln_b, *, tl=512):
    B, L, _ = proj3.shape
    vec = pl.BlockSpec((1, CONV_WIDTH), lambda b, t: (0, 0))
    return pl.pallas_call(
        functools.partial(_conv_body, tl=tl),
        grid=(B, L // tl),
        in_specs=[
            pl.BlockSpec((None, tl, CONV_WIDTH), lambda b, t: (b, t, COL_CA)),
            pl.BlockSpec((None, tl, CONV_WIDTH), lambda b, t: (b, t, COL_CG)),
            pl.BlockSpec((None, tl, CONV_WIDTH), lambda b, t: (b, t, COL_CZ)),
            pl.BlockSpec((CONV_KERNEL, CONV_WIDTH), lambda b, t: (0, 0)),
            vec, vec, vec,
        ],
        out_specs=pl.BlockSpec((None, tl, CONV_WIDTH), lambda b, t: (b, t, 0)),
        out_shape=jax.ShapeDtypeStruct((B, L, CONV_WIDTH), BF16),
        scratch_shapes=[pltpu.VMEM((CONV_HALO + tl, CONV_WIDTH), F32),
                        pltpu.VMEM((CONV_ROWS, CONV_WIDTH), F32)],
        compiler_params=pltpu.CompilerParams(
            dimension_semantics=("parallel", "arbitrary"), vmem_limit_bytes=VMEM_LIMIT),
        name="conv_branch",
    )(proj3, proj3, proj3, dw_w, dw_b, ln_w, ln_b)


def _nt_dot(a, b):
    return lax.dot_general(a, b, (((1,), (1,)), ((), ())), preferred_element_type=F32)


def _rope(t, cos, sin_signed):
    lane = lax.broadcasted_iota(jnp.int32, t.shape, 1)
    first_half = (lane % ATT_QK_DIM) < (ATT_QK_DIM // 2)
    half = ATT_QK_DIM // 2
    swapped = jnp.where(first_half, pltpu.roll(t, t.shape[1] - half, axis=1),
                        pltpu.roll(t, half, axis=1))
    return t * cos + swapped * sin_signed


ATT_HPS = 2
ATT_ONES_ROWS = 16


def _attn_body(q_ref, k_ref, v_ref, z_ref, cq_ref, sq_ref, ck_ref, sk_ref, lam_ref, sw_ref, o_ref,
               kr_ref, vt_ref, m_ref, alpha_ref, acc_ref, s_ref, p_ref, *, lambda_init, tq):
    qi = pl.program_id(2)
    heads = range(ATT_HPS)
    hl = [slice(h * LANES, (h + 1) * LANES) for h in heads]
    cos_q = jnp.concatenate([cq_ref[...]] * ATT_HPS, axis=1)
    sin_q = jnp.concatenate([sq_ref[...]] * ATT_HPS, axis=1)

    @pl.when(qi == 0)
    def _():
        cos_k = jnp.concatenate([ck_ref[...]] * ATT_HPS, axis=1)
        sin_k = jnp.concatenate([sk_ref[...]] * ATT_HPS, axis=1)
        kr_ref[...] = _rope(k_ref[...].astype(F32), cos_k, sin_k).astype(BF16)

        ri = lax.broadcasted_iota(jnp.int32, (ATT_V_DIM, ATT_V_DIM), 0)
        ci = lax.broadcasted_iota(jnp.int32, (ATT_V_DIM, ATT_V_DIM), 1)
        ident = (ri == ci).astype(BF16)
        for jb in range(vt_ref.shape[1]):
            for h in heads:
                vt_ref[h, jb, 0:ATT_V_DIM, :] = _nt_dot(
                    ident, v_ref[jb * tq:(jb + 1) * tq, hl[h]]).astype(BF16)
                vt_ref[h, jb, ATT_V_DIM:, :] = jnp.ones((ATT_ONES_ROWS, tq), BF16)

    q = _rope(q_ref[...].astype(F32), cos_q, sin_q) * (ATT_QK_DIM ** -0.5 * math.log2(math.e))
    lane = lax.broadcasted_iota(jnp.int32, q.shape, 1) % LANES
    q_maps = jnp.concatenate([jnp.where(lane < ATT_QK_DIM, q, 0.0),
                              jnp.where(lane >= ATT_QK_DIM, q, 0.0)], axis=0).astype(BF16)
    qs = [q_maps[:, hl[h]] for h in heads]

    def scores(h, j):
        k0 = pl.multiple_of(j * tq, tq)
        return _nt_dot(kr_ref[pl.ds(k0, tq), hl[h]], qs[h])

    def softmax_step(h, s):
        m_prev = m_ref[h]
        m_new = jnp.maximum(m_prev, jnp.max(s, axis=0, keepdims=True))
        alpha = jnp.exp2(m_prev - m_new)
        p = jnp.exp2(s - m_new)
        m_ref[h] = m_new
        return alpha, p.astype(BF16)

    def pv(h, j, p):
        return jnp.dot(vt_ref[h, j], p, preferred_element_type=F32)

    m_ref[...] = jnp.full(m_ref.shape, -jnp.inf, F32)
    acc_ref[...] = jnp.zeros(acc_ref.shape, F32)
    alpha_ref[...] = jnp.ones(alpha_ref.shape, F32)
    for h in heads:
        p_ref[h, 1] = jnp.zeros(p_ref.shape[2:], BF16)
        s_ref[h, 0] = scores(h, 0)

    def trip(t, carry):
        slot = t % 2
        pv_prev = [pv(h, jnp.maximum(t - 1, 0), p_ref[h, 1 - slot]) for h in heads]
        s_next = [scores(h, t + 1) for h in heads]
        for h in heads:
            alpha, p = softmax_step(h, s_ref[h, slot])
            p_ref[h, slot] = p
            acc_ref[h] = alpha_ref[h] * acc_ref[h] + pv_prev[h]
            alpha_ref[h] = alpha
        for h in heads:
            s_ref[h, 1 - slot] = s_next[h]
        return carry

    lax.fori_loop(0, qi, trip, 0)

    slot = qi % 2
    pv_prev = [pv(h, jnp.maximum(qi - 1, 0), p_ref[h, 1 - slot]) for h in heads]
    key = lax.broadcasted_iota(jnp.int32, (tq, 2 * tq), 0)
    qry = lax.broadcasted_iota(jnp.int32, (tq, 2 * tq), 1) % tq
    last = [softmax_step(h, jnp.where(key <= qry, s_ref[h, slot], -jnp.inf)) for h in heads]
    pv_last = [pv(h, qi, last[h][1]) for h in heads]

    lq = lam_ref[...]
    lam = (jnp.exp(jnp.sum(lq[0:1, :] * lq[1:2, :], axis=-1, keepdims=True))
           - jnp.exp(jnp.sum(lq[2:3, :] * lq[3:4, :], axis=-1, keepdims=True)) + lambda_init)
    outs = []
    for h in heads:
        acc = last[h][0] * (alpha_ref[h] * acc_ref[h] + pv_prev[h]) + pv_last[h]
        o_all = acc[:ATT_V_DIM, :] / acc[ATT_V_DIM:ATT_V_DIM + 1, :]
        ot = o_all[:, :tq] - lam * o_all[:, tq:]
        ms = jnp.mean(ot * ot, axis=0, keepdims=True)
        outs.append((ot * lax.rsqrt(ms + RMS_EPS) * sw_ref[...] * (1.0 - lambda_init)).T)
    on = jnp.concatenate(outs, axis=1)
    o_ref[...] = (on * _silu(z_ref[...].astype(F32))).astype(BF16)


def _attention(proj3, cosf, sinf, lam_qk, subln_w, *, lambda_init, tq=256):
    B, L, _ = proj3.shape
    hw = ATT_HPS * LANES
    hpb = D_MODEL // hw
    qblk = lambda col: pl.BlockSpec((None, tq, hw), lambda b, h, i: (b, i, col * hpb + h))
    full = lambda col: pl.BlockSpec((None, L, hw), lambda b, h, i: (b, 0, col * hpb + h))
    return pl.pallas_call(
        functools.partial(_attn_body, lambda_init=lambda_init, tq=tq),
        grid=(B, ATT_HEADS // ATT_HPS, L // tq),
        in_specs=[
            qblk(COL_AQ), full(COL_AK), full(COL_AV), qblk(COL_AZ),
            pl.BlockSpec((None, tq, LANES), lambda b, h, i: (b, i, 0)),
            pl.BlockSpec((None, tq, LANES), lambda b, h, i: (b, i, 0)),
            pl.BlockSpec((None, L, LANES), lambda b, h, i: (b, 0, 0)),
            pl.BlockSpec((None, L, LANES), lambda b, h, i: (b, 0, 0)),
            pl.BlockSpec((4, ATT_QK_DIM), lambda b, h, i: (0, 0)),
            pl.BlockSpec((ATT_V_DIM, 1), lambda b, h, i: (0, 0)),
        ],
        out_specs=pl.BlockSpec((None, tq, hw), lambda b, h, i: (b, i, h)),
        out_shape=jax.ShapeDtypeStruct((B, L, ATT_HEADS * ATT_V_DIM), BF16),
        scratch_shapes=[
            pltpu.VMEM((L, hw), BF16),
            pltpu.VMEM((ATT_HPS, L // tq, ATT_V_DIM + ATT_ONES_ROWS, tq), BF16),
            pltpu.VMEM((ATT_HPS, 1, 2 * tq), F32),
            pltpu.VMEM((ATT_HPS, 1, 2 * tq), F32),
            pltpu.VMEM((ATT_HPS, ATT_V_DIM + ATT_ONES_ROWS, 2 * tq), F32),
            pltpu.VMEM((ATT_HPS, 2, tq, 2 * tq), F32),
            pltpu.VMEM((ATT_HPS, 2, tq, 2 * tq), BF16),
        ],
        compiler_params=pltpu.CompilerParams(
            dimension_semantics=("parallel", "parallel", "arbitrary"), vmem_limit_bytes=VMEM_LIMIT),
        name="diff_attention",
    )(proj3, proj3, proj3, proj3, cosf, sinf, cosf, sinf, lam_qk, subln_w)


DN_HALO = 8
DN_STRIP = 256
DN_B_LANE = 0
DN_A_LANE = DN_HEADS
DN_ONE_LANE = DN_A_LANE + 3 * DN_HEADS
DN_GROUP = 4


def _split3(x):
    hi = x.astype(BF16).astype(F32)
    r1 = x - hi
    mid = r1.astype(BF16).astype(F32)
    lo = (r1 - mid).astype(BF16).astype(F32)
    return hi, mid, lo


def _dn_body(q_ref, k_ref, v_ref, z_ref, ba_ref, cw_ref, gp_ref, nw_ref, o_ref,
             xbuf, qkv, u_s, w_s, qg_s, kd_s, in_s, dec_s, s_ref, *, tb):
    C = DN_CHUNK
    W = DN_WIDTH
    hd = DN_HEAD_DIM

    @pl.when(pl.program_id(1) == 0)
    def _():
        xbuf[0:DN_HALO, :] = jnp.zeros((DN_HALO, 3 * W), F32)
        s_ref[...] = jnp.zeros(s_ref.shape, F32)

    xbuf[DN_HALO:DN_HALO + tb, 0:W] = q_ref[...].astype(F32)
    xbuf[DN_HALO:DN_HALO + tb, W:2 * W] = k_ref[...].astype(F32)
    xbuf[DN_HALO:DN_HALO + tb, 2 * W:3 * W] = v_ref[...].astype(F32)

    def conv_step(i, carry):
        r0 = pl.multiple_of(i * C, C)
        for c in range(3 * W // DN_STRIP):
            lanes = slice(c * DN_STRIP, (c + 1) * DN_STRIP)
            qkv[pl.ds(r0, C), lanes] = _silu(
                _causal_taps(xbuf, cw_ref, r0, C, lanes, DN_SHORT_CONV, DN_HALO))
        return carry

    lax.fori_loop(0, tb // C, conv_step, 0)
    xbuf[0:DN_HALO, :] = xbuf[tb:tb + DN_HALO, :]

    n_chunks = tb // C
    G = DN_GROUP
    n_groups = DN_HEADS // G
    gw = G * hd
    gc_w = G * C

    row = lax.broadcasted_iota(jnp.int32, (C, C), 0)
    col = lax.broadcasted_iota(jnp.int32, (C, C), 1)
    tri = (row >= col).astype(BF16)
    lane = lax.broadcasted_iota(jnp.int32, (C, LANES), 1)
    ri = lax.broadcasted_iota(jnp.int32, (C, gc_w), 0)
    cj = lax.broadcasted_iota(jnp.int32, (C, gc_w), 1) % C
    eye_cat = (ri == cj).astype(F32)
    bd_sq = (lax.broadcasted_iota(jnp.int32, (gc_w, gc_w), 0) // C
             == lax.broadcasted_iota(jnp.int32, (gc_w, gc_w), 1) // C)
    bd_wide = (lax.broadcasted_iota(jnp.int32, (gc_w, gw), 0) // C
               == lax.broadcasted_iota(jnp.int32, (gc_w, gw), 1) // hd)
    neg_a = -jnp.exp(gp_ref[0:1, :])
    dt_bias = gp_ref[1:2, :]
    norm_w = nw_ref[...]

    def block_diag(x, mask):
        return jnp.where(mask, jnp.tile(x, (G, 1)), 0.0).astype(BF16)

    def bdot(a, b):
        return jnp.dot(a, b, preferred_element_type=F32)

    probs = [(ci, g) for ci in range(n_chunks) for g in range(n_groups)]

    gd_lhs, gcb_all, kq_lhs, kn_grp, vb_grp, kbe_grp = {}, {}, {}, {}, {}, {}
    for ci in range(n_chunks):
        r0 = ci * C
        ba = ba_ref[r0:r0 + C, :]
        beta_all = jax.nn.sigmoid(ba)
        xa = ba + dt_bias
        softplus = jnp.maximum(xa, 0.0) + jnp.log1p(jnp.exp(-jnp.abs(xa)))
        g = jnp.where((lane >= DN_A_LANE) & (lane < DN_A_LANE + DN_HEADS), neg_a * softplus, 0.0)
        g_hi, g_mid, g_lo = _split3(g)
        gc = (bdot(tri, g_hi.astype(BF16)) + bdot(tri, g_mid.astype(BF16))
              + bdot(tri, g_lo.astype(BF16)))
        c_hi, c_mid, c_lo = _split3(gc)
        gd_lhs[ci] = (c_hi + pltpu.roll(c_mid, DN_HEADS, axis=1) + pltpu.roll(c_lo, 2 * DN_HEADS, axis=1)
                      + jnp.where((lane >= DN_ONE_LANE) & (lane < DN_ONE_LANE + 3), 1.0, 0.0)).astype(BF16)

        beta_b = jnp.concatenate(
            [jnp.broadcast_to(beta_all[:, DN_B_LANE + h:DN_B_LANE + h + 1], (C, hd))
             for h in range(DN_HEADS)], axis=1)
        gcb = jnp.concatenate(
            [jnp.broadcast_to(gc[:, DN_A_LANE + h:DN_A_LANE + h + 1], (C, hd))
             for h in range(DN_HEADS)], axis=1)
        gcb_all[ci] = gcb
        eg = jnp.exp(gcb)
        glast = gcb[C - 1:C, :]
        dec_s[ci] = jnp.exp(glast)

        q_all = qkv[r0:r0 + C, 0:W]
        k_all = qkv[r0:r0 + C, W:2 * W]
        v_all = qkv[r0:r0 + C, 2 * W:3 * W]
        qn = jnp.concatenate(
            [q_all[:, h * hd:(h + 1) * hd] * lax.rsqrt(jnp.sum(
                q_all[:, h * hd:(h + 1) * hd] ** 2, axis=-1, keepdims=True) + L2_EPS)
             for h in range(DN_HEADS)], axis=1) * (hd ** -0.5)
        kn = jnp.concatenate(
            [k_all[:, h * hd:(h + 1) * hd] * lax.rsqrt(jnp.sum(
                k_all[:, h * hd:(h + 1) * hd] ** 2, axis=-1, keepdims=True) + L2_EPS)
             for h in range(DN_HEADS)], axis=1)
        kb = kn * beta_b
        qg_s[r0:r0 + C, :] = (qn * eg).astype(BF16)
        kd_s[r0:r0 + C, :] = (kn * jnp.exp(glast - gcb)).astype(BF16)
        for gi in range(n_groups):
            gl = slice(gi * gw, (gi + 1) * gw)
            kq_lhs[ci, gi] = jnp.concatenate([kb[:, gl], qn[:, gl]], axis=0).astype(BF16)
            kn_grp[ci, gi] = kn[:, gl]
            vb_grp[ci, gi] = (v_all * beta_b)[:, gl]
            kbe_grp[ci, gi] = (kb * eg)[:, gl]

    gdiff, kq = {}, {}
    for ci, gi in probs:
        blocks = []
        for h in range(gi * G, (gi + 1) * G):
            p_hi, p_mid, p_lo = _split3(gcb_all[ci][:, h * hd:(h + 1) * hd])
            onehot = ((lane == DN_A_LANE + h) | (lane == DN_A_LANE + DN_HEADS + h)
                      | (lane == DN_A_LANE + 2 * DN_HEADS + h))
            blocks.append(jnp.where(lane == DN_ONE_LANE, -p_hi, jnp.where(
                lane == DN_ONE_LANE + 1, -p_mid, jnp.where(
                    lane == DN_ONE_LANE + 2, -p_lo, jnp.where(onehot, 1.0, 0.0)))))
        gd_rhs = jnp.concatenate(blocks, axis=0).astype(BF16)
        gdiff[ci, gi] = _nt_dot(gd_lhs[ci], gd_rhs)
        kq[ci, gi] = _nt_dot(kq_lhs[ci, gi], block_diag(kn_grp[ci, gi], bd_wide))

    m, p = {}, {}
    for pr in probs:
        decay = jnp.exp(jnp.where(ri >= cj, gdiff[pr], -jnp.inf))
        m[pr] = -jnp.where(ri > cj, kq[pr][:C, :] * decay, 0.0)
        ci, gi = pr
        in_s[ci * C:(ci + 1) * C, gi * gc_w:(gi + 1) * gc_w] = (kq[pr][C:, :] * decay).astype(BF16)
        p[pr] = eye_cat + m[pr]

    n_levels = int(math.log2(C)) - 1
    for pr in probs:
        m[pr] = bdot(m[pr].astype(BF16), block_diag(m[pr], bd_sq))
    for lvl in range(n_levels):
        last = lvl == n_levels - 1
        res = {}
        for pr in probs:
            lhs = p[pr] if last else jnp.concatenate([p[pr], m[pr]], axis=0)
            res[pr] = bdot(lhs.astype(BF16), block_diag(m[pr], bd_sq))
        for pr in probs:
            p[pr] = p[pr] + res[pr][:C, :]
            if not last:
                m[pr] = res[pr][C:, :]

    for pr in probs:
        ci, gi = pr
        pb = p[pr].astype(BF16)
        rows = slice(ci * C, (ci + 1) * C)
        gl = slice(gi * gw, (gi + 1) * gw)
        u_s[rows, gl] = bdot(pb, block_diag(vb_grp[pr], bd_wide))
        w_s[rows, gl] = bdot(pb, block_diag(kbe_grp[pr], bd_wide)).astype(BF16)

    def chunk_step(ci, carry):
        r0 = pl.multiple_of(ci * C, C)
        rows = pl.ds(r0, C)
        sb = [s_ref[h].astype(BF16) for h in range(DN_HEADS)]
        ws = [bdot(jnp.concatenate([w_s[rows, h * hd:(h + 1) * hd], qg_s[rows, h * hd:(h + 1) * hd]],
                                   axis=0), sb[h]) for h in range(DN_HEADS)]
        v_new = u_s[rows, :] - jnp.concatenate([x[:C, :] for x in ws], axis=1)
        vnb = v_new.astype(BF16)
        o_intra = [bdot(in_s[rows, gi * gc_w:(gi + 1) * gc_w],
                        block_diag(v_new[:, gi * gw:(gi + 1) * gw], bd_wide))
                   for gi in range(n_groups)]
        dec = dec_s[ci]
        for h in range(DN_HEADS):
            hl = slice(h * hd, (h + 1) * hd)
            s_ref[h] = s_ref[h] * dec[:, hl] + lax.dot_general(
                kd_s[rows, hl], vnb[:, hl], (((0,), (0,)), ((), ())), preferred_element_type=F32)
        o = jnp.concatenate([x[C:, :] for x in ws], axis=1) + jnp.concatenate(o_intra, axis=1)
        for h in range(DN_HEADS):
            hl = slice(h * hd, (h + 1) * hd)
            oh = o[:, hl]
            ms = jnp.mean(oh * oh, axis=-1, keepdims=True)
            on = oh * lax.rsqrt(ms + RMS_EPS) * norm_w
            o_ref[rows, hl] = (on * _silu(z_ref[rows, hl].astype(F32))).astype(BF16)
        return carry

    lax.fori_loop(0, n_chunks, chunk_step, 0)


def _deltanet(proj3, ba3, conv_w, gate_params, norm_w, *, tb=256):
    B, L, _ = proj3.shape
    blk = lambda col: pl.BlockSpec((None, tb, DN_WIDTH), lambda b, t: (b, t, col))
    return pl.pallas_call(
        functools.partial(_dn_body, tb=tb),
        grid=(B, L // tb),
        in_specs=[
            blk(COL_DQ), blk(COL_DK), blk(COL_DV), blk(COL_DZ),
            pl.BlockSpec((None, tb, LANES), lambda b, t: (b, t, 0)),
            pl.BlockSpec((DN_SHORT_CONV, 3 * DN_WIDTH), lambda b, t: (0, 0)),
            pl.BlockSpec((2, LANES), lambda b, t: (0, 0)),
            pl.BlockSpec((1, DN_HEAD_DIM), lambda b, t: (0, 0)),
        ],
        out_specs=pl.BlockSpec((None, tb, DN_WIDTH), lambda b, t: (b, t, 0)),
        out_shape=jax.ShapeDtypeStruct((B, L, DN_WIDTH), BF16),
        scratch_shapes=[
            pltpu.VMEM((DN_HALO + tb, 3 * DN_WIDTH), F32),
            pltpu.VMEM((tb, 3 * DN_WIDTH), F32),
            pltpu.VMEM((tb, DN_WIDTH), F32),
            pltpu.VMEM((tb, DN_WIDTH), BF16),
            pltpu.VMEM((tb, DN_WIDTH), BF16),
            pltpu.VMEM((tb, DN_WIDTH), BF16),
            pltpu.VMEM((tb, DN_HEADS * DN_CHUNK), BF16),
            pltpu.VMEM((tb // DN_CHUNK, 1, DN_WIDTH), F32),
            pltpu.VMEM((DN_HEADS, DN_HEAD_DIM, DN_HEAD_DIM), F32),
        ],
        compiler_params=pltpu.CompilerParams(
            dimension_semantics=("parallel", "arbitrary"), vmem_limit_bytes=VMEM_LIMIT),
        name="gated_deltanet",
    )(proj3, proj3, proj3, proj3, ba3, conv_w, gate_params, norm_w)


def _rope_tables(positions):
    half = ATT_QK_DIM // 2
    inv_freq = ROPE_THETA ** (-jnp.arange(0, ATT_QK_DIM, 2, dtype=F32) / ATT_QK_DIM)
    ang = positions.astype(F32)[..., None] * inv_freq
    cos, sin = jnp.cos(ang), jnp.sin(ang)
    reps = LANES // half
    cosf = jnp.tile(cos, (1, 1, reps))
    sinf = jnp.concatenate([-sin, sin] * (reps // 2), axis=-1)
    return cosf, sinf


def kernel(x, positions, norm_w, w_in, lam_qk, attn_subln_w, w_attn_out, conv_dw_w, conv_dw_b,
           conv_ln_w, conv_ln_b, w_conv_out, dn_conv_w, dn_a_log, dn_dt_bias, dn_norm_w,
           w_dn_out, w_out, final_norm_w):
    B, L, _ = x.shape
    T = B * L
    x2d = x.reshape(T, D_MODEL)
    depth = norm_w.shape[0]
    cosf, sinf = _rope_tables(positions)
    pad_heads = lambda v: jnp.pad(v, (DN_A_LANE, LANES - DN_A_LANE - DN_HEADS))
    for l in range(depth):
        w = w_in[l]
        w_gate = w[:, ORIG_GATE:]
        w_ba = jnp.pad(w[:, ORIG_DB:ORIG_GATE], ((0, 0), (0, LANES - 2 * DN_HEADS)))
        proj, ba = _inproj(x2d, norm_w[l][None, :], w, w_gate, w_ba)
        proj3 = proj.reshape(B, L, N_MAIN)
        lambda_init = 0.8 - 0.6 * math.exp(-0.3 * l)
        ya = _attention(proj3, cosf, sinf, lam_qk[l], attn_subln_w[l][:, None],
                        lambda_init=lambda_init).reshape(T, ATT_HEADS * ATT_V_DIM)
        yc = _conv_branch(proj3, conv_dw_w[l], conv_dw_b[l][None, :], conv_ln_w[l][None, :],
                          conv_ln_b[l][None, :]).reshape(T, CONV_WIDTH)
        gate_params = jnp.stack([pad_heads(dn_a_log[l]), pad_heads(dn_dt_bias[l])])
        yd = _deltanet(proj3, ba.reshape(B, L, LANES), dn_conv_w[l], gate_params,
                       dn_norm_w[l][None, :]).reshape(T, DN_WIDTH)
        x2d = _merge(x2d, ya, yc, yd, proj, w_attn_out[l].astype(BF16), w_conv_out[l].astype(BF16),
                     w_dn_out[l].astype(BF16), w_out[l].astype(BF16), final_norm_w[None, :],
                     final_norm=(l == depth - 1))
    return x2d.reshape(B, L, D_MODEL)
```

```python
import functools
import math

import jax
import jax.numpy as jnp
from jax import lax
from jax.experimental import pallas as pl
from jax.experimental.pallas import tpu as pltpu

F32 = jnp.float32
BF16 = jnp.bfloat16

D_MODEL = 1024
ATT_HEADS = 8
ATT_QK_DIM = 64
ATT_V_DIM = 128
ROPE_THETA = 10000.0
CONV_WIDTH = 1024
CONV_KERNEL = 31
DN_HEADS = 8
DN_HEAD_DIM = 128
DN_WIDTH = DN_HEADS * DN_HEAD_DIM
DN_SHORT_CONV = 4
DN_CHUNK = 64
RMS_EPS = 1e-6
LN_EPS = 1e-5
L2_EPS = 1e-6

LANES = 128
SUBLANES = 8
VMEM_LIMIT = 56 * 1024 * 1024
COL_AQ, COL_AK, COL_AV, COL_AZ = 0, 1, 2, 3
COL_CA, COL_CG, COL_CZ = 4, 5, 6
COL_DQ, COL_DK, COL_DV, COL_DZ = 7, 8, 9, 10
COL_GATE = 11
N_LEAD_TILES = 11
N_MAIN = 14 * D_MODEL
ORIG_DB = N_LEAD_TILES * D_MODEL
ORIG_GATE = ORIG_DB + 2 * DN_HEADS


def _silu(x):
    return x * jax.nn.sigmoid(x)


def _merge_body(x_ref, a_ref, c_ref, d_ref, ga_ref, gc_ref, gd_ref,
                wa_ref, wc_ref, wd_ref, wo_ref, fw_ref, o_ref, *, final_norm):
    ya = jnp.dot(a_ref[...], wa_ref[...], preferred_element_type=F32)
    yc = jnp.dot(c_ref[...], wc_ref[...], preferred_element_type=F32)
    yd = jnp.dot(d_ref[...], wd_ref[...], preferred_element_type=F32)
    merged = (jax.nn.sigmoid(ga_ref[...].astype(F32)) * ya
              + jax.nn.sigmoid(gc_ref[...].astype(F32)) * yc
              + jax.nn.sigmoid(gd_ref[...].astype(F32)) * yd)
    y = x_ref[...] + jnp.dot(merged.astype(BF16), wo_ref[...], preferred_element_type=F32)
    if final_norm:
        ms = jnp.mean(y * y, axis=-1, keepdims=True)
        y = y * lax.rsqrt(ms + RMS_EPS) * fw_ref[...]
    o_ref[...] = y


def _merge(x2d, ya, yc, yd, proj, wa, wc, wd, wo, final_w, layer, *, final_norm, tm=512):
    T = x2d.shape[0]
    row = lambda i: (i, 0)
    const = lambda i: (0, 0)
    wspec = pl.BlockSpec((None, D_MODEL, D_MODEL), lambda i: (layer, 0, 0))
    return pl.pallas_call(
        functools.partial(_merge_body, final_norm=final_norm),
        grid=(T // tm,),
        in_specs=[
            pl.BlockSpec((tm, D_MODEL), row),
            pl.BlockSpec((tm, D_MODEL), row),
            pl.BlockSpec((tm, D_MODEL), row),
            pl.BlockSpec((tm, D_MODEL), row),
            pl.BlockSpec((tm, D_MODEL), lambda i: (i, COL_GATE)),
            pl.BlockSpec((tm, D_MODEL), lambda i: (i, COL_GATE + 1)),
            pl.BlockSpec((tm, D_MODEL), lambda i: (i, COL_GATE + 2)),
            wspec, wspec, wspec, wspec,
            pl.BlockSpec((1, D_MODEL), const),
        ],
        out_specs=pl.BlockSpec((tm, D_MODEL), row),
        out_shape=jax.ShapeDtypeStruct((T, D_MODEL), F32),
        compiler_params=pltpu.CompilerParams(
            dimension_semantics=("parallel",), vmem_limit_bytes=VMEM_LIMIT),
        name="merge",
    )(x2d, ya, yc, yd, proj, proj, proj, wa, wc, wd, wo, final_w)


CONV_HALO = 32


def _causal_taps(buf_ref, w_ref, r0, rows, lanes, n_taps, halo):
    assert halo >= SUBLANES * ((n_taps - 1) // SUBLANES + 1)
    acc = None
    for r in range(min(SUBLANES, n_taps)):
        part = None
        for a in range((n_taps - 1 - r) // SUBLANES + 1):
            s = SUBLANES * a + r
            start = r0 + (halo - SUBLANES * (a + 1))
            if not isinstance(start, int):
                start = pl.multiple_of(start, SUBLANES)
            term = (w_ref[n_taps - 1 - s:n_taps - s, lanes]
                    * buf_ref[pl.ds(start, rows + SUBLANES), lanes])
            part = term if part is None else part + term
        if r:
            part = pltpu.roll(part, r, axis=0)
        acc = part if acc is None else acc + part
    return acc[SUBLANES:, :]


def _inproj_body(x_ref, nw_ref, w_ref, wg_ref, wba_ref, o_ref, ba_ref, h_ref):
    j = pl.program_id(1)

    @pl.when(j == 0)
    def _():
        x = x_ref[...]
        ms = jnp.mean(x * x, axis=-1, keepdims=True)
        h = (x * lax.rsqrt(ms + RMS_EPS) * nw_ref[...]).astype(BF16)
        h_ref[...] = h
        ba_ref[...] = jnp.dot(h, wba_ref[...], preferred_element_type=F32)

    @pl.when(j < N_LEAD_TILES)
    def _():
        o_ref[...] = jnp.dot(h_ref[...], w_ref[...], preferred_element_type=F32).astype(BF16)

    @pl.when(j >= N_LEAD_TILES)
    def _():
        o_ref[...] = jnp.dot(h_ref[...], wg_ref[...], preferred_element_type=F32).astype(BF16)


def _inproj(x2d, norm_w, w_all, w_gate, w_ba, layer, *, tm=2048):
    T = x2d.shape[0]
    tn = D_MODEL
    const = lambda i, j: (0, 0)
    return pl.pallas_call(
        _inproj_body,
        grid=(T // tm, N_MAIN // tn),
        in_specs=[
            pl.BlockSpec((tm, D_MODEL), lambda i, j: (i, 0)),
            pl.BlockSpec((1, D_MODEL), const),
            pl.BlockSpec((None, D_MODEL, tn),
                         lambda i, j: (layer, 0, jnp.minimum(j, N_LEAD_TILES - 1))),
            pl.BlockSpec((None, D_MODEL, tn),
                         lambda i, j: (layer, 0, jnp.maximum(j - N_LEAD_TILES, 0))),
            pl.BlockSpec((None, D_MODEL, LANES), lambda i, j: (layer, 0, 0)),
        ],
        out_specs=[
            pl.BlockSpec((tm, tn), lambda i, j: (i, j)),
            pl.BlockSpec((tm, LANES), lambda i, j: (i, 0)),
        ],
        out_shape=[
            jax.ShapeDtypeStruct((T, N_MAIN), BF16),
            jax.ShapeDtypeStruct((T, LANES), F32),
        ],
        scratch_shapes=[pltpu.VMEM((tm, D_MODEL), BF16)],
        compiler_params=pltpu.CompilerParams(
            dimension_semantics=("parallel", "arbitrary"), vmem_limit_bytes=VMEM_LIMIT),
        name="inproj",
    )(x2d, norm_w, w_all, w_gate, w_ba)


CONV_ROWS = 128
CONV_STRIP = 128


def _conv_body(a_ref, g_ref, z_ref, w_ref, b_ref, lnw_ref, lnb_ref, o_ref, ubuf, cbuf, *, tl):
    @pl.when(pl.program_id(1) == 0)
    def _():
        ubuf[0:CONV_HALO, :] = jnp.zeros((CONV_HALO, CONV_WIDTH), F32)

    a = a_ref[...].astype(F32)
    g = g_ref[...].astype(F32)
    ubuf[CONV_HALO:CONV_HALO + tl, :] = a * jax.nn.sigmoid(g)

    bias = b_ref[...]
    lnw = lnw_ref[...]
    lnb = lnb_ref[...]

    def step(r, carry):
        r0 = pl.multiple_of(r * CONV_ROWS, CONV_ROWS)
        for c in range(CONV_WIDTH // CONV_STRIP):
            lanes = slice(c * CONV_STRIP, (c + 1) * CONV_STRIP)
            cbuf[:, lanes] = _causal_taps(ubuf, w_ref, r0, CONV_ROWS, lanes, CONV_KERNEL, CONV_HALO)
        u = cbuf[...] + bias
        mu = jnp.mean(u, axis=-1, keepdims=True)
        uc = u - mu
        var = jnp.mean(uc * uc, axis=-1, keepdims=True)
        y = uc * lax.rsqrt(var + LN_EPS) * lnw + lnb
        zz = z_ref[pl.ds(r0, CONV_ROWS), :].astype(F32)
        o_ref[pl.ds(r0, CONV_ROWS), :] = (_silu(y) * _silu(zz)).astype(BF16)
        return carry

    lax.fori_loop(0, tl // CONV_ROWS, step, 0)
    ubuf[0:CONV_HALO, :] = ubuf[tl:tl + CONV_HALO, :]


def _conv_branch(proj3, dw_w, dw_b, ln_w, ln_b, *, tl=512):
    B, L, _ = proj3.shape
    vec = pl.BlockSpec((1, CONV_WIDTH), lambda b, t: (0, 0))
    return pl.pallas_call(
        functools.partial(_conv_body, tl=tl),
        grid=(B, L // tl),
        in_specs=[
            pl.BlockSpec((None, tl, CONV_WIDTH), lambda b, t: (b, t, COL_CA)),
            pl.BlockSpec((None, tl, CONV_WIDTH), lambda b, t: (b, t, COL_CG)),
            pl.BlockSpec((None, tl, CONV_WIDTH), lambda b, t: (b, t, COL_CZ)),
            pl.BlockSpec((CONV_KERNEL, CONV_WIDTH), lambda b, t: (0, 0)),
            vec, vec, vec,
        ],
        out_specs=pl.BlockSpec((None, tl, CONV_WIDTH), lambda b, t: (b, t, 0)),
        out_shape=jax.ShapeDtypeStruct((B, L, CONV_WIDTH), BF16),
        scratch_shapes=[pltpu.VMEM((CONV_HALO + tl, CONV_WIDTH), F32),
                        pltpu.VMEM((CONV_ROWS, CONV_WIDTH), F32)],
        compiler_params=pltpu.CompilerParams(
            dimension_semantics=("parallel", "arbitrary"), vmem_limit_bytes=VMEM_LIMIT),
        name="conv_branch",
    )(proj3, proj3, proj3, dw_w, dw_b, ln_w, ln_b)


def _nt_dot(a, b):
    return lax.dot_general(a, b, (((1,), (1,)), ((), ())), preferred_element_type=F32)


def _rope(t, cos, sin_signed):
    lane = lax.broadcasted_iota(jnp.int32, t.shape, 1)
    first_half = (lane % ATT_QK_DIM) < (ATT_QK_DIM // 2)
    half = ATT_QK_DIM // 2
    swapped = jnp.where(first_half, pltpu.roll(t, t.shape[1] - half, axis=1),
                        pltpu.roll(t, half, axis=1))
    return t * cos + swapped * sin_signed


ATT_HPS = 2
ATT_ONES_ROWS = 16


def _attn_body(q_ref, k_ref, v_ref, z_ref, cq_ref, sq_ref, ck_ref, sk_ref, lam_ref, sw_ref, o_ref,
               kr_ref, vt_ref, m_ref, alpha_ref, acc_ref, s_ref, p_ref, *, lambda_init, tq):
    qi = pl.program_id(2)
    heads = range(ATT_HPS)
    hl = [slice(h * LANES, (h + 1) * LANES) for h in heads]
    cos_q = jnp.concatenate([cq_ref[...]] * ATT_HPS, axis=1)
    sin_q = jnp.concatenate([sq_ref[...]] * ATT_HPS, axis=1)

    @pl.when(qi == 0)
    def _():
        cos_k = jnp.concatenate([ck_ref[...]] * ATT_HPS, axis=1)
        sin_k = jnp.concatenate([sk_ref[...]] * ATT_HPS, axis=1)
        kr_ref[...] = _rope(k_ref[...].astype(F32), cos_k, sin_k).astype(BF16)

        ri = lax.broadcasted_iota(jnp.int32, (ATT_V_DIM, ATT_V_DIM), 0)
        ci = lax.broadcasted_iota(jnp.int32, (ATT_V_DIM, ATT_V_DIM), 1)
        ident = (ri == ci).astype(BF16)
        for jb in range(vt_ref.shape[1]):
            for h in heads:
                vt_ref[h, jb, 0:ATT_V_DIM, :] = _nt_dot(
                    ident, v_ref[jb * tq:(jb + 1) * tq, hl[h]]).astype(BF16)
                vt_ref[h, jb, ATT_V_DIM:, :] = jnp.ones((ATT_ONES_ROWS, tq), BF16)

    q = _rope(q_ref[...].astype(F32), cos_q, sin_q) * (ATT_QK_DIM ** -0.5 * math.log2(math.e))
    lane = lax.broadcasted_iota(jnp.int32, q.shape, 1) % LANES
    q_maps = jnp.concatenate([jnp.where(lane < ATT_QK_DIM, q, 0.0),
                              jnp.where(lane >= ATT_QK_DIM, q, 0.0)], axis=0).astype(BF16)
    qs = [q_maps[:, hl[h]] for h in heads]

    def scores(h, j):
        k0 = pl.multiple_of(j * tq, tq)
        return _nt_dot(kr_ref[pl.ds(k0, tq), hl[h]], qs[h])

    def softmax_step(h, s):
        m_prev = m_ref[h]
        m_new = jnp.maximum(m_prev, jnp.max(s, axis=0, keepdims=True))
        alpha = jnp.exp2(m_prev - m_new)
        p = jnp.exp2(s - m_new)
        m_ref[h] = m_new
        return alpha, p.astype(BF16)

    def pv(h, j, p):
        return jnp.dot(vt_ref[h, j], p, preferred_element_type=F32)

    m_ref[...] = jnp.full(m_ref.shape, -jnp.inf, F32)
    acc_ref[...] = jnp.zeros(acc_ref.shape, F32)
    alpha_ref[...] = jnp.ones(alpha_ref.shape, F32)
    for h in heads:
        p_ref[h, 1] = jnp.zeros(p_ref.shape[2:], BF16)
        s_ref[h, 0] = scores(h, 0)

    def trip(t, carry):
        slot = t % 2
        pv_prev = [pv(h, jnp.maximum(t - 1, 0), p_ref[h, 1 - slot]) for h in heads]
        s_next = [scores(h, t + 1) for h in heads]
        for h in heads:
            alpha, p = softmax_step(h, s_ref[h, slot])
            p_ref[h, slot] = p
            acc_ref[h] = alpha_ref[h] * acc_ref[h] + pv_prev[h]
            alpha_ref[h] = alpha
        for h in heads:
            s_ref[h, 1 - slot] = s_next[h]
        return carry

    lax.fori_loop(0, qi, trip, 0)

    slot = qi % 2
    pv_prev = [pv(h, jnp.maximum(qi - 1, 0), p_ref[h, 1 - slot]) for h in heads]
    key = lax.broadcasted_iota(jnp.int32, (tq, 2 * tq), 0)
    qry = lax.broadcasted_iota(jnp.int32, (tq, 2 * tq), 1) % tq
    last = [softmax_step(h, jnp.where(key <= qry, s_ref[h, slot], -jnp.inf)) for h in heads]
    pv_last = [pv(h, qi, last[h][1]) for h in heads]

    lq = lam_ref[...]
    lam = (jnp.exp(jnp.sum(lq[0:1, :] * lq[1:2, :], axis=-1, keepdims=True))
           - jnp.exp(jnp.sum(lq[2:3, :] * lq[3:4, :], axis=-1, keepdims=True)) + lambda_init)
    outs = []
    for h in heads:
        acc = last[h][0] * (alpha_ref[h] * acc_ref[h] + pv_prev[h]) + pv_last[h]
        o_all = acc[:ATT_V_DIM, :] / acc[ATT_V_DIM:ATT_V_DIM + 1, :]
        ot = o_all[:, :tq] - lam * o_all[:, tq:]
        ms = jnp.mean(ot * ot, axis=0, keepdims=True)
        outs.append((ot * lax.rsqrt(ms + RMS_EPS) * sw_ref[...] * (1.0 - lambda_init)).T)
    on = jnp.concatenate(outs, axis=1)
    o_ref[...] = (on * _silu(z_ref[...].astype(F32))).astype(BF16)


def _attention(proj3, cosf, sinf, lam_qk, subln_w, *, lambda_init, tq=256):
    B, L, _ = proj3.shape
    hw = ATT_HPS * LANES
    hpb = D_MODEL // hw
    qblk = lambda col: pl.BlockSpec((None, tq, hw), lambda b, h, i: (b, i, col * hpb + h))
    full = lambda col: pl.BlockSpec((None, L, hw), lambda b, h, i: (b, 0, col * hpb + h))
    return pl.pallas_call(
        functools.partial(_attn_body, lambda_init=lambda_init, tq=tq),
        grid=(B, ATT_HEADS // ATT_HPS, L // tq),
        in_specs=[
            qblk(COL_AQ), full(COL_AK), full(COL_AV), qblk(COL_AZ),
            pl.BlockSpec((None, tq, LANES), lambda b, h, i: (b, i, 0)),
            pl.BlockSpec((None, tq, LANES), lambda b, h, i: (b, i, 0)),
            pl.BlockSpec((None, L, LANES), lambda b, h, i: (b, 0, 0)),
            pl.BlockSpec((None, L, LANES), lambda b, h, i: (b, 0, 0)),
            pl.BlockSpec((4, ATT_QK_DIM), lambda b, h, i: (0, 0)),
            pl.BlockSpec((ATT_V_DIM, 1), lambda b, h, i: (0, 0)),
        ],
        out_specs=pl.BlockSpec((None, tq, hw), lambda b, h, i: (b, i, h)),
        out_shape=jax.ShapeDtypeStruct((B, L, ATT_HEADS * ATT_V_DIM), BF16),
        scratch_shapes=[
            pltpu.VMEM((L, hw), BF16),
            pltpu.VMEM((ATT_HPS, L // tq, ATT_V_DIM + ATT_ONES_ROWS, tq), BF16),
            pltpu.VMEM((ATT_HPS, 1, 2 * tq), F32),
            pltpu.VMEM((ATT_HPS, 1, 2 * tq), F32),
            pltpu.VMEM((ATT_HPS, ATT_V_DIM + ATT_ONES_ROWS, 2 * tq), F32),
            pltpu.VMEM((ATT_HPS, 2, tq, 2 * tq), F32),
            pltpu.VMEM((ATT_HPS, 2, tq, 2 * tq), BF16),
        ],
        compiler_params=pltpu.CompilerParams(
            dimension_semantics=("parallel", "parallel", "arbitrary"), vmem_limit_bytes=VMEM_LIMIT),
        name="diff_attention",
    )(proj3, proj3, proj3, proj3, cosf, sinf, cosf, sinf, lam_qk, subln_w)


DN_HALO = 8
DN_STRIP = 256
DN_B_LANE = 0
DN_A_LANE = DN_HEADS
DN_ONE_LANE = DN_A_LANE + 3 * DN_HEADS
DN_GROUP = 4


def _split3(x):
    hi = x.astype(BF16).astype(F32)
    r1 = x - hi
    mid = r1.astype(BF16).astype(F32)
    lo = (r1 - mid).astype(BF16).astype(F32)
    return hi, mid, lo


def _dn_body(q_ref, k_ref, v_ref, z_ref, ba_ref, cw_ref, gp_ref, nw_ref, o_ref,
             xbuf, qkv, u_s, w_s, qg_s, kd_s, in_s, dec_s, s_ref, *, tb):
    C = DN_CHUNK
    W = DN_WIDTH
    hd = DN_HEAD_DIM

    @pl.when(pl.program_id(1) == 0)
    def _():
        xbuf[0:DN_HALO, :] = jnp.zeros((DN_HALO, 3 * W), F32)
        s_ref[...] = jnp.zeros(s_ref.shape, F32)

    xbuf[DN_HALO:DN_HALO + tb, 0:W] = q_ref[...].astype(F32)
    xbuf[DN_HALO:DN_HALO + tb, W:2 * W] = k_ref[...].astype(F32)
    xbuf[DN_HALO:DN_HALO + tb, 2 * W:3 * W] = v_ref[...].astype(F32)

    def conv_step(i, carry):
        r0 = pl.multiple_of(i * C, C)
        for c in range(3 * W // DN_STRIP):
            lanes = slice(c * DN_STRIP, (c + 1) * DN_STRIP)
            qkv[pl.ds(r0, C), lanes] = _silu(
                _causal_taps(xbuf, cw_ref, r0, C, lanes, DN_SHORT_CONV, DN_HALO))
        return carry

    lax.fori_loop(0, tb // C, conv_step, 0)
    xbuf[0:DN_HALO, :] = xbuf[tb:tb + DN_HALO, :]

    n_chunks = tb // C
    G = DN_GROUP
    n_groups = DN_HEADS // G
    gw = G * hd
    gc_w = G * C

    row = lax.broadcasted_iota(jnp.int32, (C, C), 0)
    col = lax.broadcasted_iota(jnp.int32, (C, C), 1)
    tri = (row >= col).astype(BF16)
    lane = lax.broadcasted_iota(jnp.int32, (C, LANES), 1)
    ri = lax.broadcasted_iota(jnp.int32, (C, gc_w), 0)
    cj = lax.broadcasted_iota(jnp.int32, (C, gc_w), 1) % C
    eye_cat = (ri == cj).astype(F32)
    bd_sq = (lax.broadcasted_iota(jnp.int32, (gc_w, gc_w), 0) // C
             == lax.broadcasted_iota(jnp.int32, (gc_w, gc_w), 1) // C)
    bd_wide = (lax.broadcasted_iota(jnp.int32, (gc_w, gw), 0) // C
               == lax.broadcasted_iota(jnp.int32, (gc_w, gw), 1) // hd)
    neg_a = -jnp.exp(gp_ref[0:1, :])
    dt_bias = gp_ref[1:2, :]
    norm_w = nw_ref[...]

    def block_diag(x, mask):
        return jnp.where(mask, jnp.tile(x, (G, 1)), 0.0).astype(BF16)

    def bdot(a, b):
        return jnp.dot(a, b, preferred_element_type=F32)

    probs = [(ci, g) for ci in range(n_chunks) for g in range(n_groups)]

    gd_lhs, gcb_all, kq_lhs, kn_grp, vb_grp, kbe_grp = {}, {}, {}, {}, {}, {}
    for ci in range(n_chunks):
        r0 = ci * C
        ba = ba_ref[r0:r0 + C, :]
        beta_all = jax.nn.sigmoid(ba)
        xa = ba + dt_bias
        softplus = jnp.maximum(xa, 0.0) + jnp.log1p(jnp.exp(-jnp.abs(xa)))
        g = jnp.where((lane >= DN_A_LANE) & (lane < DN_A_LANE + DN_HEADS), neg_a * softplus, 0.0)
        g_hi, g_mid, g_lo = _split3(g)
        gc = (bdot(tri, g_hi.astype(BF16)) + bdot(tri, g_mid.astype(BF16))
              + bdot(tri, g_lo.astype(BF16)))
        c_hi, c_mid, c_lo = _split3(gc)
        gd_lhs[ci] = (c_hi + pltpu.roll(c_mid, DN_HEADS, axis=1) + pltpu.roll(c_lo, 2 * DN_HEADS, axis=1)
                      + jnp.where((lane >= DN_ONE_LANE) & (lane < DN_ONE_LANE + 3), 1.0, 0.0)).astype(BF16)

        beta_b = jnp.concatenate(
            [jnp.broadcast_to(beta_all[:, DN_B_LANE + h:DN_B_LANE + h + 1], (C, hd))
             for h in range(DN_HEADS)], axis=1)
        gcb = jnp.concatenate(
            [jnp.broadcast_to(gc[:, DN_A_LANE + h:DN_A_LANE + h + 1], (C, hd))
             for h in range(DN_HEADS)], axis=1)
        gcb_all[ci] = gcb
        eg = jnp.exp(gcb)
        glast = gcb[C - 1:C, :]
        dec_s[ci] = jnp.exp(glast)

        q_all = qkv[r0:r0 + C, 0:W]
        k_all = qkv[r0:r0 + C, W:2 * W]
        v_all = qkv[r0:r0 + C, 2 * W:3 * W]
        qn = jnp.concatenate(
            [q_all[:, h * hd:(h + 1) * hd] * lax.rsqrt(jnp.sum(
                q_all[:, h * hd:(h + 1) * hd] ** 2, axis=-1, keepdims=True) + L2_EPS)
             for h in range(DN_HEADS)], axis=1) * (hd ** -0.5)
        kn = jnp.concatenate(
            [k_all[:, h * hd:(h + 1) * hd] * lax.rsqrt(jnp.sum(
                k_all[:, h * hd:(h + 1) * hd] ** 2, axis=-1, keepdims=True) + L2_EPS)
             for h in range(DN_HEADS)], axis=1)
        kb = kn * beta_b
        qg_s[r0:r0 + C, :] = (qn * eg).astype(BF16)
        kd_s[r0:r0 + C, :] = (kn * jnp.exp(glast - gcb)).astype(BF16)
        for gi in range(n_groups):
            gl = slice(gi * gw, (gi + 1) * gw)
            kq_lhs[ci, gi] = jnp.concatenate([kb[:, gl], qn[:, gl]], axis=0).astype(BF16)
            kn_grp[ci, gi] = kn[:, gl]
            vb_grp[ci, gi] = (v_all * beta_b)[:, gl]
            kbe_grp[ci, gi] = (kb * eg)[:, gl]

    gdiff, kq = {}, {}
    for ci, gi in probs:
        blocks = []
        for h in range(gi * G, (gi + 1) * G):
            p_hi, p_mid, p_lo = _split3(gcb_all[ci][:, h * hd:(h + 1) * hd])
            onehot = ((lane == DN_A_LANE + h) | (lane == DN_A_LANE + DN_HEADS + h)
                      | (lane == DN_A_LANE + 2 * DN_HEADS + h))
            blocks.append(jnp.where(lane == DN_ONE_LANE, -p_hi, jnp.where(
                lane == DN_ONE_LANE + 1, -p_mid, jnp.where(
                    lane == DN_ONE_LANE + 2, -p_lo, jnp.where(onehot, 1.0, 0.0)))))
        gd_rhs = jnp.concatenate(blocks, axis=0).astype(BF16)
        gdiff[ci, gi] = _nt_dot(gd_lhs[ci], gd_rhs)
        kq[ci, gi] = _nt_dot(kq_lhs[ci, gi], block_diag(kn_grp[ci, gi], bd_wide))

    m, p = {}, {}
    for pr in probs:
        decay = jnp.exp(jnp.where(ri >= cj, gdiff[pr], -jnp.inf))
        m[pr] = -jnp.where(ri > cj, kq[pr][:C, :] * decay, 0.0)
        ci, gi = pr
        in_s[ci * C:(ci + 1) * C, gi * gc_w:(gi + 1) * gc_w] = (kq[pr][C:, :] * decay).astype(BF16)
        p[pr] = eye_cat + m[pr]

    n_levels = int(math.log2(C)) - 1
    for pr in probs:
        m[pr] = bdot(m[pr].astype(BF16), block_diag(m[pr], bd_sq))
    for lvl in range(n_levels):
        last = lvl == n_levels - 1
        res = {}
        for pr in probs:
            lhs = p[pr] if last else jnp.concatenate([p[pr], m[pr]], axis=0)
            res[pr] = bdot(lhs.astype(BF16), block_diag(m[pr], bd_sq))
        for pr in probs:
            p[pr] = p[pr] + res[pr][:C, :]
            if not last:
                m[pr] = res[pr][C:, :]

    for pr in probs:
        ci, gi = pr
        pb = p[pr].astype(BF16)
        rows = slice(ci * C, (ci + 1) * C)
        gl = slice(gi * gw, (gi + 1) * gw)
        u_s[rows, gl] = bdot(pb, block_diag(vb_grp[pr], bd_wide))
        w_s[rows, gl] = bdot(pb, block_diag(kbe_grp[pr], bd_wide)).astype(BF16)

    def chunk_step(ci, carry):
        r0 = pl.multiple_of(ci * C, C)
        rows = pl.ds(r0, C)
        sb = [s_ref[h].astype(BF16) for h in range(DN_HEADS)]
        ws = [bdot(jnp.concatenate([w_s[rows, h * hd:(h + 1) * hd], qg_s[rows, h * hd:(h + 1) * hd]],
                                   axis=0), sb[h]) for h in range(DN_HEADS)]
        v_new = u_s[rows, :] - jnp.concatenate([x[:C, :] for x in ws], axis=1)
        vnb = v_new.astype(BF16)
        o_intra = [bdot(in_s[rows, gi * gc_w:(gi + 1) * gc_w],
                        block_diag(v_new[:, gi * gw:(gi + 1) * gw], bd_wide))
                   for gi in range(n_groups)]
        dec = dec_s[ci]
        for h in range(DN_HEADS):
            hl = slice(h * hd, (h + 1) * hd)
            s_ref[h] = s_ref[h] * dec[:, hl] + lax.dot_general(
                kd_s[rows, hl], vnb[:, hl], (((0,), (0,)), ((), ())), preferred_element_type=F32)
        o = jnp.concatenate([x[C:, :] for x in ws], axis=1) + jnp.concatenate(o_intra, axis=1)
        for h in range(DN_HEADS):
            hl = slice(h * hd, (h + 1) * hd)
            oh = o[:, hl]
            ms = jnp.mean(oh * oh, axis=-1, keepdims=True)
            on = oh * lax.rsqrt(ms + RMS_EPS) * norm_w
            o_ref[rows, hl] = (on * _silu(z_ref[rows, hl].astype(F32))).astype(BF16)
        return carry

    lax.fori_loop(0, n_chunks, chunk_step, 0)


def _deltanet(proj3, ba3, conv_w, gate_params, norm_w, *, tb=256):
    B, L, _ = proj3.shape
    blk = lambda col: pl.BlockSpec((None, tb, DN_WIDTH), lambda b, t: (b, t, col))
    return pl.pallas_call(
        functools.partial(_dn_body, tb=tb),
        grid=(B, L // tb),
        in_specs=[
            blk(COL_DQ), blk(COL_DK), blk(COL_DV), blk(COL_DZ),
            pl.BlockSpec((None, tb, LANES), lambda b, t: (b, t, 0)),
            pl.BlockSpec((DN_SHORT_CONV, 3 * DN_WIDTH), lambda b, t: (0, 0)),
            pl.BlockSpec((2, LANES), lambda b, t: (0, 0)),
            pl.BlockSpec((1, DN_HEAD_DIM), lambda b, t: (0, 0)),
        ],
        out_specs=pl.BlockSpec((None, tb, DN_WIDTH), lambda b, t: (b, t, 0)),
        out_shape=jax.ShapeDtypeStruct((B, L, DN_WIDTH), BF16),
        scratch_shapes=[
            pltpu.VMEM((DN_HALO + tb, 3 * DN_WIDTH), F32),
            pltpu.VMEM((tb, 3 * DN_WIDTH), F32),
            pltpu.VMEM((tb, DN_WIDTH), F32),
            pltpu.VMEM((tb, DN_WIDTH), BF16),
            pltpu.VMEM((tb, DN_WIDTH), BF16),
            pltpu.VMEM((tb, DN_WIDTH), BF16),
            pltpu.VMEM((tb, DN_HEADS * DN_CHUNK), BF16),
            pltpu.VMEM((tb // DN_CHUNK, 1, DN_WIDTH), F32),
            pltpu.VMEM((DN_HEADS, DN_HEAD_DIM, DN_HEAD_DIM), F32),
        ],
        compiler_params=pltpu.CompilerParams(
            dimension_semantics=("parallel", "arbitrary"), vmem_limit_bytes=VMEM_LIMIT),
        name="gated_deltanet",
    )(proj3, proj3, proj3, proj3, ba3, conv_w, gate_params, norm_w)


def _rope_tables(positions):
    half = ATT_QK_DIM // 2
    inv_freq = ROPE_THETA ** (-jnp.arange(0, ATT_QK_DIM, 2, dtype=F32) / ATT_QK_DIM)
    reps = LANES // half
    freq_lanes = jnp.tile(inv_freq, reps)
    sign_lanes = jnp.tile(jnp.repeat(jnp.array([-1.0, 1.0], F32), half), reps // 2)
    ang = positions.astype(F32)[..., None] * freq_lanes
    return jnp.cos(ang), jnp.sin(ang) * sign_lanes


def kernel(x, positions, norm_w, w_in, lam_qk, attn_subln_w, w_attn_out, conv_dw_w, conv_dw_b,
           conv_ln_w, conv_ln_b, w_conv_out, dn_conv_w, dn_a_log, dn_dt_bias, dn_norm_w,
           w_dn_out, w_out, final_norm_w):
    B, L, _ = x.shape
    T = B * L
    x2d = x.reshape(T, D_MODEL)
    depth = norm_w.shape[0]
    cosf, sinf = _rope_tables(positions)
    pad_heads = lambda v: jnp.pad(v, (DN_A_LANE, LANES - DN_A_LANE - DN_HEADS))
    w_all = w_in.astype(BF16)
    w_gate = w_all[:, :, ORIG_GATE:]
    w_ba = jnp.pad(w_all[:, :, ORIG_DB:ORIG_GATE], ((0, 0), (0, 0), (0, LANES - 2 * DN_HEADS)))
    wa, wc, wd, wo = (w.astype(BF16) for w in (w_attn_out, w_conv_out, w_dn_out, w_out))
    for l in range(depth):
        proj, ba = _inproj(x2d, norm_w[l][None, :], w_all, w_gate, w_ba, l)
        proj3 = proj.reshape(B, L, N_MAIN)
        lambda_init = 0.8 - 0.6 * math.exp(-0.3 * l)
        ya = _attention(proj3, cosf, sinf, lam_qk[l], attn_subln_w[l][:, None],
                        lambda_init=lambda_init).reshape(T, ATT_HEADS * ATT_V_DIM)
        yc = _conv_branch(proj3, conv_dw_w[l], conv_dw_b[l][None, :], conv_ln_w[l][None, :],
                          conv_ln_b[l][None, :]).reshape(T, CONV_WIDTH)
        gate_params = jnp.stack([pad_heads(dn_a_log[l]), pad_heads(dn_dt_bias[l])])
        yd = _deltanet(proj3, ba.reshape(B, L, LANES), dn_conv_w[l], gate_params,
                       dn_norm_w[l][None, :]).reshape(T, DN_WIDTH)
        x2d = _merge(x2d, ya, yc, yd, proj, wa, wc, wd, wo, final_norm_w[None, :], l,
                     final_norm=(l == depth - 1))
    return x2d.reshape(B, L, D_MODEL)
```

```python
import functools
import math

import jax
import jax.numpy as jnp
from jax import lax
from jax.experimental import pallas as pl
from jax.experimental.pallas import tpu as pltpu

F32 = jnp.float32
BF16 = jnp.bfloat16

D_MODEL = 1024
ATT_HEADS = 8
ATT_QK_DIM = 64
ATT_V_DIM = 128
ROPE_THETA = 10000.0
CONV_WIDTH = 1024
CONV_KERNEL = 31
DN_HEADS = 8
DN_HEAD_DIM = 128
DN_WIDTH = DN_HEADS * DN_HEAD_DIM
DN_SHORT_CONV = 4
DN_CHUNK = 64
RMS_EPS = 1e-6
LN_EPS = 1e-5
L2_EPS = 1e-6

LANES = 128
SUBLANES = 8
VMEM_LIMIT = 56 * 1024 * 1024
COL_AQ, COL_AK, COL_AV, COL_AZ = 0, 1, 2, 3
COL_CA, COL_CG, COL_CZ = 4, 5, 6
COL_DQ, COL_DK, COL_DV, COL_DZ = 7, 8, 9, 10
COL_GATE = 11
N_LEAD_TILES = 11
N_MAIN = 14 * D_MODEL
ORIG_DB = N_LEAD_TILES * D_MODEL
ORIG_GATE = ORIG_DB + 2 * DN_HEADS


def _silu(x):
    return x * jax.nn.sigmoid(x)


def _merge_body(x_ref, a_ref, c_ref, d_ref, ga_ref, gc_ref, gd_ref,
                wa_ref, wc_ref, wd_ref, wo_ref, fw_ref, o_ref, *, final_norm):
    ya = jnp.dot(a_ref[...], wa_ref[...], preferred_element_type=F32)
    yc = jnp.dot(c_ref[...], wc_ref[...], preferred_element_type=F32)
    yd = jnp.dot(d_ref[...], wd_ref[...], preferred_element_type=F32)
    merged = (jax.nn.sigmoid(ga_ref[...].astype(F32)) * ya
              + jax.nn.sigmoid(gc_ref[...].astype(F32)) * yc
              + jax.nn.sigmoid(gd_ref[...].astype(F32)) * yd)
    y = x_ref[...] + jnp.dot(merged.astype(BF16), wo_ref[...], preferred_element_type=F32)
    if final_norm:
        ms = jnp.mean(y * y, axis=-1, keepdims=True)
        y = y * lax.rsqrt(ms + RMS_EPS) * fw_ref[...]
    o_ref[...] = y


def _merge(x2d, ya, yc, yd, proj, wa, wc, wd, wo, final_w, layer, *, final_norm, tm=512):
    T = x2d.shape[0]
    row = lambda i: (i, 0)
    const = lambda i: (0, 0)
    wspec = pl.BlockSpec((None, D_MODEL, D_MODEL), lambda i: (layer, 0, 0))
    return pl.pallas_call(
        functools.partial(_merge_body, final_norm=final_norm),
        grid=(T // tm,),
        in_specs=[
            pl.BlockSpec((tm, D_MODEL), row),
            pl.BlockSpec((tm, D_MODEL), row),
            pl.BlockSpec((tm, D_MODEL), row),
            pl.BlockSpec((tm, D_MODEL), row),
            pl.BlockSpec((tm, D_MODEL), lambda i: (i, COL_GATE)),
            pl.BlockSpec((tm, D_MODEL), lambda i: (i, COL_GATE + 1)),
            pl.BlockSpec((tm, D_MODEL), lambda i: (i, COL_GATE + 2)),
            wspec, wspec, wspec, wspec,
            pl.BlockSpec((1, D_MODEL), const),
        ],
        out_specs=pl.BlockSpec((tm, D_MODEL), row),
        out_shape=jax.ShapeDtypeStruct((T, D_MODEL), F32),
        compiler_params=pltpu.CompilerParams(
            dimension_semantics=("parallel",), vmem_limit_bytes=VMEM_LIMIT),
        name="merge",
    )(x2d, ya, yc, yd, proj, proj, proj, wa, wc, wd, wo, final_w)


CONV_HALO = 32


def _causal_taps(buf_ref, w_ref, r0, rows, lanes, n_taps, halo):
    assert halo >= SUBLANES * ((n_taps - 1) // SUBLANES + 1)
    acc = None
    for r in range(min(SUBLANES, n_taps)):
        part = None
        for a in range((n_taps - 1 - r) // SUBLANES + 1):
            s = SUBLANES * a + r
            start = r0 + (halo - SUBLANES * (a + 1))
            if not isinstance(start, int):
                start = pl.multiple_of(start, SUBLANES)
            term = (w_ref[n_taps - 1 - s:n_taps - s, lanes]
                    * buf_ref[pl.ds(start, rows + SUBLANES), lanes])
            part = term if part is None else part + term
        if r:
            part = pltpu.roll(part, r, axis=0)
        acc = part if acc is None else acc + part
    return acc[SUBLANES:, :]


def _inproj_body(x_ref, nw_ref, w_ref, wg_ref, wba_ref, o_ref, ba_ref, h_ref):
    j = pl.program_id(1)

    @pl.when(j == 0)
    def _():
        x = x_ref[...]
        ms = jnp.mean(x * x, axis=-1, keepdims=True)
        h = (x * lax.rsqrt(ms + RMS_EPS) * nw_ref[...]).astype(BF16)
        h_ref[...] = h
        ba_ref[...] = jnp.dot(h, wba_ref[...], preferred_element_type=F32)

    @pl.when(j < N_LEAD_TILES)
    def _():
        o_ref[...] = jnp.dot(h_ref[...], w_ref[...], preferred_element_type=F32).astype(BF16)

    @pl.when(j >= N_LEAD_TILES)
    def _():
        o_ref[...] = jnp.dot(h_ref[...], wg_ref[...], preferred_element_type=F32).astype(BF16)


def _inproj(x2d, norm_w, w_all, w_gate, w_ba, layer, *, tm=2048):
    T = x2d.shape[0]
    tn = D_MODEL
    const = lambda i, j: (0, 0)
    return pl.pallas_call(
        _inproj_body,
        grid=(T // tm, N_MAIN // tn),
        in_specs=[
            pl.BlockSpec((tm, D_MODEL), lambda i, j: (i, 0)),
            pl.BlockSpec((1, D_MODEL), const),
            pl.BlockSpec((None, D_MODEL, tn),
                         lambda i, j: (layer, 0, jnp.minimum(j, N_LEAD_TILES - 1))),
            pl.BlockSpec((None, D_MODEL, tn),
                         lambda i, j: (layer, 0, jnp.maximum(j - N_LEAD_TILES, 0))),
            pl.BlockSpec((None, D_MODEL, LANES), lambda i, j: (layer, 0, 0)),
        ],
        out_specs=[
            pl.BlockSpec((tm, tn), lambda i, j: (i, j)),
            pl.BlockSpec((tm, LANES), lambda i, j: (i, 0)),
        ],
        out_shape=[
            jax.ShapeDtypeStruct((T, N_MAIN), BF16),
            jax.ShapeDtypeStruct((T, LANES), F32),
        ],
        scratch_shapes=[pltpu.VMEM((tm, D_MODEL), BF16)],
        compiler_params=pltpu.CompilerParams(
            dimension_semantics=("parallel", "arbitrary"), vmem_limit_bytes=VMEM_LIMIT),
        name="inproj",
    )(x2d, norm_w, w_all, w_gate, w_ba)


CONV_ROWS = 128
CONV_STRIP = 128


def _conv_body(a_ref, g_ref, z_ref, w_ref, b_ref, lnw_ref, lnb_ref, o_ref, ubuf, cbuf, *, tl):
    @pl.when(pl.program_id(1) == 0)
    def _():
        ubuf[0:CONV_HALO, :] = jnp.zeros((CONV_HALO, CONV_WIDTH), F32)

    a = a_ref[...].astype(F32)
    g = g_ref[...].astype(F32)
    ubuf[CONV_HALO:CONV_HALO + tl, :] = a * (0.5 * jnp.tanh(0.5 * g) + 0.5)

    bias = b_ref[...]
    lnw = lnw_ref[...]
    lnb = lnb_ref[...]

    def step(r, carry):
        r0 = pl.multiple_of(r * CONV_ROWS, CONV_ROWS)
        for c in range(CONV_WIDTH // CONV_STRIP):
            lanes = slice(c * CONV_STRIP, (c + 1) * CONV_STRIP)
            cbuf[:, lanes] = _causal_taps(ubuf, w_ref, r0, CONV_ROWS, lanes, CONV_KERNEL, CONV_HALO)
        u = cbuf[...] + bias
        mu = jnp.mean(u, axis=-1, keepdims=True)
        uc = u - mu
        var = jnp.mean(uc * uc, axis=-1, keepdims=True)
        y = uc * lax.rsqrt(var + LN_EPS) * lnw + lnb
        zz = z_ref[pl.ds(r0, CONV_ROWS), :].astype(F32)
        o_ref[pl.ds(r0, CONV_ROWS), :] = (_silu(y) * _silu(zz)).astype(BF16)
        return carry

    lax.fori_loop(0, tl // CONV_ROWS, step, 0)
    ubuf[0:CONV_HALO, :] = ubuf[tl:tl + CONV_HALO, :]


def _conv_branch(proj3, dw_w, dw_b, ln_w, ln_b, *, tl=512):
    B, L, _ = proj3.shape
    vec = pl.BlockSpec((1, CONV_WIDTH), lambda b, t: (0, 0))
    return pl.pallas_call(
        functools.partial(_conv_body, tl=tl),
        grid=(B, L // tl),
        in_specs=[
            pl.BlockSpec((None, tl, CONV_WIDTH), lambda b, t: (b, t, COL_CA)),
            pl.BlockSpec((None, tl, CONV_WIDTH), lambda b, t: (b, t, COL_CG)),
            pl.BlockSpec((None, tl, CONV_WIDTH), lambda b, t: (b, t, COL_CZ)),
            pl.BlockSpec((CONV_KERNEL, CONV_WIDTH), lambda b, t: (0, 0)),
            vec, vec, vec,
        ],
        out_specs=pl.BlockSpec((None, tl, CONV_WIDTH), lambda b, t: (b, t, 0)),
        out_shape=jax.ShapeDtypeStruct((B, L, CONV_WIDTH), BF16),
        scratch_shapes=[pltpu.VMEM((CONV_HALO + tl, CONV_WIDTH), F32),
                        pltpu.VMEM((CONV_ROWS, CONV_WIDTH), F32)],
        compiler_params=pltpu.CompilerParams(
            dimension_semantics=("parallel", "arbitrary"), vmem_limit_bytes=VMEM_LIMIT),
        name="conv_branch",
    )(proj3, proj3, proj3, dw_w, dw_b, ln_w, ln_b)


def _nt_dot(a, b):
    return lax.dot_general(a, b, (((1,), (1,)), ((), ())), preferred_element_type=F32)


def _rope(t, cos, sin_signed):
    lane = lax.broadcasted_iota(jnp.int32, t.shape, 1)
    first_half = (lane % ATT_QK_DIM) < (ATT_QK_DIM // 2)
    half = ATT_QK_DIM // 2
    swapped = jnp.where(first_half, pltpu.roll(t, t.shape[1] - half, axis=1),
                        pltpu.roll(t, half, axis=1))
    return t * cos + swapped * sin_signed


ATT_HPS = 2
ATT_ZERO_SLOT = 2
ATT_ONES_ROWS = 16


def _attn_body(q_ref, k_ref, v_ref, z_ref, cq_ref, sq_ref, ck_ref, sk_ref, lam_ref, sw_ref, o_ref,
               kr_ref, vt_ref, m_ref, alpha_ref, acc_ref, s_ref, p_ref, *, lambda_init, tq):
    qi = pl.program_id(2)
    heads = range(ATT_HPS)
    hl = [slice(h * LANES, (h + 1) * LANES) for h in heads]
    cos_q = jnp.concatenate([cq_ref[...]] * ATT_HPS, axis=1)
    sin_q = jnp.concatenate([sq_ref[...]] * ATT_HPS, axis=1)

    @pl.when(qi == 0)
    def _():
        cos_k = jnp.concatenate([ck_ref[...]] * ATT_HPS, axis=1)
        sin_k = jnp.concatenate([sk_ref[...]] * ATT_HPS, axis=1)
        kr_ref[...] = _rope(k_ref[...].astype(F32), cos_k, sin_k).astype(BF16)

        ri = lax.broadcasted_iota(jnp.int32, (ATT_V_DIM, ATT_V_DIM), 0)
        ci = lax.broadcasted_iota(jnp.int32, (ATT_V_DIM, ATT_V_DIM), 1)
        ident = (ri == ci).astype(BF16)
        for jb in range(vt_ref.shape[1]):
            for h in heads:
                vt_ref[h, jb, 0:ATT_V_DIM, :] = _nt_dot(
                    ident, v_ref[jb * tq:(jb + 1) * tq, hl[h]]).astype(BF16)
                vt_ref[h, jb, ATT_V_DIM:, :] = jnp.ones((ATT_ONES_ROWS, tq), BF16)
        for h in heads:
            p_ref[h, ATT_ZERO_SLOT] = jnp.zeros(p_ref.shape[2:], BF16)

    q = _rope(q_ref[...].astype(F32), cos_q, sin_q) * (ATT_QK_DIM ** -0.5 * math.log2(math.e))
    lane = lax.broadcasted_iota(jnp.int32, q.shape, 1) % LANES
    q_maps = jnp.concatenate([jnp.where(lane < ATT_QK_DIM, q, 0.0),
                              jnp.where(lane >= ATT_QK_DIM, q, 0.0)], axis=0).astype(BF16)
    qs = [q_maps[:, hl[h]] for h in heads]

    def scores(h, j):
        k0 = pl.multiple_of(j * tq, tq)
        return _nt_dot(kr_ref[pl.ds(k0, tq), hl[h]], qs[h])

    def softmax_step(h, s):
        m_prev = m_ref[h]
        m_new = jnp.maximum(m_prev, jnp.max(s, axis=0, keepdims=True))
        alpha = jnp.exp2(m_prev - m_new)
        p = jnp.exp2(s - m_new)
        m_ref[h] = m_new
        return alpha, p.astype(BF16)

    def pv(h, j, p):
        return jnp.dot(vt_ref[h, j], p, preferred_element_type=F32)

    m_ref[...] = jnp.full(m_ref.shape, -jnp.inf, F32)
    acc_ref[...] = jnp.zeros(acc_ref.shape, F32)
    alpha_ref[...] = jnp.ones(alpha_ref.shape, F32)
    for h in heads:
        s_ref[h, 0] = scores(h, 0)

    def prev_slot(t):
        return jnp.where(t == 0, ATT_ZERO_SLOT, 1 - t % 2)

    def trip(t, carry):
        slot = t % 2
        pv_prev = [pv(h, jnp.maximum(t - 1, 0), p_ref[h, prev_slot(t)]) for h in heads]
        s_next = [scores(h, t + 1) for h in heads]
        for h in heads:
            alpha, p = softmax_step(h, s_ref[h, slot])
            p_ref[h, slot] = p
            acc_ref[h] = alpha_ref[h] * acc_ref[h] + pv_prev[h]
            alpha_ref[h] = alpha
        for h in heads:
            s_ref[h, 1 - slot] = s_next[h]
        return carry

    lax.fori_loop(0, qi, trip, 0)

    slot = qi % 2
    pv_prev = [pv(h, jnp.maximum(qi - 1, 0), p_ref[h, prev_slot(qi)]) for h in heads]
    key = lax.broadcasted_iota(jnp.int32, (tq, 2 * tq), 0)
    qry = lax.broadcasted_iota(jnp.int32, (tq, 2 * tq), 1) % tq
    last = [softmax_step(h, jnp.where(key <= qry, s_ref[h, slot], -jnp.inf)) for h in heads]
    pv_last = [pv(h, qi, last[h][1]) for h in heads]

    lq = lam_ref[...]
    lam = (jnp.exp(jnp.sum(lq[0:1, :] * lq[1:2, :], axis=-1, keepdims=True))
           - jnp.exp(jnp.sum(lq[2:3, :] * lq[3:4, :], axis=-1, keepdims=True)) + lambda_init)
    outs = []
    for h in heads:
        acc = last[h][0] * (alpha_ref[h] * acc_ref[h] + pv_prev[h]) + pv_last[h]
        o_all = acc[:ATT_V_DIM, :] / acc[ATT_V_DIM:ATT_V_DIM + 1, :]
        ot = o_all[:, :tq] - lam * o_all[:, tq:]
        ms = jnp.mean(ot * ot, axis=0, keepdims=True)
        outs.append((ot * lax.rsqrt(ms + RMS_EPS) * sw_ref[...] * (1.0 - lambda_init)).T)
    on = jnp.concatenate(outs, axis=1)
    o_ref[...] = (on * _silu(z_ref[...].astype(F32))).astype(BF16)


def _attention(proj3, cosf, sinf, lam_qk, subln_w, *, lambda_init, tq=256):
    B, L, _ = proj3.shape
    hw = ATT_HPS * LANES
    hpb = D_MODEL // hw
    qblk = lambda col: pl.BlockSpec((None, tq, hw), lambda b, h, i: (b, i, col * hpb + h))
    full = lambda col: pl.BlockSpec((None, L, hw), lambda b, h, i: (b, 0, col * hpb + h))
    return pl.pallas_call(
        functools.partial(_attn_body, lambda_init=lambda_init, tq=tq),
        grid=(B, ATT_HEADS // ATT_HPS, L // tq),
        in_specs=[
            qblk(COL_AQ), full(COL_AK), full(COL_AV), qblk(COL_AZ),
            pl.BlockSpec((None, tq, LANES), lambda b, h, i: (b, i, 0)),
            pl.BlockSpec((None, tq, LANES), lambda b, h, i: (b, i, 0)),
            pl.BlockSpec((None, L, LANES), lambda b, h, i: (b, 0, 0)),
            pl.BlockSpec((None, L, LANES), lambda b, h, i: (b, 0, 0)),
            pl.BlockSpec((4, ATT_QK_DIM), lambda b, h, i: (0, 0)),
            pl.BlockSpec((ATT_V_DIM, 1), lambda b, h, i: (0, 0)),
        ],
        out_specs=pl.BlockSpec((None, tq, hw), lambda b, h, i: (b, i, h)),
        out_shape=jax.ShapeDtypeStruct((B, L, ATT_HEADS * ATT_V_DIM), BF16),
        scratch_shapes=[
            pltpu.VMEM((L, hw), BF16),
            pltpu.VMEM((ATT_HPS, L // tq, ATT_V_DIM + ATT_ONES_ROWS, tq), BF16),
            pltpu.VMEM((ATT_HPS, 1, 2 * tq), F32),
            pltpu.VMEM((ATT_HPS, 1, 2 * tq), F32),
            pltpu.VMEM((ATT_HPS, ATT_V_DIM + ATT_ONES_ROWS, 2 * tq), F32),
            pltpu.VMEM((ATT_HPS, 2, tq, 2 * tq), F32),
            pltpu.VMEM((ATT_HPS, 3, tq, 2 * tq), BF16),
        ],
        compiler_params=pltpu.CompilerParams(
            dimension_semantics=("parallel", "parallel", "arbitrary"), vmem_limit_bytes=VMEM_LIMIT),
        name="diff_attention",
    )(proj3, proj3, proj3, proj3, cosf, sinf, cosf, sinf, lam_qk, subln_w)


DN_HALO = 8
DN_STRIP = 128
DN_CONV_ROWS = 128
DN_B_LANE = 0
DN_A_LANE = DN_HEADS
DN_ONE_LANE = DN_A_LANE + 3 * DN_HEADS
DN_GROUP = 4


def _split3(x):
    hi = x.astype(BF16).astype(F32)
    r1 = x - hi
    mid = r1.astype(BF16).astype(F32)
    lo = (r1 - mid).astype(BF16).astype(F32)
    return hi, mid, lo


def _dn_body(q_ref, k_ref, v_ref, z_ref, ba_ref, cw_ref, gp_ref, nw_ref, o_ref,
             xbuf, qkv, u_s, w_s, qg_s, kd_s, in_s, dec_s, s_ref, *, tb):
    C = DN_CHUNK
    W = DN_WIDTH
    hd = DN_HEAD_DIM

    @pl.when(pl.program_id(1) == 0)
    def _():
        xbuf[0:DN_HALO, :] = jnp.zeros((DN_HALO, 3 * W), F32)
        s_ref[...] = jnp.zeros(s_ref.shape, F32)

    xbuf[DN_HALO:DN_HALO + tb, 0:W] = q_ref[...].astype(F32)
    xbuf[DN_HALO:DN_HALO + tb, W:2 * W] = k_ref[...].astype(F32)
    xbuf[DN_HALO:DN_HALO + tb, 2 * W:3 * W] = v_ref[...].astype(F32)

    def conv_step(i, carry):
        r0 = pl.multiple_of(i * DN_CONV_ROWS, DN_CONV_ROWS)
        for c in range(3 * W // DN_STRIP):
            lanes = slice(c * DN_STRIP, (c + 1) * DN_STRIP)
            qkv[pl.ds(r0, DN_CONV_ROWS), lanes] = _silu(
                _causal_taps(xbuf, cw_ref, r0, DN_CONV_ROWS, lanes, DN_SHORT_CONV, DN_HALO))
        return carry

    lax.fori_loop(0, tb // DN_CONV_ROWS, conv_step, 0)
    xbuf[0:DN_HALO, :] = xbuf[tb:tb + DN_HALO, :]

    n_chunks = tb // C
    G = DN_GROUP
    n_groups = DN_HEADS // G
    gw = G * hd
    gc_w = G * C

    row = lax.broadcasted_iota(jnp.int32, (C, C), 0)
    col = lax.broadcasted_iota(jnp.int32, (C, C), 1)
    tri = (row >= col).astype(BF16)
    lane = lax.broadcasted_iota(jnp.int32, (C, LANES), 1)
    ri = lax.broadcasted_iota(jnp.int32, (C, gc_w), 0)
    cj = lax.broadcasted_iota(jnp.int32, (C, gc_w), 1) % C
    eye_cat = (ri == cj).astype(F32)
    bd_sq = (lax.broadcasted_iota(jnp.int32, (gc_w, gc_w), 0) // C
             == lax.broadcasted_iota(jnp.int32, (gc_w, gc_w), 1) // C)
    bd_wide = (lax.broadcasted_iota(jnp.int32, (gc_w, gw), 0) // C
               == lax.broadcasted_iota(jnp.int32, (gc_w, gw), 1) // hd)
    neg_a = -jnp.exp(gp_ref[0:1, :])
    dt_bias = gp_ref[1:2, :]
    norm_w = nw_ref[...]

    def block_diag(x, mask):
        return jnp.where(mask, jnp.tile(x, (G, 1)), 0.0).astype(BF16)

    def bdot(a, b):
        return jnp.dot(a, b, preferred_element_type=F32)

    probs = [(ci, g) for ci in range(n_chunks) for g in range(n_groups)]

    gd_lhs, gcb_all, kq_lhs, kn_grp, vb_grp, kbe_grp = {}, {}, {}, {}, {}, {}
    for ci in range(n_chunks):
        r0 = ci * C
        ba = ba_ref[r0:r0 + C, :]
        beta_all = jax.nn.sigmoid(ba)
        xa = ba + dt_bias
        softplus = jnp.maximum(xa, 0.0) + jnp.log1p(jnp.exp(-jnp.abs(xa)))
        g = jnp.where((lane >= DN_A_LANE) & (lane < DN_A_LANE + DN_HEADS), neg_a * softplus, 0.0)
        g_hi, g_mid, g_lo = _split3(g)
        gc = (bdot(tri, g_hi.astype(BF16)) + bdot(tri, g_mid.astype(BF16))
              + bdot(tri, g_lo.astype(BF16)))
        c_hi, c_mid, c_lo = _split3(gc)
        gd_lhs[ci] = (c_hi + pltpu.roll(c_mid, DN_HEADS, axis=1) + pltpu.roll(c_lo, 2 * DN_HEADS, axis=1)
                      + jnp.where((lane >= DN_ONE_LANE) & (lane < DN_ONE_LANE + 3), 1.0, 0.0)).astype(BF16)

        beta_b = jnp.concatenate(
            [jnp.broadcast_to(beta_all[:, DN_B_LANE + h:DN_B_LANE + h + 1], (C, hd))
             for h in range(DN_HEADS)], axis=1)
        gcb = jnp.concatenate(
            [jnp.broadcast_to(gc[:, DN_A_LANE + h:DN_A_LANE + h + 1], (C, hd))
             for h in range(DN_HEADS)], axis=1)
        gcb_all[ci] = gcb
        eg = jnp.exp(gcb)
        glast = gcb[C - 1:C, :]
        dec_s[ci] = jnp.exp(glast)

        q_all = qkv[r0:r0 + C, 0:W]
        k_all = qkv[r0:r0 + C, W:2 * W]
        v_all = qkv[r0:r0 + C, 2 * W:3 * W]
        qn = jnp.concatenate(
            [q_all[:, h * hd:(h + 1) * hd] * lax.rsqrt(jnp.sum(
                q_all[:, h * hd:(h + 1) * hd] ** 2, axis=-1, keepdims=True) + L2_EPS)
             for h in range(DN_HEADS)], axis=1) * (hd ** -0.5)
        kn = jnp.concatenate(
            [k_all[:, h * hd:(h + 1) * hd] * lax.rsqrt(jnp.sum(
                k_all[:, h * hd:(h + 1) * hd] ** 2, axis=-1, keepdims=True) + L2_EPS)
             for h in range(DN_HEADS)], axis=1)
        kb = kn * beta_b
        qg_s[r0:r0 + C, :] = (qn * eg).astype(BF16)
        kd_s[r0:r0 + C, :] = (kn * jnp.exp(glast - gcb)).astype(BF16)
        for gi in range(n_groups):
            gl = slice(gi * gw, (gi + 1) * gw)
            kq_lhs[ci, gi] = jnp.concatenate([kb[:, gl], qn[:, gl]], axis=0).astype(BF16)
            kn_grp[ci, gi] = kn[:, gl]
            vb_grp[ci, gi] = (v_all * beta_b)[:, gl]
            kbe_grp[ci, gi] = (kb * eg)[:, gl]

    gdiff, kq = {}, {}
    for ci, gi in probs:
        blocks = []
        for h in range(gi * G, (gi + 1) * G):
            p_hi, p_mid, p_lo = _split3(gcb_all[ci][:, h * hd:(h + 1) * hd])
            onehot = ((lane == DN_A_LANE + h) | (lane == DN_A_LANE + DN_HEADS + h)
                      | (lane == DN_A_LANE + 2 * DN_HEADS + h))
            blocks.append(jnp.where(lane == DN_ONE_LANE, -p_hi, jnp.where(
                lane == DN_ONE_LANE + 1, -p_mid, jnp.where(
                    lane == DN_ONE_LANE + 2, -p_lo, jnp.where(onehot, 1.0, 0.0)))))
        gd_rhs = jnp.concatenate(blocks, axis=0).astype(BF16)
        gdiff[ci, gi] = _nt_dot(gd_lhs[ci], gd_rhs)
        kq[ci, gi] = _nt_dot(kq_lhs[ci, gi], block_diag(kn_grp[ci, gi], bd_wide))

    m, p = {}, {}
    for pr in probs:
        decay = jnp.exp(jnp.where(ri >= cj, gdiff[pr], -jnp.inf))
        m[pr] = -jnp.where(ri > cj, kq[pr][:C, :] * decay, 0.0)
        ci, gi = pr
        in_s[ci * C:(ci + 1) * C, gi * gc_w:(gi + 1) * gc_w] = (kq[pr][C:, :] * decay).astype(BF16)
        p[pr] = eye_cat + m[pr]

    n_levels = int(math.log2(C)) - 1
    for pr in probs:
        m[pr] = bdot(m[pr].astype(BF16), block_diag(m[pr], bd_sq))
    for lvl in range(n_levels):
        last = lvl == n_levels - 1
        res = {}
        for pr in probs:
            lhs = p[pr] if last else jnp.concatenate([p[pr], m[pr]], axis=0)
            res[pr] = bdot(lhs.astype(BF16), block_diag(m[pr], bd_sq))
        for pr in probs:
            p[pr] = p[pr] + res[pr][:C, :]
            if not last:
                m[pr] = res[pr][C:, :]

    for pr in probs:
        ci, gi = pr
        pb = p[pr].astype(BF16)
        rows = slice(ci * C, (ci + 1) * C)
        gl = slice(gi * gw, (gi + 1) * gw)
        u_s[rows, gl] = bdot(pb, block_diag(vb_grp[pr], bd_wide))
        w_s[rows, gl] = bdot(pb, block_diag(kbe_grp[pr], bd_wide)).astype(BF16)

    def chunk_step(ci, carry):
        r0 = pl.multiple_of(ci * C, C)
        rows = pl.ds(r0, C)
        sb = [s_ref[h].astype(BF16) for h in range(DN_HEADS)]
        ws = [bdot(jnp.concatenate([w_s[rows, h * hd:(h + 1) * hd], qg_s[rows, h * hd:(h + 1) * hd]],
                                   axis=0), sb[h]) for h in range(DN_HEADS)]
        v_new = u_s[rows, :] - jnp.concatenate([x[:C, :] for x in ws], axis=1)
        vnb = v_new.astype(BF16)
        o_intra = [bdot(in_s[rows, gi * gc_w:(gi + 1) * gc_w],
                        block_diag(v_new[:, gi * gw:(gi + 1) * gw], bd_wide))
                   for gi in range(n_groups)]
        dec = dec_s[ci]
        for h in range(DN_HEADS):
            hl = slice(h * hd, (h + 1) * hd)
            s_ref[h] = s_ref[h] * dec[:, hl] + lax.dot_general(
                kd_s[rows, hl], vnb[:, hl], (((0,), (0,)), ((), ())), preferred_element_type=F32)
        o = jnp.concatenate([x[C:, :] for x in ws], axis=1) + jnp.concatenate(o_intra, axis=1)
        for h in range(DN_HEADS):
            hl = slice(h * hd, (h + 1) * hd)
            oh = o[:, hl]
            ms = jnp.mean(oh * oh, axis=-1, keepdims=True)
            on = oh * lax.rsqrt(ms + RMS_EPS) * norm_w
            o_ref[rows, hl] = (on * _silu(z_ref[rows, hl].astype(F32))).astype(BF16)
        return carry

    lax.fori_loop(0, n_chunks, chunk_step, 0)


def _deltanet(proj3, ba3, conv_w, gate_params, norm_w, *, tb=256):
    B, L, _ = proj3.shape
    blk = lambda col: pl.BlockSpec((None, tb, DN_WIDTH), lambda b, t: (b, t, col))
    return pl.pallas_call(
        functools.partial(_dn_body, tb=tb),
        grid=(B, L // tb),
        in_specs=[
            blk(COL_DQ), blk(COL_DK), blk(COL_DV), blk(COL_DZ),
            pl.BlockSpec((None, tb, LANES), lambda b, t: (b, t, 0)),
            pl.BlockSpec((DN_SHORT_CONV, 3 * DN_WIDTH), lambda b, t: (0, 0)),
            pl.BlockSpec((2, LANES), lambda b, t: (0, 0)),
            pl.BlockSpec((1, DN_HEAD_DIM), lambda b, t: (0, 0)),
        ],
        out_specs=pl.BlockSpec((None, tb, DN_WIDTH), lambda b, t: (b, t, 0)),
        out_shape=jax.ShapeDtypeStruct((B, L, DN_WIDTH), BF16),
        scratch_shapes=[
            pltpu.VMEM((DN_HALO + tb, 3 * DN_WIDTH), F32),
            pltpu.VMEM((tb, 3 * DN_WIDTH), F32),
            pltpu.VMEM((tb, DN_WIDTH), F32),
            pltpu.VMEM((tb, DN_WIDTH), BF16),
            pltpu.VMEM((tb, DN_WIDTH), BF16),
            pltpu.VMEM((tb, DN_WIDTH), BF16),
            pltpu.VMEM((tb, DN_HEADS * DN_CHUNK), BF16),
            pltpu.VMEM((tb // DN_CHUNK, 1, DN_WIDTH), F32),
            pltpu.VMEM((DN_HEADS, DN_HEAD_DIM, DN_HEAD_DIM), F32),
        ],
        compiler_params=pltpu.CompilerParams(
            dimension_semantics=("parallel", "arbitrary"), vmem_limit_bytes=VMEM_LIMIT),
        name="gated_deltanet",
    )(proj3, proj3, proj3, proj3, ba3, conv_w, gate_params, norm_w)


def _rope_tables(positions):
    half = ATT_QK_DIM // 2
    inv_freq = ROPE_THETA ** (-jnp.arange(0, ATT_QK_DIM, 2, dtype=F32) / ATT_QK_DIM)
    reps = LANES // half
    freq_lanes = jnp.tile(inv_freq, reps)
    sign_lanes = jnp.tile(jnp.repeat(jnp.array([-1.0, 1.0], F32), half), reps // 2)
    ang = positions.astype(F32)[..., None] * freq_lanes
    return jnp.cos(ang), jnp.sin(ang) * sign_lanes


def kernel(x, positions, norm_w, w_in, lam_qk, attn_subln_w, w_attn_out, conv_dw_w, conv_dw_b,
           conv_ln_w, conv_ln_b, w_conv_out, dn_conv_w, dn_a_log, dn_dt_bias, dn_norm_w,
           w_dn_out, w_out, final_norm_w):
    B, L, _ = x.shape
    T = B * L
    x2d = x.reshape(T, D_MODEL)
    depth = norm_w.shape[0]
    cosf, sinf = _rope_tables(positions)
    pad_heads = lambda v: jnp.pad(v, (DN_A_LANE, LANES - DN_A_LANE - DN_HEADS))
    w_all = w_in.astype(BF16)
    w_gate = w_all[:, :, ORIG_GATE:]
    w_ba = jnp.pad(w_all[:, :, ORIG_DB:ORIG_GATE], ((0, 0), (0, 0), (0, LANES - 2 * DN_HEADS)))
    wa, wc, wd, wo = (w.astype(BF16) for w in (w_attn_out, w_conv_out, w_dn_out, w_out))
    for l in range(depth):
        proj, ba = _inproj(x2d, norm_w[l][None, :], w_all, w_gate, w_ba, l)
        proj3 = proj.reshape(B, L, N_MAIN)
        lambda_init = 0.8 - 0.6 * math.exp(-0.3 * l)
        ya = _attention(proj3, cosf, sinf, lam_qk[l], attn_subln_w[l][:, None],
                        lambda_init=lambda_init).reshape(T, ATT_HEADS * ATT_V_DIM)
        yc = _conv_branch(proj3, conv_dw_w[l], conv_dw_b[l][None, :], conv_ln_w[l][None, :],
                          conv_ln_b[l][None, :]).reshape(T, CONV_WIDTH)
        gate_params = jnp.stack([pad_heads(dn_a_log[l]), pad_heads(dn_dt_bias[l])])
        yd = _deltanet(proj3, ba.reshape(B, L, LANES), dn_conv_w[l], gate_params,
                       dn_norm_w[l][None, :]).reshape(T, DN_WIDTH)
        x2d = _merge(x2d, ya, yc, yd, proj, wa, wc, wd, wo, final_norm_w[None, :], l,
                     final_norm=(l == depth - 1))
    return x2d.reshape(B, L, D_MODEL)
```

```python
import functools
import math

import jax
import jax.numpy as jnp
from jax import lax
from jax.experimental import pallas as pl
from jax.experimental.pallas import tpu as pltpu

F32 = jnp.float32
BF16 = jnp.bfloat16

D_MODEL = 1024
ATT_HEADS = 8
ATT_QK_DIM = 64
ATT_V_DIM = 128
ROPE_THETA = 10000.0
CONV_WIDTH = 1024
CONV_KERNEL = 31
DN_HEADS = 8
DN_HEAD_DIM = 128
DN_WIDTH = DN_HEADS * DN_HEAD_DIM
DN_SHORT_CONV = 4
DN_CHUNK = 64
RMS_EPS = 1e-6
LN_EPS = 1e-5
L2_EPS = 1e-6

LANES = 128
SUBLANES = 8
VMEM_LIMIT = 56 * 1024 * 1024
COL_AQ, COL_AK, COL_AV, COL_AZ = 0, 1, 2, 3
COL_CA, COL_CG, COL_CZ = 4, 5, 6
COL_DQ, COL_DK, COL_DV, COL_DZ = 7, 8, 9, 10
COL_GATE = 11
N_LEAD_TILES = 11
N_MAIN = 14 * D_MODEL
ORIG_DB = N_LEAD_TILES * D_MODEL
ORIG_GATE = ORIG_DB + 2 * DN_HEADS


def _silu(x):
    return x * jax.nn.sigmoid(x)


def _merge_body(x_ref, a_ref, c_ref, d_ref, ga_ref, gc_ref, gd_ref,
                wa_ref, wc_ref, wd_ref, wo_ref, fw_ref, o_ref, *, final_norm):
    ya = jnp.dot(a_ref[...], wa_ref[...], preferred_element_type=F32)
    yc = jnp.dot(c_ref[...], wc_ref[...], preferred_element_type=F32)
    yd = jnp.dot(d_ref[...], wd_ref[...], preferred_element_type=F32)
    merged = (jax.nn.sigmoid(ga_ref[...].astype(F32)) * ya
              + jax.nn.sigmoid(gc_ref[...].astype(F32)) * yc
              + jax.nn.sigmoid(gd_ref[...].astype(F32)) * yd)
    y = x_ref[...] + jnp.dot(merged.astype(BF16), wo_ref[...], preferred_element_type=F32)
    if final_norm:
        ms = jnp.mean(y * y, axis=-1, keepdims=True)
        y = y * lax.rsqrt(ms + RMS_EPS) * fw_ref[...]
    o_ref[...] = y


def _merge(x2d, ya, yc, yd, proj, wa, wc, wd, wo, final_w, layer, *, final_norm, tm=512):
    T = x2d.shape[0]
    row = lambda i: (i, 0)
    const = lambda i: (0, 0)
    wspec = pl.BlockSpec((None, D_MODEL, D_MODEL), lambda i: (layer, 0, 0))
    return pl.pallas_call(
        functools.partial(_merge_body, final_norm=final_norm),
        grid=(T // tm,),
        in_specs=[
            pl.BlockSpec((tm, D_MODEL), row),
            pl.BlockSpec((tm, D_MODEL), row),
            pl.BlockSpec((tm, D_MODEL), row),
            pl.BlockSpec((tm, D_MODEL), row),
            pl.BlockSpec((tm, D_MODEL), lambda i: (i, COL_GATE)),
            pl.BlockSpec((tm, D_MODEL), lambda i: (i, COL_GATE + 1)),
            pl.BlockSpec((tm, D_MODEL), lambda i: (i, COL_GATE + 2)),
            wspec, wspec, wspec, wspec,
            pl.BlockSpec((1, D_MODEL), const),
        ],
        out_specs=pl.BlockSpec((tm, D_MODEL), row),
        out_shape=jax.ShapeDtypeStruct((T, D_MODEL), F32),
        compiler_params=pltpu.CompilerParams(
            dimension_semantics=("parallel",), vmem_limit_bytes=VMEM_LIMIT),
        name="merge",
    )(x2d, ya, yc, yd, proj, proj, proj, wa, wc, wd, wo, final_w)


CONV_HALO = 32


def _causal_taps(buf_ref, w_ref, r0, rows, lanes, n_taps, halo):
    assert halo >= SUBLANES * ((n_taps - 1) // SUBLANES + 1)
    acc = None
    for r in range(min(SUBLANES, n_taps)):
        part = None
        for a in range((n_taps - 1 - r) // SUBLANES + 1):
            s = SUBLANES * a + r
            start = r0 + (halo - SUBLANES * (a + 1))
            if not isinstance(start, int):
                start = pl.multiple_of(start, SUBLANES)
            term = (w_ref[n_taps - 1 - s:n_taps - s, lanes]
                    * buf_ref[pl.ds(start, rows + SUBLANES), lanes])
            part = term if part is None else part + term
        if r:
            part = pltpu.roll(part, r, axis=0)
        acc = part if acc is None else acc + part
    return acc[SUBLANES:, :]


def _inproj_body(x_ref, nw_ref, w_ref, wg_ref, wba_ref, o_ref, ba_ref, h_ref):
    j = pl.program_id(1)

    @pl.when(j == 0)
    def _():
        x = x_ref[...]
        ms = jnp.mean(x * x, axis=-1, keepdims=True)
        h = (x * lax.rsqrt(ms + RMS_EPS) * nw_ref[...]).astype(BF16)
        h_ref[...] = h
        ba_ref[...] = jnp.dot(h, wba_ref[...], preferred_element_type=F32)

    @pl.when(j < N_LEAD_TILES)
    def _():
        o_ref[...] = jnp.dot(h_ref[...], w_ref[...], preferred_element_type=F32).astype(BF16)

    @pl.when(j >= N_LEAD_TILES)
    def _():
        o_ref[...] = jnp.dot(h_ref[...], wg_ref[...], preferred_element_type=F32).astype(BF16)


def _inproj(x2d, norm_w, w_all, w_gate, w_ba, layer, *, tm=2048):
    T = x2d.shape[0]
    tn = D_MODEL
    const = lambda i, j: (0, 0)
    return pl.pallas_call(
        _inproj_body,
        grid=(T // tm, N_MAIN // tn),
        in_specs=[
            pl.BlockSpec((tm, D_MODEL), lambda i, j: (i, 0)),
            pl.BlockSpec((1, D_MODEL), const),
            pl.BlockSpec((None, D_MODEL, tn),
                         lambda i, j: (layer, 0, jnp.minimum(j, N_LEAD_TILES - 1))),
            pl.BlockSpec((None, D_MODEL, tn),
                         lambda i, j: (layer, 0, jnp.maximum(j - N_LEAD_TILES, 0))),
            pl.BlockSpec((None, D_MODEL, LANES), lambda i, j: (layer, 0, 0)),
        ],
        out_specs=[
            pl.BlockSpec((tm, tn), lambda i, j: (i, j)),
            pl.BlockSpec((tm, LANES), lambda i, j: (i, 0)),
        ],
        out_shape=[
            jax.ShapeDtypeStruct((T, N_MAIN), BF16),
            jax.ShapeDtypeStruct((T, LANES), F32),
        ],
        scratch_shapes=[pltpu.VMEM((tm, D_MODEL), BF16)],
        compiler_params=pltpu.CompilerParams(
            dimension_semantics=("parallel", "arbitrary"), vmem_limit_bytes=VMEM_LIMIT),
        name="inproj",
    )(x2d, norm_w, w_all, w_gate, w_ba)


CONV_ROWS = 128
CONV_STRIP = 128


def _conv_body(a_ref, g_ref, z_ref, w_ref, b_ref, lnw_ref, lnb_ref, o_ref, ubuf, cbuf, *, tl):
    @pl.when(pl.program_id(1) == 0)
    def _():
        ubuf[0:CONV_HALO, :] = jnp.zeros((CONV_HALO, CONV_WIDTH), F32)

    a = a_ref[...].astype(F32)
    g = g_ref[...].astype(F32)
    ubuf[CONV_HALO:CONV_HALO + tl, :] = a * (0.5 * jnp.tanh(0.5 * g) + 0.5)

    bias = b_ref[...]
    lnw = lnw_ref[...]
    lnb = lnb_ref[...]

    def step(r, carry):
        r0 = pl.multiple_of(r * CONV_ROWS, CONV_ROWS)
        for c in range(CONV_WIDTH // CONV_STRIP):
            lanes = slice(c * CONV_STRIP, (c + 1) * CONV_STRIP)
            cbuf[:, lanes] = _causal_taps(ubuf, w_ref, r0, CONV_ROWS, lanes, CONV_KERNEL, CONV_HALO)
        u = cbuf[...] + bias
        mu = jnp.mean(u, axis=-1, keepdims=True)
        uc = u - mu
        var = jnp.mean(uc * uc, axis=-1, keepdims=True)
        y = uc * lax.rsqrt(var + LN_EPS) * lnw + lnb
        zz = z_ref[pl.ds(r0, CONV_ROWS), :].astype(F32)
        o_ref[pl.ds(r0, CONV_ROWS), :] = (_silu(y) * _silu(zz)).astype(BF16)
        return carry

    lax.fori_loop(0, tl // CONV_ROWS, step, 0)
    ubuf[0:CONV_HALO, :] = ubuf[tl:tl + CONV_HALO, :]


def _conv_branch(proj3, dw_w, dw_b, ln_w, ln_b, *, tl=512):
    B, L, _ = proj3.shape
    vec = pl.BlockSpec((1, CONV_WIDTH), lambda b, t: (0, 0))
    return pl.pallas_call(
        functools.partial(_conv_body, tl=tl),
        grid=(B, L // tl),
        in_specs=[
            pl.BlockSpec((None, tl, CONV_WIDTH), lambda b, t: (b, t, COL_CA)),
            pl.BlockSpec((None, tl, CONV_WIDTH), lambda b, t: (b, t, COL_CG)),
            pl.BlockSpec((None, tl, CONV_WIDTH), lambda b, t: (b, t, COL_CZ)),
            pl.BlockSpec((CONV_KERNEL, CONV_WIDTH), lambda b, t: (0, 0)),
            vec, vec, vec,
        ],
        out_specs=pl.BlockSpec((None, tl, CONV_WIDTH), lambda b, t: (b, t, 0)),
        out_shape=jax.ShapeDtypeStruct((B, L, CONV_WIDTH), BF16),
        scratch_shapes=[pltpu.VMEM((CONV_HALO + tl, CONV_WIDTH), F32),
                        pltpu.VMEM((CONV_ROWS, CONV_WIDTH), F32)],
        compiler_params=pltpu.CompilerParams(
            dimension_semantics=("parallel", "arbitrary"), vmem_limit_bytes=VMEM_LIMIT),
        name="conv_branch",
    )(proj3, proj3, proj3, dw_w, dw_b, ln_w, ln_b)


def _nt_dot(a, b):
    return lax.dot_general(a, b, (((1,), (1,)), ((), ())), preferred_element_type=F32)


def _rope(t, cos, sin_signed):
    lane = lax.broadcasted_iota(jnp.int32, t.shape, 1)
    first_half = (lane % ATT_QK_DIM) < (ATT_QK_DIM // 2)
    half = ATT_QK_DIM // 2
    swapped = jnp.where(first_half, pltpu.roll(t, t.shape[1] - half, axis=1),
                        pltpu.roll(t, half, axis=1))
    return t * cos + swapped * sin_signed


ATT_HPS = 2
ATT_ZERO_SLOT = 2
ATT_ONES_ROWS = 16


def _attn_body(q_ref, k_ref, v_ref, z_ref, cos_ref, sin_ref, lam_ref, sw_ref, o_ref,
               kr_ref, vt_ref, m_ref, alpha_ref, acc_ref, s_ref, p_ref, *, lambda_init, tq):
    heads = range(ATT_HPS)
    hl = [slice(h * LANES, (h + 1) * LANES) for h in heads]
    n_blocks = vt_ref.shape[1]

    cos_k = jnp.concatenate([cos_ref[...]] * ATT_HPS, axis=1)
    sin_k = jnp.concatenate([sin_ref[...]] * ATT_HPS, axis=1)
    kr_ref[...] = _rope(k_ref[...].astype(F32), cos_k, sin_k).astype(BF16)

    ri = lax.broadcasted_iota(jnp.int32, (ATT_V_DIM, ATT_V_DIM), 0)
    ci = lax.broadcasted_iota(jnp.int32, (ATT_V_DIM, ATT_V_DIM), 1)
    ident = (ri == ci).astype(BF16)
    for jb in range(n_blocks):
        for h in heads:
            vt_ref[h, jb, 0:ATT_V_DIM, :] = _nt_dot(
                ident, v_ref[jb * tq:(jb + 1) * tq, hl[h]]).astype(BF16)
            vt_ref[h, jb, ATT_V_DIM:, :] = jnp.ones((ATT_ONES_ROWS, tq), BF16)
    for h in heads:
        p_ref[h, ATT_ZERO_SLOT] = jnp.zeros(p_ref.shape[2:], BF16)

    lq = lam_ref[...]
    lam = (jnp.exp(jnp.sum(lq[0:1, :] * lq[1:2, :], axis=-1, keepdims=True))
           - jnp.exp(jnp.sum(lq[2:3, :] * lq[3:4, :], axis=-1, keepdims=True)) + lambda_init)

    def q_block(qi, carry):
        _attn_q_block(qi, lam, q_ref, z_ref, cos_ref, sin_ref, sw_ref, o_ref, kr_ref, vt_ref,
                      m_ref, alpha_ref, acc_ref, s_ref, p_ref, lambda_init=lambda_init, tq=tq)
        return carry

    lax.fori_loop(0, n_blocks, q_block, 0)


def _attn_q_block(qi, lam, q_ref, z_ref, cos_ref, sin_ref, sw_ref, o_ref, kr_ref, vt_ref,
                  m_ref, alpha_ref, acc_ref, s_ref, p_ref, *, lambda_init, tq):
    heads = range(ATT_HPS)
    hl = [slice(h * LANES, (h + 1) * LANES) for h in heads]
    rows = pl.ds(pl.multiple_of(qi * tq, tq), tq)
    cos_q = jnp.concatenate([cos_ref[rows, :]] * ATT_HPS, axis=1)
    sin_q = jnp.concatenate([sin_ref[rows, :]] * ATT_HPS, axis=1)

    q = _rope(q_ref[rows, :].astype(F32), cos_q, sin_q) * (ATT_QK_DIM ** -0.5 * math.log2(math.e))
    lane = lax.broadcasted_iota(jnp.int32, q.shape, 1) % LANES
    q_maps = jnp.concatenate([jnp.where(lane < ATT_QK_DIM, q, 0.0),
                              jnp.where(lane >= ATT_QK_DIM, q, 0.0)], axis=0).astype(BF16)
    qs = [q_maps[:, hl[h]] for h in heads]

    def scores(h, j):
        k0 = pl.multiple_of(j * tq, tq)
        return _nt_dot(kr_ref[pl.ds(k0, tq), hl[h]], qs[h])

    def softmax_step(h, s):
        m_prev = m_ref[h]
        m_new = jnp.maximum(m_prev, jnp.max(s, axis=0, keepdims=True))
        alpha = jnp.exp2(m_prev - m_new)
        p = jnp.exp2(s - m_new)
        m_ref[h] = m_new
        return alpha, p.astype(BF16)

    def pv(h, j, p):
        return jnp.dot(vt_ref[h, j], p, preferred_element_type=F32)

    m_ref[...] = jnp.full(m_ref.shape, -jnp.inf, F32)
    acc_ref[...] = jnp.zeros(acc_ref.shape, F32)
    alpha_ref[...] = jnp.ones(alpha_ref.shape, F32)
    for h in heads:
        s_ref[h, 0] = scores(h, 0)

    def prev_slot(t):
        return jnp.where(t == 0, ATT_ZERO_SLOT, 1 - t % 2)

    def trip(t, carry):
        slot = t % 2
        pv_prev = [pv(h, jnp.maximum(t - 1, 0), p_ref[h, prev_slot(t)]) for h in heads]
        s_next = [scores(h, t + 1) for h in heads]
        for h in heads:
            alpha, p = softmax_step(h, s_ref[h, slot])
            p_ref[h, slot] = p
            acc_ref[h] = alpha_ref[h] * acc_ref[h] + pv_prev[h]
            alpha_ref[h] = alpha
        for h in heads:
            s_ref[h, 1 - slot] = s_next[h]
        return carry

    lax.fori_loop(0, qi, trip, 0)

    slot = qi % 2
    pv_prev = [pv(h, jnp.maximum(qi - 1, 0), p_ref[h, prev_slot(qi)]) for h in heads]
    key = lax.broadcasted_iota(jnp.int32, (tq, 2 * tq), 0)
    qry = lax.broadcasted_iota(jnp.int32, (tq, 2 * tq), 1) % tq
    last = [softmax_step(h, jnp.where(key <= qry, s_ref[h, slot], -jnp.inf)) for h in heads]
    pv_last = [pv(h, qi, last[h][1]) for h in heads]

    outs = []
    for h in heads:
        acc = last[h][0] * (alpha_ref[h] * acc_ref[h] + pv_prev[h]) + pv_last[h]
        o_all = acc[:ATT_V_DIM, :] / acc[ATT_V_DIM:ATT_V_DIM + 1, :]
        ot = o_all[:, :tq] - lam * o_all[:, tq:]
        ms = jnp.mean(ot * ot, axis=0, keepdims=True)
        outs.append((ot * lax.rsqrt(ms + RMS_EPS) * sw_ref[...] * (1.0 - lambda_init)).T)
    on = jnp.concatenate(outs, axis=1)
    o_ref[rows, :] = (on * _silu(z_ref[rows, :].astype(F32))).astype(BF16)


def _attention(proj3, cosf, sinf, lam_qk, subln_w, *, lambda_init, tq=256):
    B, L, _ = proj3.shape
    hw = ATT_HPS * LANES
    hpb = D_MODEL // hw
    full = lambda col: pl.BlockSpec((None, L, hw), lambda b, h: (b, 0, col * hpb + h))
    table = pl.BlockSpec((None, L, LANES), lambda b, h: (b, 0, 0))
    return pl.pallas_call(
        functools.partial(_attn_body, lambda_init=lambda_init, tq=tq),
        grid=(B, ATT_HEADS // ATT_HPS),
        in_specs=[
            full(COL_AQ), full(COL_AK), full(COL_AV), full(COL_AZ), table, table,
            pl.BlockSpec((4, ATT_QK_DIM), lambda b, h: (0, 0)),
            pl.BlockSpec((ATT_V_DIM, 1), lambda b, h: (0, 0)),
        ],
        out_specs=pl.BlockSpec((None, L, hw), lambda b, h: (b, 0, h)),
        out_shape=jax.ShapeDtypeStruct((B, L, ATT_HEADS * ATT_V_DIM), BF16),
        scratch_shapes=[
            pltpu.VMEM((L, hw), BF16),
            pltpu.VMEM((ATT_HPS, L // tq, ATT_V_DIM + ATT_ONES_ROWS, tq), BF16),
            pltpu.VMEM((ATT_HPS, 1, 2 * tq), F32),
            pltpu.VMEM((ATT_HPS, 1, 2 * tq), F32),
            pltpu.VMEM((ATT_HPS, ATT_V_DIM + ATT_ONES_ROWS, 2 * tq), F32),
            pltpu.VMEM((ATT_HPS, 2, tq, 2 * tq), F32),
            pltpu.VMEM((ATT_HPS, 3, tq, 2 * tq), BF16),
        ],
        compiler_params=pltpu.CompilerParams(
            dimension_semantics=("parallel", "parallel"), vmem_limit_bytes=VMEM_LIMIT),
        name="diff_attention",
    )(proj3, proj3, proj3, proj3, cosf, sinf, lam_qk, subln_w)


DN_HALO = 8
DN_STRIP = 128
DN_CONV_ROWS = 128
DN_B_LANE = 0
DN_A_LANE = DN_HEADS
DN_ONE_LANE = DN_A_LANE + 3 * DN_HEADS
DN_GROUP = 4


def _split3(x):
    hi = x.astype(BF16).astype(F32)
    r1 = x - hi
    mid = r1.astype(BF16).astype(F32)
    lo = (r1 - mid).astype(BF16).astype(F32)
    return hi, mid, lo


def _dn_body(q_ref, k_ref, v_ref, z_ref, ba_ref, cw_ref, gp_ref, nw_ref, o_ref,
             xbuf, qkv, u_s, w_s, qg_s, kd_s, in_s, dec_s, s_ref, *, tb):
    C = DN_CHUNK
    W = DN_WIDTH
    hd = DN_HEAD_DIM

    @pl.when(pl.program_id(1) == 0)
    def _():
        xbuf[0:DN_HALO, :] = jnp.zeros((DN_HALO, 3 * W), F32)
        s_ref[...] = jnp.zeros(s_ref.shape, F32)

    xbuf[DN_HALO:DN_HALO + tb, 0:W] = q_ref[...].astype(F32)
    xbuf[DN_HALO:DN_HALO + tb, W:2 * W] = k_ref[...].astype(F32)
    xbuf[DN_HALO:DN_HALO + tb, 2 * W:3 * W] = v_ref[...].astype(F32)

    def conv_step(i, carry):
        r0 = pl.multiple_of(i * DN_CONV_ROWS, DN_CONV_ROWS)
        for c in range(3 * W // DN_STRIP):
            lanes = slice(c * DN_STRIP, (c + 1) * DN_STRIP)
            qkv[pl.ds(r0, DN_CONV_ROWS), lanes] = _silu(
                _causal_taps(xbuf, cw_ref, r0, DN_CONV_ROWS, lanes, DN_SHORT_CONV, DN_HALO))
        return carry

    lax.fori_loop(0, tb // DN_CONV_ROWS, conv_step, 0)
    xbuf[0:DN_HALO, :] = xbuf[tb:tb + DN_HALO, :]

    n_chunks = tb // C
    G = DN_GROUP
    n_groups = DN_HEADS // G
    gw = G * hd
    gc_w = G * C

    row = lax.broadcasted_iota(jnp.int32, (C, C), 0)
    col = lax.broadcasted_iota(jnp.int32, (C, C), 1)
    tri = (row >= col).astype(BF16)
    lane = lax.broadcasted_iota(jnp.int32, (C, LANES), 1)
    ri = lax.broadcasted_iota(jnp.int32, (C, gc_w), 0)
    cj = lax.broadcasted_iota(jnp.int32, (C, gc_w), 1) % C
    eye_cat = (ri == cj).astype(F32)
    bd_sq = (lax.broadcasted_iota(jnp.int32, (gc_w, gc_w), 0) // C
             == lax.broadcasted_iota(jnp.int32, (gc_w, gc_w), 1) // C)
    bd_wide = (lax.broadcasted_iota(jnp.int32, (gc_w, gw), 0) // C
               == lax.broadcasted_iota(jnp.int32, (gc_w, gw), 1) // hd)
    neg_a = -jnp.exp(gp_ref[0:1, :])
    dt_bias = gp_ref[1:2, :]
    norm_w = nw_ref[...]

    def block_diag(x, mask):
        return jnp.where(mask, jnp.tile(x, (G, 1)), 0.0).astype(BF16)

    def bdot(a, b):
        return jnp.dot(a, b, preferred_element_type=F32)

    probs = [(ci, g) for ci in range(n_chunks) for g in range(n_groups)]

    gd_lhs, gcb_all, kq_lhs, kn_grp, vb_grp, kbe_grp = {}, {}, {}, {}, {}, {}
    for ci in range(n_chunks):
        r0 = ci * C
        ba = ba_ref[r0:r0 + C, :]
        beta_all = jax.nn.sigmoid(ba)
        xa = ba + dt_bias
        softplus = jnp.maximum(xa, 0.0) + jnp.log1p(jnp.exp(-jnp.abs(xa)))
        g = jnp.where((lane >= DN_A_LANE) & (lane < DN_A_LANE + DN_HEADS), neg_a * softplus, 0.0)
        g_hi, g_mid, g_lo = _split3(g)
        gc = (bdot(tri, g_hi.astype(BF16)) + bdot(tri, g_mid.astype(BF16))
              + bdot(tri, g_lo.astype(BF16)))
        c_hi, c_mid, c_lo = _split3(gc)
        gd_lhs[ci] = (c_hi + pltpu.roll(c_mid, DN_HEADS, axis=1) + pltpu.roll(c_lo, 2 * DN_HEADS, axis=1)
                      + jnp.where((lane >= DN_ONE_LANE) & (lane < DN_ONE_LANE + 3), 1.0, 0.0)).astype(BF16)

        beta_b = jnp.concatenate(
            [jnp.broadcast_to(beta_all[:, DN_B_LANE + h:DN_B_LANE + h + 1], (C, hd))
             for h in range(DN_HEADS)], axis=1)
        gcb = jnp.concatenate(
            [jnp.broadcast_to(gc[:, DN_A_LANE + h:DN_A_LANE + h + 1], (C, hd))
             for h in range(DN_HEADS)], axis=1)
        gcb_all[ci] = gcb
        eg = jnp.exp(gcb)
        glast = gcb[C - 1:C, :]
        dec_s[ci] = jnp.exp(glast)

        q_all = qkv[r0:r0 + C, 0:W]
        k_all = qkv[r0:r0 + C, W:2 * W]
        v_all = qkv[r0:r0 + C, 2 * W:3 * W]
        qn = jnp.concatenate(
            [q_all[:, h * hd:(h + 1) * hd] * lax.rsqrt(jnp.sum(
                q_all[:, h * hd:(h + 1) * hd] ** 2, axis=-1, keepdims=True) + L2_EPS)
             for h in range(DN_HEADS)], axis=1) * (hd ** -0.5)
        kn = jnp.concatenate(
            [k_all[:, h * hd:(h + 1) * hd] * lax.rsqrt(jnp.sum(
                k_all[:, h * hd:(h + 1) * hd] ** 2, axis=-1, keepdims=True) + L2_EPS)
             for h in range(DN_HEADS)], axis=1)
        kb = kn * beta_b
        qg_s[r0:r0 + C, :] = (qn * eg).astype(BF16)
        kd_s[r0:r0 + C, :] = (kn * jnp.exp(glast - gcb)).astype(BF16)
        for gi in range(n_groups):
            gl = slice(gi * gw, (gi + 1) * gw)
            kq_lhs[ci, gi] = jnp.concatenate([kb[:, gl], qn[:, gl]], axis=0).astype(BF16)
            kn_grp[ci, gi] = kn[:, gl]
            vb_grp[ci, gi] = (v_all * beta_b)[:, gl]
            kbe_grp[ci, gi] = (kb * eg)[:, gl]

    gdiff, kq = {}, {}
    for ci, gi in probs:
        blocks = []
        for h in range(gi * G, (gi + 1) * G):
            p_hi, p_mid, p_lo = _split3(gcb_all[ci][:, h * hd:(h + 1) * hd])
            onehot = ((lane == DN_A_LANE + h) | (lane == DN_A_LANE + DN_HEADS + h)
                      | (lane == DN_A_LANE + 2 * DN_HEADS + h))
            blocks.append(jnp.where(lane == DN_ONE_LANE, -p_hi, jnp.where(
                lane == DN_ONE_LANE + 1, -p_mid, jnp.where(
                    lane == DN_ONE_LANE + 2, -p_lo, jnp.where(onehot, 1.0, 0.0)))))
        gd_rhs = jnp.concatenate(blocks, axis=0).astype(BF16)
        gdiff[ci, gi] = _nt_dot(gd_lhs[ci], gd_rhs)
        kq[ci, gi] = _nt_dot(kq_lhs[ci, gi], block_diag(kn_grp[ci, gi], bd_wide))

    m, p = {}, {}
    for pr in probs:
        decay = jnp.exp(jnp.where(ri >= cj, gdiff[pr], -jnp.inf))
        m[pr] = -jnp.where(ri > cj, kq[pr][:C, :] * decay, 0.0)
        ci, gi = pr
        in_s[ci * C:(ci + 1) * C, gi * gc_w:(gi + 1) * gc_w] = (kq[pr][C:, :] * decay).astype(BF16)
        p[pr] = eye_cat + m[pr]

    n_levels = int(math.log2(C)) - 1
    for pr in probs:
        m[pr] = bdot(m[pr].astype(BF16), block_diag(m[pr], bd_sq))
    for lvl in range(n_levels):
        last = lvl == n_levels - 1
        res = {}
        for pr in probs:
            lhs = p[pr] if last else jnp.concatenate([p[pr], m[pr]], axis=0)
            res[pr] = bdot(lhs.astype(BF16), block_diag(m[pr], bd_sq))
        for pr in probs:
            p[pr] = p[pr] + res[pr][:C, :]
            if not last:
                m[pr] = res[pr][C:, :]

    for pr in probs:
        ci, gi = pr
        pb = p[pr].astype(BF16)
        rows = slice(ci * C, (ci + 1) * C)
        gl = slice(gi * gw, (gi + 1) * gw)
        u_s[rows, gl] = bdot(pb, block_diag(vb_grp[pr], bd_wide))
        w_s[rows, gl] = bdot(pb, block_diag(kbe_grp[pr], bd_wide)).astype(BF16)

    def chunk_step(ci, carry):
        r0 = pl.multiple_of(ci * C, C)
        rows = pl.ds(r0, C)
        sb = [s_ref[h].astype(BF16) for h in range(DN_HEADS)]
        ws = [bdot(jnp.concatenate([w_s[rows, h * hd:(h + 1) * hd], qg_s[rows, h * hd:(h + 1) * hd]],
                                   axis=0), sb[h]) for h in range(DN_HEADS)]
        v_new = u_s[rows, :] - jnp.concatenate([x[:C, :] for x in ws], axis=1)
        vnb = v_new.astype(BF16)
        o_intra = [bdot(in_s[rows, gi * gc_w:(gi + 1) * gc_w],
                        block_diag(v_new[:, gi * gw:(gi + 1) * gw], bd_wide))
                   for gi in range(n_groups)]
        dec = dec_s[ci]
        for h in range(DN_HEADS):
            hl = slice(h * hd, (h + 1) * hd)
            s_ref[h] = s_ref[h] * dec[:, hl] + lax.dot_general(
                kd_s[rows, hl], vnb[:, hl], (((0,), (0,)), ((), ())), preferred_element_type=F32)
        o = jnp.concatenate([x[C:, :] for x in ws], axis=1) + jnp.concatenate(o_intra, axis=1)
        for h in range(DN_HEADS):
            hl = slice(h * hd, (h + 1) * hd)
            oh = o[:, hl]
            ms = jnp.mean(oh * oh, axis=-1, keepdims=True)
            on = oh * lax.rsqrt(ms + RMS_EPS) * norm_w
            o_ref[rows, hl] = (on * _silu(z_ref[rows, hl].astype(F32))).astype(BF16)
        return carry

    lax.fori_loop(0, n_chunks, chunk_step, 0)


def _deltanet(proj3, ba3, conv_w, gate_params, norm_w, *, tb=256):
    B, L, _ = proj3.shape
    blk = lambda col: pl.BlockSpec((None, tb, DN_WIDTH), lambda b, t: (b, t, col))
    return pl.pallas_call(
        functools.partial(_dn_body, tb=tb),
        grid=(B, L // tb),
        in_specs=[
            blk(COL_DQ), blk(COL_DK), blk(COL_DV), blk(COL_DZ),
            pl.BlockSpec((None, tb, LANES), lambda b, t: (b, t, 0)),
            pl.BlockSpec((DN_SHORT_CONV, 3 * DN_WIDTH), lambda b, t: (0, 0)),
            pl.BlockSpec((2, LANES), lambda b, t: (0, 0)),
            pl.BlockSpec((1, DN_HEAD_DIM), lambda b, t: (0, 0)),
        ],
        out_specs=pl.BlockSpec((None, tb, DN_WIDTH), lambda b, t: (b, t, 0)),
        out_shape=jax.ShapeDtypeStruct((B, L, DN_WIDTH), BF16),
        scratch_shapes=[
            pltpu.VMEM((DN_HALO + tb, 3 * DN_WIDTH), F32),
            pltpu.VMEM((tb, 3 * DN_WIDTH), F32),
            pltpu.VMEM((tb, DN_WIDTH), F32),
            pltpu.VMEM((tb, DN_WIDTH), BF16),
            pltpu.VMEM((tb, DN_WIDTH), BF16),
            pltpu.VMEM((tb, DN_WIDTH), BF16),
            pltpu.VMEM((tb, DN_HEADS * DN_CHUNK), BF16),
            pltpu.VMEM((tb // DN_CHUNK, 1, DN_WIDTH), F32),
            pltpu.VMEM((DN_HEADS, DN_HEAD_DIM, DN_HEAD_DIM), F32),
        ],
        compiler_params=pltpu.CompilerParams(
            dimension_semantics=("parallel", "arbitrary"), vmem_limit_bytes=VMEM_LIMIT),
        name="gated_deltanet",
    )(proj3, proj3, proj3, proj3, ba3, conv_w, gate_params, norm_w)


def _rope_tables(positions):
    half = ATT_QK_DIM // 2
    inv_freq = ROPE_THETA ** (-jnp.arange(0, ATT_QK_DIM, 2, dtype=F32) / ATT_QK_DIM)
    reps = LANES // half
    freq_lanes = jnp.tile(inv_freq, reps)
    sign_lanes = jnp.tile(jnp.repeat(jnp.array([-1.0, 1.0], F32), half), reps // 2)
    ang = positions.astype(F32)[..., None] * freq_lanes
    return jnp.cos(ang), jnp.sin(ang) * sign_lanes


def kernel(x, positions, norm_w, w_in, lam_qk, attn_subln_w, w_attn_out, conv_dw_w, conv_dw_b,
           conv_ln_w, conv_ln_b, w_conv_out, dn_conv_w, dn_a_log, dn_dt_bias, dn_norm_w,
           w_dn_out, w_out, final_norm_w):
    B, L, _ = x.shape
    T = B * L
    x2d = x.reshape(T, D_MODEL)
    depth = norm_w.shape[0]
    cosf, sinf = _rope_tables(positions)
    pad_heads = lambda v: jnp.pad(v, (DN_A_LANE, LANES - DN_A_LANE - DN_HEADS))
    w_all = w_in.astype(BF16)
    w_gate = w_all[:, :, ORIG_GATE:]
    w_ba = jnp.pad(w_all[:, :, ORIG_DB:ORIG_GATE], ((0, 0), (0, 0), (0, LANES - 2 * DN_HEADS)))
    wa, wc, wd, wo = (w.astype(BF16) for w in (w_attn_out, w_conv_out, w_dn_out, w_out))
    for l in range(depth):
        proj, ba = _inproj(x2d, norm_w[l][None, :], w_all, w_gate, w_ba, l)
        proj3 = proj.reshape(B, L, N_MAIN)
        lambda_init = 0.8 - 0.6 * math.exp(-0.3 * l)
        ya = _attention(proj3, cosf, sinf, lam_qk[l], attn_subln_w[l][:, None],
                        lambda_init=lambda_init).reshape(T, ATT_HEADS * ATT_V_DIM)
        yc = _conv_branch(proj3, conv_dw_w[l], conv_dw_b[l][None, :], conv_ln_w[l][None, :],
                          conv_ln_b[l][None, :]).reshape(T, CONV_WIDTH)
        gate_params = jnp.stack([pad_heads(dn_a_log[l]), pad_heads(dn_dt_bias[l])])
        yd = _deltanet(proj3, ba.reshape(B, L, LANES), dn_conv_w[l], gate_params,
                       dn_norm_w[l][None, :]).reshape(T, DN_WIDTH)
        x2d = _merge(x2d, ya, yc, yd, proj, wa, wc, wd, wo, final_norm_w[None, :], l,
                     final_norm=(l == depth - 1))
    return x2d.reshape(B, L, D_MODEL)
```

```python
import functools
import math

import jax
import jax.numpy as jnp
from jax import lax
from jax.experimental import pallas as pl
from jax.experimental.pallas import tpu as pltpu

F32 = jnp.float32
BF16 = jnp.bfloat16

D_MODEL = 1024
ATT_HEADS = 8
ATT_QK_DIM = 64
ATT_V_DIM = 128
ROPE_THETA = 10000.0
CONV_WIDTH = 1024
CONV_KERNEL = 31
DN_HEADS = 8
DN_HEAD_DIM = 128
DN_WIDTH = DN_HEADS * DN_HEAD_DIM
DN_SHORT_CONV = 4
DN_CHUNK = 64
RMS_EPS = 1e-6
LN_EPS = 1e-5
L2_EPS = 1e-6

LANES = 128
SUBLANES = 8
VMEM_LIMIT = 56 * 1024 * 1024
COL_AQ, COL_AK, COL_AV, COL_AZ = 0, 1, 2, 3
COL_CA, COL_CG, COL_CZ = 4, 5, 6
COL_DQ, COL_DK, COL_DV, COL_DZ = 7, 8, 9, 10
COL_GATE = 11
N_LEAD_TILES = 11
N_MAIN = 14 * D_MODEL
ORIG_DB = N_LEAD_TILES * D_MODEL
ORIG_GATE = ORIG_DB + 2 * DN_HEADS


def _silu(x):
    return x * jax.nn.sigmoid(x)


def _merge_body(x_ref, a_ref, c_ref, d_ref, ga_ref, gc_ref, gd_ref,
                wa_ref, wc_ref, wd_ref, wo_ref, fw_ref, o_ref, *, final_norm):
    ya = jnp.dot(a_ref[...], wa_ref[...], preferred_element_type=F32)
    yc = jnp.dot(c_ref[...], wc_ref[...], preferred_element_type=F32)
    yd = jnp.dot(d_ref[...], wd_ref[...], preferred_element_type=F32)
    merged = (jax.nn.sigmoid(ga_ref[...].astype(F32)) * ya
              + jax.nn.sigmoid(gc_ref[...].astype(F32)) * yc
              + jax.nn.sigmoid(gd_ref[...].astype(F32)) * yd)
    y = x_ref[...] + jnp.dot(merged.astype(BF16), wo_ref[...], preferred_element_type=F32)
    if final_norm:
        ms = jnp.mean(y * y, axis=-1, keepdims=True)
        y = y * lax.rsqrt(ms + RMS_EPS) * fw_ref[...]
    o_ref[...] = y


def _merge(x2d, ya, yc, yd, proj, wa, wc, wd, wo, final_w, layer, *, final_norm, tm=512):
    T = x2d.shape[0]
    row = lambda i: (i, 0)
    const = lambda i: (0, 0)
    wspec = pl.BlockSpec((None, D_MODEL, D_MODEL), lambda i: (layer, 0, 0))
    return pl.pallas_call(
        functools.partial(_merge_body, final_norm=final_norm),
        grid=(T // tm,),
        in_specs=[
            pl.BlockSpec((tm, D_MODEL), row),
            pl.BlockSpec((tm, D_MODEL), row),
            pl.BlockSpec((tm, D_MODEL), row),
            pl.BlockSpec((tm, D_MODEL), row),
            pl.BlockSpec((tm, D_MODEL), lambda i: (i, COL_GATE)),
            pl.BlockSpec((tm, D_MODEL), lambda i: (i, COL_GATE + 1)),
            pl.BlockSpec((tm, D_MODEL), lambda i: (i, COL_GATE + 2)),
            wspec, wspec, wspec, wspec,
            pl.BlockSpec((1, D_MODEL), const),
        ],
        out_specs=pl.BlockSpec((tm, D_MODEL), row),
        out_shape=jax.ShapeDtypeStruct((T, D_MODEL), F32),
        compiler_params=pltpu.CompilerParams(
            dimension_semantics=("parallel",), vmem_limit_bytes=VMEM_LIMIT),
        name="merge",
    )(x2d, ya, yc, yd, proj, proj, proj, wa, wc, wd, wo, final_w)


CONV_HALO = 32


def _causal_taps(buf_ref, w_ref, r0, rows, lanes, n_taps, halo):
    assert halo >= SUBLANES * ((n_taps - 1) // SUBLANES + 1)
    acc = None
    for r in range(min(SUBLANES, n_taps)):
        part = None
        for a in range((n_taps - 1 - r) // SUBLANES + 1):
            s = SUBLANES * a + r
            start = r0 + (halo - SUBLANES * (a + 1))
            if not isinstance(start, int):
                start = pl.multiple_of(start, SUBLANES)
            term = (w_ref[n_taps - 1 - s:n_taps - s, lanes]
                    * buf_ref[pl.ds(start, rows + SUBLANES), lanes])
            part = term if part is None else part + term
        if r:
            part = pltpu.roll(part, r, axis=0)
        acc = part if acc is None else acc + part
    return acc[SUBLANES:, :]


def _inproj_body(x_ref, nw_ref, w_ref, wg_ref, wba_ref, o_ref, ba_ref, h_ref):
    j = pl.program_id(1)

    @pl.when(j == 0)
    def _():
        x = x_ref[...]
        ms = jnp.mean(x * x, axis=-1, keepdims=True)
        h = (x * lax.rsqrt(ms + RMS_EPS) * nw_ref[...]).astype(BF16)
        h_ref[...] = h
        ba_ref[...] = jnp.dot(h, wba_ref[...], preferred_element_type=F32)

    @pl.when(j < N_LEAD_TILES)
    def _():
        o_ref[...] = jnp.dot(h_ref[...], w_ref[...], preferred_element_type=F32).astype(BF16)

    @pl.when(j >= N_LEAD_TILES)
    def _():
        o_ref[...] = jnp.dot(h_ref[...], wg_ref[...], preferred_element_type=F32).astype(BF16)


def _inproj(x2d, norm_w, w_all, w_gate, w_ba, layer, *, tm=2048):
    T = x2d.shape[0]
    tn = D_MODEL
    const = lambda i, j: (0, 0)
    return pl.pallas_call(
        _inproj_body,
        grid=(T // tm, N_MAIN // tn),
        in_specs=[
            pl.BlockSpec((tm, D_MODEL), lambda i, j: (i, 0)),
            pl.BlockSpec((1, D_MODEL), const),
            pl.BlockSpec((None, D_MODEL, tn),
                         lambda i, j: (layer, 0, jnp.minimum(j, N_LEAD_TILES - 1))),
            pl.BlockSpec((None, D_MODEL, tn),
                         lambda i, j: (layer, 0, jnp.maximum(j - N_LEAD_TILES, 0))),
            pl.BlockSpec((None, D_MODEL, LANES), lambda i, j: (layer, 0, 0)),
        ],
        out_specs=[
            pl.BlockSpec((tm, tn), lambda i, j: (i, j)),
            pl.BlockSpec((tm, LANES), lambda i, j: (i, 0)),
        ],
        out_shape=[
            jax.ShapeDtypeStruct((T, N_MAIN), BF16),
            jax.ShapeDtypeStruct((T, LANES), F32),
        ],
        scratch_shapes=[pltpu.VMEM((tm, D_MODEL), BF16)],
        compiler_params=pltpu.CompilerParams(
            dimension_semantics=("parallel", "arbitrary"), vmem_limit_bytes=VMEM_LIMIT),
        name="inproj",
    )(x2d, norm_w, w_all, w_gate, w_ba)


CONV_ROWS = 128
CONV_STRIP = 128


def _conv_body(a_ref, g_ref, z_ref, w_ref, b_ref, lnw_ref, lnb_ref, o_ref, ubuf, cbuf, *, tl):
    @pl.when(pl.program_id(1) == 0)
    def _():
        ubuf[0:CONV_HALO, :] = jnp.zeros((CONV_HALO, CONV_WIDTH), F32)

    a = a_ref[...].astype(F32)
    g = g_ref[...].astype(F32)
    ubuf[CONV_HALO:CONV_HALO + tl, :] = a * (0.5 * jnp.tanh(0.5 * g) + 0.5)

    bias = b_ref[...]
    lnw = lnw_ref[...]
    lnb = lnb_ref[...]

    def step(r, carry):
        r0 = pl.multiple_of(r * CONV_ROWS, CONV_ROWS)
        for c in range(CONV_WIDTH // CONV_STRIP):
            lanes = slice(c * CONV_STRIP, (c + 1) * CONV_STRIP)
            cbuf[:, lanes] = _causal_taps(ubuf, w_ref, r0, CONV_ROWS, lanes, CONV_KERNEL, CONV_HALO)
        u = cbuf[...] + bias
        mu = jnp.mean(u, axis=-1, keepdims=True)
        uc = u - mu
        var = jnp.mean(uc * uc, axis=-1, keepdims=True)
        y = uc * lax.rsqrt(var + LN_EPS) * lnw + lnb
        zz = z_ref[pl.ds(r0, CONV_ROWS), :].astype(F32)
        o_ref[pl.ds(r0, CONV_ROWS), :] = (_silu(y) * _silu(zz)).astype(BF16)
        return carry

    lax.fori_loop(0, tl // CONV_ROWS, step, 0)
    ubuf[0:CONV_HALO, :] = ubuf[tl:tl + CONV_HALO, :]


def _conv_branch(proj3, dw_w, dw_b, ln_w, ln_b, *, tl=512):
    B, L, _ = proj3.shape
    vec = pl.BlockSpec((1, CONV_WIDTH), lambda b, t: (0, 0))
    return pl.pallas_call(
        functools.partial(_conv_body, tl=tl),
        grid=(B, L // tl),
        in_specs=[
            pl.BlockSpec((None, tl, CONV_WIDTH), lambda b, t: (b, t, COL_CA)),
            pl.BlockSpec((None, tl, CONV_WIDTH), lambda b, t: (b, t, COL_CG)),
            pl.BlockSpec((None, tl, CONV_WIDTH), lambda b, t: (b, t, COL_CZ)),
            pl.BlockSpec((CONV_KERNEL, CONV_WIDTH), lambda b, t: (0, 0)),
            vec, vec, vec,
        ],
        out_specs=pl.BlockSpec((None, tl, CONV_WIDTH), lambda b, t: (b, t, 0)),
        out_shape=jax.ShapeDtypeStruct((B, L, CONV_WIDTH), BF16),
        scratch_shapes=[pltpu.VMEM((CONV_HALO + tl, CONV_WIDTH), F32),
                        pltpu.VMEM((CONV_ROWS, CONV_WIDTH), F32)],
        compiler_params=pltpu.CompilerParams(
            dimension_semantics=("parallel", "arbitrary"), vmem_limit_bytes=VMEM_LIMIT),
        name="conv_branch",
    )(proj3, proj3, proj3, dw_w, dw_b, ln_w, ln_b)


def _nt_dot(a, b):
    return lax.dot_general(a, b, (((1,), (1,)), ((), ())), preferred_element_type=F32)


def _rope(t, cos, sin_signed):
    lane = lax.broadcasted_iota(jnp.int32, t.shape, 1)
    first_half = (lane % ATT_QK_DIM) < (ATT_QK_DIM // 2)
    half = ATT_QK_DIM // 2
    swapped = jnp.where(first_half, pltpu.roll(t, t.shape[1] - half, axis=1),
                        pltpu.roll(t, half, axis=1))
    return t * cos + swapped * sin_signed


ATT_HPS = 2
ATT_ZERO_SLOT = 2
ATT_ONES_ROWS = 16


def _attn_body(q_ref, k_ref, v_ref, z_ref, cos_ref, sin_ref, lam_ref, sw_ref, o_ref,
               kr_ref, vt_ref, m_ref, alpha_ref, acc_ref, s_ref, p_ref, qs_ref, *, lambda_init, tq):
    heads = range(ATT_HPS)
    hl = [slice(h * LANES, (h + 1) * LANES) for h in heads]
    n_blocks = vt_ref.shape[1]

    cos_k = jnp.concatenate([cos_ref[...]] * ATT_HPS, axis=1)
    sin_k = jnp.concatenate([sin_ref[...]] * ATT_HPS, axis=1)
    kr_ref[...] = _rope(k_ref[...].astype(F32), cos_k, sin_k).astype(BF16)

    ri = lax.broadcasted_iota(jnp.int32, (ATT_V_DIM, ATT_V_DIM), 0)
    ci = lax.broadcasted_iota(jnp.int32, (ATT_V_DIM, ATT_V_DIM), 1)
    ident = (ri == ci).astype(BF16)
    for jb in range(n_blocks):
        for h in heads:
            vt_ref[h, jb, 0:ATT_V_DIM, :] = _nt_dot(
                ident, v_ref[jb * tq:(jb + 1) * tq, hl[h]]).astype(BF16)
            vt_ref[h, jb, ATT_V_DIM:, :] = jnp.ones((ATT_ONES_ROWS, tq), BF16)
    for h in heads:
        p_ref[h, ATT_ZERO_SLOT] = jnp.zeros(p_ref.shape[2:], BF16)

    lq = lam_ref[...]
    lam = (jnp.exp(jnp.sum(lq[0:1, :] * lq[1:2, :], axis=-1, keepdims=True))
           - jnp.exp(jnp.sum(lq[2:3, :] * lq[3:4, :], axis=-1, keepdims=True)) + lambda_init)

    def block_rows(qi):
        return pl.ds(pl.multiple_of(qi * tq, tq), tq)

    def scores(h, j):
        return _nt_dot(kr_ref[block_rows(j), hl[h]], qs_ref[:, hl[h]])

    def prologue(qi):
        rows = block_rows(qi)
        cos_q = jnp.concatenate([cos_ref[rows, :]] * ATT_HPS, axis=1)
        sin_q = jnp.concatenate([sin_ref[rows, :]] * ATT_HPS, axis=1)
        q = (_rope(q_ref[rows, :].astype(F32), cos_q, sin_q)
             * (ATT_QK_DIM ** -0.5 * math.log2(math.e)))
        lane = lax.broadcasted_iota(jnp.int32, q.shape, 1) % LANES
        qs_ref[...] = jnp.concatenate([jnp.where(lane < ATT_QK_DIM, q, 0.0),
                                       jnp.where(lane >= ATT_QK_DIM, q, 0.0)], axis=0).astype(BF16)
        m_ref[...] = jnp.full(m_ref.shape, -jnp.inf, F32)
        acc_ref[...] = jnp.zeros(acc_ref.shape, F32)
        alpha_ref[...] = jnp.ones(alpha_ref.shape, F32)
        for h in heads:
            s_ref[h, 0] = scores(h, 0)

    def softmax_step(h, s):
        m_prev = m_ref[h]
        m_new = jnp.maximum(m_prev, jnp.max(s, axis=0, keepdims=True))
        alpha = jnp.exp2(m_prev - m_new)
        p = jnp.exp2(s - m_new)
        m_ref[h] = m_new
        return alpha, p.astype(BF16)

    def pv(h, j, p):
        return jnp.dot(vt_ref[h, j], p, preferred_element_type=F32)

    def prev_slot(t):
        return jnp.where(t == 0, ATT_ZERO_SLOT, 1 - t % 2)

    def trip(t, carry):
        slot = t % 2
        pv_prev = [pv(h, jnp.maximum(t - 1, 0), p_ref[h, prev_slot(t)]) for h in heads]
        s_next = [scores(h, t + 1) for h in heads]
        for h in heads:
            alpha, p = softmax_step(h, s_ref[h, slot])
            p_ref[h, slot] = p
            acc_ref[h] = alpha_ref[h] * acc_ref[h] + pv_prev[h]
            alpha_ref[h] = alpha
        for h in heads:
            s_ref[h, 1 - slot] = s_next[h]
        return carry

    def epilogue(qi):
        slot = qi % 2
        pv_prev = [pv(h, jnp.maximum(qi - 1, 0), p_ref[h, prev_slot(qi)]) for h in heads]
        key = lax.broadcasted_iota(jnp.int32, (tq, 2 * tq), 0)
        qry = lax.broadcasted_iota(jnp.int32, (tq, 2 * tq), 1) % tq
        last = [softmax_step(h, jnp.where(key <= qry, s_ref[h, slot], -jnp.inf)) for h in heads]
        pv_last = [pv(h, qi, last[h][1]) for h in heads]
        outs = []
        for h in heads:
            acc = last[h][0] * (alpha_ref[h] * acc_ref[h] + pv_prev[h]) + pv_last[h]
            o_all = acc[:ATT_V_DIM, :] / acc[ATT_V_DIM:ATT_V_DIM + 1, :]
            ot = o_all[:, :tq] - lam * o_all[:, tq:]
            ms = jnp.mean(ot * ot, axis=0, keepdims=True)
            outs.append((ot * lax.rsqrt(ms + RMS_EPS) * sw_ref[...] * (1.0 - lambda_init)).T)
        return jnp.concatenate(outs, axis=1)

    prologue(0)

    def q_block(qi, carry):
        lax.fori_loop(0, qi, trip, 0)
        on = epilogue(qi)
        prologue(jnp.minimum(qi + 1, n_blocks - 1))
        rows = block_rows(qi)
        o_ref[rows, :] = (on * _silu(z_ref[rows, :].astype(F32))).astype(BF16)
        return carry

    lax.fori_loop(0, n_blocks, q_block, 0)


def _attention(proj3, cosf, sinf, lam_qk, subln_w, *, lambda_init, tq=256):
    B, L, _ = proj3.shape
    hw = ATT_HPS * LANES
    hpb = D_MODEL // hw
    full = lambda col: pl.BlockSpec((None, L, hw), lambda b, h: (b, 0, col * hpb + h))
    table = pl.BlockSpec((None, L, LANES), lambda b, h: (b, 0, 0))
    return pl.pallas_call(
        functools.partial(_attn_body, lambda_init=lambda_init, tq=tq),
        grid=(B, ATT_HEADS // ATT_HPS),
        in_specs=[
            full(COL_AQ), full(COL_AK), full(COL_AV), full(COL_AZ), table, table,
            pl.BlockSpec((4, ATT_QK_DIM), lambda b, h: (0, 0)),
            pl.BlockSpec((ATT_V_DIM, 1), lambda b, h: (0, 0)),
        ],
        out_specs=pl.BlockSpec((None, L, hw), lambda b, h: (b, 0, h)),
        out_shape=jax.ShapeDtypeStruct((B, L, ATT_HEADS * ATT_V_DIM), BF16),
        scratch_shapes=[
            pltpu.VMEM((L, hw), BF16),
            pltpu.VMEM((ATT_HPS, L // tq, ATT_V_DIM + ATT_ONES_ROWS, tq), BF16),
            pltpu.VMEM((ATT_HPS, 1, 2 * tq), F32),
            pltpu.VMEM((ATT_HPS, 1, 2 * tq), F32),
            pltpu.VMEM((ATT_HPS, ATT_V_DIM + ATT_ONES_ROWS, 2 * tq), F32),
            pltpu.VMEM((ATT_HPS, 2, tq, 2 * tq), F32),
            pltpu.VMEM((ATT_HPS, 3, tq, 2 * tq), BF16),
            pltpu.VMEM((2 * tq, hw), BF16),
        ],
        compiler_params=pltpu.CompilerParams(
            dimension_semantics=("parallel", "parallel"), vmem_limit_bytes=VMEM_LIMIT),
        name="diff_attention",
    )(proj3, proj3, proj3, proj3, cosf, sinf, lam_qk, subln_w)


DN_HALO = 8
DN_STRIP = 128
DN_CONV_ROWS = 128
DN_B_LANE = 0
DN_A_LANE = DN_HEADS
DN_ONE_LANE = DN_A_LANE + 3 * DN_HEADS
DN_GROUP = 4


def _split3(x):
    hi = x.astype(BF16).astype(F32)
    r1 = x - hi
    mid = r1.astype(BF16).astype(F32)
    lo = (r1 - mid).astype(BF16).astype(F32)
    return hi, mid, lo


def _dn_body(q_ref, k_ref, v_ref, z_ref, ba_ref, cw_ref, gp_ref, nw_ref, o_ref,
             xbuf, qkv, u_s, w_s, qg_s, kd_s, in_s, dec_s, s_ref, *, tb):
    C = DN_CHUNK
    W = DN_WIDTH
    hd = DN_HEAD_DIM

    @pl.when(pl.program_id(1) == 0)
    def _():
        xbuf[0:DN_HALO, :] = jnp.zeros((DN_HALO, 3 * W), F32)
        s_ref[...] = jnp.zeros(s_ref.shape, F32)

    xbuf[DN_HALO:DN_HALO + tb, 0:W] = q_ref[...].astype(F32)
    xbuf[DN_HALO:DN_HALO + tb, W:2 * W] = k_ref[...].astype(F32)
    xbuf[DN_HALO:DN_HALO + tb, 2 * W:3 * W] = v_ref[...].astype(F32)

    def conv_step(i, carry):
        r0 = pl.multiple_of(i * DN_CONV_ROWS, DN_CONV_ROWS)
        for c in range(3 * W // DN_STRIP):
            lanes = slice(c * DN_STRIP, (c + 1) * DN_STRIP)
            qkv[pl.ds(r0, DN_CONV_ROWS), lanes] = _silu(
                _causal_taps(xbuf, cw_ref, r0, DN_CONV_ROWS, lanes, DN_SHORT_CONV, DN_HALO))
        return carry

    lax.fori_loop(0, tb // DN_CONV_ROWS, conv_step, 0)
    xbuf[0:DN_HALO, :] = xbuf[tb:tb + DN_HALO, :]

    n_chunks = tb // C
    G = DN_GROUP
    n_groups = DN_HEADS // G
    gw = G * hd
    gc_w = G * C

    row = lax.broadcasted_iota(jnp.int32, (C, C), 0)
    col = lax.broadcasted_iota(jnp.int32, (C, C), 1)
    tri = (row >= col).astype(BF16)
    lane = lax.broadcasted_iota(jnp.int32, (C, LANES), 1)
    ri = lax.broadcasted_iota(jnp.int32, (C, gc_w), 0)
    cj = lax.broadcasted_iota(jnp.int32, (C, gc_w), 1) % C
    eye_cat = (ri == cj).astype(F32)
    bd_sq = (lax.broadcasted_iota(jnp.int32, (gc_w, gc_w), 0) // C
             == lax.broadcasted_iota(jnp.int32, (gc_w, gc_w), 1) // C)
    bd_wide = (lax.broadcasted_iota(jnp.int32, (gc_w, gw), 0) // C
               == lax.broadcasted_iota(jnp.int32, (gc_w, gw), 1) // hd)
    neg_a = -jnp.exp(gp_ref[0:1, :])
    dt_bias = gp_ref[1:2, :]
    norm_w = nw_ref[...]

    def block_diag(x, mask):
        return jnp.where(mask, jnp.tile(x, (G, 1)), 0.0).astype(BF16)

    def bdot(a, b):
        return jnp.dot(a, b, preferred_element_type=F32)

    probs = [(ci, g) for ci in range(n_chunks) for g in range(n_groups)]

    gd_lhs, gcb_all, kq_lhs, kn_grp, vb_grp, kbe_grp = {}, {}, {}, {}, {}, {}
    for ci in range(n_chunks):
        r0 = ci * C
        ba = ba_ref[r0:r0 + C, :]
        beta_all = jax.nn.sigmoid(ba)
        xa = ba + dt_bias
        softplus = jnp.maximum(xa, 0.0) + jnp.log1p(jnp.exp(-jnp.abs(xa)))
        g = jnp.where((lane >= DN_A_LANE) & (lane < DN_A_LANE + DN_HEADS), neg_a * softplus, 0.0)
        g_hi, g_mid, g_lo = _split3(g)
        gc = (bdot(tri, g_hi.astype(BF16)) + bdot(tri, g_mid.astype(BF16))
              + bdot(tri, g_lo.astype(BF16)))
        c_hi, c_mid, c_lo = _split3(gc)
        gd_lhs[ci] = (c_hi + pltpu.roll(c_mid, DN_HEADS, axis=1) + pltpu.roll(c_lo, 2 * DN_HEADS, axis=1)
                      + jnp.where((lane >= DN_ONE_LANE) & (lane < DN_ONE_LANE + 3), 1.0, 0.0)).astype(BF16)

        beta_b = jnp.concatenate(
            [jnp.broadcast_to(beta_all[:, DN_B_LANE + h:DN_B_LANE + h + 1], (C, hd))
             for h in range(DN_HEADS)], axis=1)
        gcb = jnp.concatenate(
            [jnp.broadcast_to(gc[:, DN_A_LANE + h:DN_A_LANE + h + 1], (C, hd))
             for h in range(DN_HEADS)], axis=1)
        gcb_all[ci] = gcb
        eg = jnp.exp(gcb)
        glast = gcb[C - 1:C, :]
        dec_s[ci] = jnp.exp(glast)

        q_all = qkv[r0:r0 + C, 0:W]
        k_all = qkv[r0:r0 + C, W:2 * W]
        v_all = qkv[r0:r0 + C, 2 * W:3 * W]
        qn = jnp.concatenate(
            [q_all[:, h * hd:(h + 1) * hd] * lax.rsqrt(jnp.sum(
                q_all[:, h * hd:(h + 1) * hd] ** 2, axis=-1, keepdims=True) + L2_EPS)
             for h in range(DN_HEADS)], axis=1) * (hd ** -0.5)
        kn = jnp.concatenate(
            [k_all[:, h * hd:(h + 1) * hd] * lax.rsqrt(jnp.sum(
                k_all[:, h * hd:(h + 1) * hd] ** 2, axis=-1, keepdims=True) + L2_EPS)
             for h in range(DN_HEADS)], axis=1)
        kb = kn * beta_b
        qg_s[r0:r0 + C, :] = (qn * eg).astype(BF16)
        kd_s[r0:r0 + C, :] = (kn * jnp.exp(glast - gcb)).astype(BF16)
        for gi in range(n_groups):
            gl = slice(gi * gw, (gi + 1) * gw)
            kq_lhs[ci, gi] = jnp.concatenate([kb[:, gl], qn[:, gl]], axis=0).astype(BF16)
            kn_grp[ci, gi] = kn[:, gl]
            vb_grp[ci, gi] = (v_all * beta_b)[:, gl]
            kbe_grp[ci, gi] = (kb * eg)[:, gl]

    gdiff, kq = {}, {}
    for ci, gi in probs:
        blocks = []
        for h in range(gi * G, (gi + 1) * G):
            p_hi, p_mid, p_lo = _split3(gcb_all[ci][:, h * hd:(h + 1) * hd])
            onehot = ((lane == DN_A_LANE + h) | (lane == DN_A_LANE + DN_HEADS + h)
                      | (lane == DN_A_LANE + 2 * DN_HEADS + h))
            blocks.append(jnp.where(lane == DN_ONE_LANE, -p_hi, jnp.where(
                lane == DN_ONE_LANE + 1, -p_mid, jnp.where(
                    lane == DN_ONE_LANE + 2, -p_lo, jnp.where(onehot, 1.0, 0.0)))))
        gd_rhs = jnp.concatenate(blocks, axis=0).astype(BF16)
        gdiff[ci, gi] = _nt_dot(gd_lhs[ci], gd_rhs)
        kq[ci, gi] = _nt_dot(kq_lhs[ci, gi], block_diag(kn_grp[ci, gi], bd_wide))

    m, p = {}, {}
    for pr in probs:
        decay = jnp.exp(jnp.where(ri >= cj, gdiff[pr], -jnp.inf))
        m[pr] = -jnp.where(ri > cj, kq[pr][:C, :] * decay, 0.0)
        ci, gi = pr
        in_s[ci * C:(ci + 1) * C, gi * gc_w:(gi + 1) * gc_w] = (kq[pr][C:, :] * decay).astype(BF16)
        p[pr] = eye_cat + m[pr]

    n_levels = int(math.log2(C)) - 1
    for pr in probs:
        m[pr] = bdot(m[pr].astype(BF16), block_diag(m[pr], bd_sq))
    for lvl in range(n_levels):
        last = lvl == n_levels - 1
        res = {}
        for pr in probs:
            lhs = p[pr] if last else jnp.concatenate([p[pr], m[pr]], axis=0)
            res[pr] = bdot(lhs.astype(BF16), block_diag(m[pr], bd_sq))
        for pr in probs:
            p[pr] = p[pr] + res[pr][:C, :]
            if not last:
                m[pr] = res[pr][C:, :]

    for pr in probs:
        ci, gi = pr
        pb = p[pr].astype(BF16)
        rows = slice(ci * C, (ci + 1) * C)
        gl = slice(gi * gw, (gi + 1) * gw)
        u_s[rows, gl] = bdot(pb, block_diag(vb_grp[pr], bd_wide))
        w_s[rows, gl] = bdot(pb, block_diag(kbe_grp[pr], bd_wide)).astype(BF16)

    def chunk_step(ci, carry):
        r0 = pl.multiple_of(ci * C, C)
        rows = pl.ds(r0, C)
        sb = [s_ref[h].astype(BF16) for h in range(DN_HEADS)]
        ws = [bdot(jnp.concatenate([w_s[rows, h * hd:(h + 1) * hd], qg_s[rows, h * hd:(h + 1) * hd]],
                                   axis=0), sb[h]) for h in range(DN_HEADS)]
        v_new = u_s[rows, :] - jnp.concatenate([x[:C, :] for x in ws], axis=1)
        vnb = v_new.astype(BF16)
        o_intra = [bdot(in_s[rows, gi * gc_w:(gi + 1) * gc_w],
                        block_diag(v_new[:, gi * gw:(gi + 1) * gw], bd_wide))
                   for gi in range(n_groups)]
        dec = dec_s[ci]
        for h in range(DN_HEADS):
            hl = slice(h * hd, (h + 1) * hd)
            s_ref[h] = s_ref[h] * dec[:, hl] + lax.dot_general(
                kd_s[rows, hl], vnb[:, hl], (((0,), (0,)), ((), ())), preferred_element_type=F32)
        o = jnp.concatenate([x[C:, :] for x in ws], axis=1) + jnp.concatenate(o_intra, axis=1)
        for h in range(DN_HEADS):
            hl = slice(h * hd, (h + 1) * hd)
            oh = o[:, hl]
            ms = jnp.mean(oh * oh, axis=-1, keepdims=True)
            on = oh * lax.rsqrt(ms + RMS_EPS) * norm_w
            o_ref[rows, hl] = (on * _silu(z_ref[rows, hl].astype(F32))).astype(BF16)
        return carry

    lax.fori_loop(0, n_chunks, chunk_step, 0)


def _deltanet(proj3, ba3, conv_w, gate_params, norm_w, *, tb=256):
    B, L, _ = proj3.shape
    blk = lambda col: pl.BlockSpec((None, tb, DN_WIDTH), lambda b, t: (b, t, col))
    return pl.pallas_call(
        functools.partial(_dn_body, tb=tb),
        grid=(B, L // tb),
        in_specs=[
            blk(COL_DQ), blk(COL_DK), blk(COL_DV), blk(COL_DZ),
            pl.BlockSpec((None, tb, LANES), lambda b, t: (b, t, 0)),
            pl.BlockSpec((DN_SHORT_CONV, 3 * DN_WIDTH), lambda b, t: (0, 0)),
            pl.BlockSpec((2, LANES), lambda b, t: (0, 0)),
            pl.BlockSpec((1, DN_HEAD_DIM), lambda b, t: (0, 0)),
        ],
        out_specs=pl.BlockSpec((None, tb, DN_WIDTH), lambda b, t: (b, t, 0)),
        out_shape=jax.ShapeDtypeStruct((B, L, DN_WIDTH), BF16),
        scratch_shapes=[
            pltpu.VMEM((DN_HALO + tb, 3 * DN_WIDTH), F32),
            pltpu.VMEM((tb, 3 * DN_WIDTH), F32),
            pltpu.VMEM((tb, DN_WIDTH), F32),
            pltpu.VMEM((tb, DN_WIDTH), BF16),
            pltpu.VMEM((tb, DN_WIDTH), BF16),
            pltpu.VMEM((tb, DN_WIDTH), BF16),
            pltpu.VMEM((tb, DN_HEADS * DN_CHUNK), BF16),
            pltpu.VMEM((tb // DN_CHUNK, 1, DN_WIDTH), F32),
            pltpu.VMEM((DN_HEADS, DN_HEAD_DIM, DN_HEAD_DIM), F32),
        ],
        compiler_params=pltpu.CompilerParams(
            dimension_semantics=("parallel", "arbitrary"), vmem_limit_bytes=VMEM_LIMIT),
        name="gated_deltanet",
    )(proj3, proj3, proj3, proj3, ba3, conv_w, gate_params, norm_w)


def _rope_tables(positions):
    half = ATT_QK_DIM // 2
    inv_freq = ROPE_THETA ** (-jnp.arange(0, ATT_QK_DIM, 2, dtype=F32) / ATT_QK_DIM)
    reps = LANES // half
    freq_lanes = jnp.tile(inv_freq, reps)
    sign_lanes = jnp.tile(jnp.repeat(jnp.array([-1.0, 1.0], F32), half), reps // 2)
    ang = positions.astype(F32)[..., None] * freq_lanes
    return jnp.cos(ang), jnp.sin(ang) * sign_lanes


def kernel(x, positions, norm_w, w_in, lam_qk, attn_subln_w, w_attn_out, conv_dw_w, conv_dw_b,
           conv_ln_w, conv_ln_b, w_conv_out, dn_conv_w, dn_a_log, dn_dt_bias, dn_norm_w,
           w_dn_out, w_out, final_norm_w):
    B, L, _ = x.shape
    T = B * L
    x2d = x.reshape(T, D_MODEL)
    depth = norm_w.shape[0]
    cosf, sinf = _rope_tables(positions)
    pad_heads = lambda v: jnp.pad(v, (DN_A_LANE, LANES - DN_A_LANE - DN_HEADS))
    w_all = w_in.astype(BF16)
    w_gate = w_all[:, :, ORIG_GATE:]
    w_ba = jnp.pad(w_all[:, :, ORIG_DB:ORIG_GATE], ((0, 0), (0, 0), (0, LANES - 2 * DN_HEADS)))
    wa, wc, wd, wo = (w.astype(BF16) for w in (w_attn_out, w_conv_out, w_dn_out, w_out))
    for l in range(depth):
        proj, ba = _inproj(x2d, norm_w[l][None, :], w_all, w_gate, w_ba, l)
        proj3 = proj.reshape(B, L, N_MAIN)
        lambda_init = 0.8 - 0.6 * math.exp(-0.3 * l)
        ya = _attention(proj3, cosf, sinf, lam_qk[l], attn_subln_w[l][:, None],
                        lambda_init=lambda_init).reshape(T, ATT_HEADS * ATT_V_DIM)
        yc = _conv_branch(proj3, conv_dw_w[l], conv_dw_b[l][None, :], conv_ln_w[l][None, :],
                          conv_ln_b[l][None, :]).reshape(T, CONV_WIDTH)
        gate_params = jnp.stack([pad_heads(dn_a_log[l]), pad_heads(dn_dt_bias[l])])
        yd = _deltanet(proj3, ba.reshape(B, L, LANES), dn_conv_w[l], gate_params,
                       dn_norm_w[l][None, :]).reshape(T, DN_WIDTH)
        x2d = _merge(x2d, ya, yc, yd, proj, wa, wc, wd, wo, final_norm_w[None, :], l,
                     final_norm=(l == depth - 1))
    return x2d.reshape(B, L, D_MODEL)
```

```python
import functools
import math

import jax
import jax.numpy as jnp
from jax import lax
from jax.experimental import pallas as pl
from jax.experimental.pallas import tpu as pltpu

F32 = jnp.float32
BF16 = jnp.bfloat16

D_MODEL = 1024
ATT_HEADS = 8
ATT_QK_DIM = 64
ATT_V_DIM = 128
ROPE_THETA = 10000.0
CONV_WIDTH = 1024
CONV_KERNEL = 31
DN_HEADS = 8
DN_HEAD_DIM = 128
DN_WIDTH = DN_HEADS * DN_HEAD_DIM
DN_SHORT_CONV = 4
DN_CHUNK = 64
RMS_EPS = 1e-6
LN_EPS = 1e-5
L2_EPS = 1e-6

LANES = 128
SUBLANES = 8
VMEM_LIMIT = 56 * 1024 * 1024
COL_AQ, COL_AK, COL_AV, COL_AZ = 0, 1, 2, 3
COL_CA, COL_CG, COL_CZ = 4, 5, 6
COL_DQ, COL_DK, COL_DV, COL_DZ = 7, 8, 9, 10
COL_GATE = 11
N_LEAD_TILES = 11
N_MAIN = 14 * D_MODEL
ORIG_DB = N_LEAD_TILES * D_MODEL
ORIG_GATE = ORIG_DB + 2 * DN_HEADS


def _silu(x):
    return x * jax.nn.sigmoid(x)


def _merge_body(x_ref, a_ref, c_ref, d_ref, ga_ref, gc_ref, gd_ref,
                wa_ref, wc_ref, wd_ref, wo_ref, fw_ref, o_ref, *, final_norm):
    ya = jnp.dot(a_ref[...], wa_ref[...], preferred_element_type=F32)
    yc = jnp.dot(c_ref[...], wc_ref[...], preferred_element_type=F32)
    yd = jnp.dot(d_ref[...], wd_ref[...], preferred_element_type=F32)
    merged = (jax.nn.sigmoid(ga_ref[...].astype(F32)) * ya
              + jax.nn.sigmoid(gc_ref[...].astype(F32)) * yc
              + jax.nn.sigmoid(gd_ref[...].astype(F32)) * yd)
    y = x_ref[...] + jnp.dot(merged.astype(BF16), wo_ref[...], preferred_element_type=F32)
    if final_norm:
        ms = jnp.mean(y * y, axis=-1, keepdims=True)
        y = y * lax.rsqrt(ms + RMS_EPS) * fw_ref[...]
    o_ref[...] = y


def _merge(x2d, ya, yc, yd, proj, wa, wc, wd, wo, final_w, layer, *, final_norm, tm=512):
    T = x2d.shape[0]
    row = lambda i: (i, 0)
    const = lambda i: (0, 0)
    wspec = pl.BlockSpec((None, D_MODEL, D_MODEL), lambda i: (layer, 0, 0))
    return pl.pallas_call(
        functools.partial(_merge_body, final_norm=final_norm),
        grid=(T // tm,),
        in_specs=[
            pl.BlockSpec((tm, D_MODEL), row),
            pl.BlockSpec((tm, D_MODEL), row),
            pl.BlockSpec((tm, D_MODEL), row),
            pl.BlockSpec((tm, D_MODEL), row),
            pl.BlockSpec((tm, D_MODEL), lambda i: (i, COL_GATE)),
            pl.BlockSpec((tm, D_MODEL), lambda i: (i, COL_GATE + 1)),
            pl.BlockSpec((tm, D_MODEL), lambda i: (i, COL_GATE + 2)),
            wspec, wspec, wspec, wspec,
            pl.BlockSpec((1, D_MODEL), const),
        ],
        out_specs=pl.BlockSpec((tm, D_MODEL), row),
        out_shape=jax.ShapeDtypeStruct((T, D_MODEL), F32),
        compiler_params=pltpu.CompilerParams(
            dimension_semantics=("parallel",), vmem_limit_bytes=VMEM_LIMIT),
        name="merge",
    )(x2d, ya, yc, yd, proj, proj, proj, wa, wc, wd, wo, final_w)


CONV_HALO = 32


def _causal_taps(buf_ref, w_ref, r0, rows, lanes, n_taps, halo):
    assert halo >= SUBLANES * ((n_taps - 1) // SUBLANES + 1)
    acc = None
    for r in range(min(SUBLANES, n_taps)):
        part = None
        for a in range((n_taps - 1 - r) // SUBLANES + 1):
            s = SUBLANES * a + r
            start = r0 + (halo - SUBLANES * (a + 1))
            if not isinstance(start, int):
                start = pl.multiple_of(start, SUBLANES)
            term = (w_ref[n_taps - 1 - s:n_taps - s, lanes]
                    * buf_ref[pl.ds(start, rows + SUBLANES), lanes])
            part = term if part is None else part + term
        if r:
            part = pltpu.roll(part, r, axis=0)
        acc = part if acc is None else acc + part
    return acc[SUBLANES:, :]


def _inproj_body(x_ref, nw_ref, w_ref, wg_ref, wba_ref, o_ref, ba_ref, h_ref):
    j = pl.program_id(1)

    @pl.when(j == 0)
    def _():
        x = x_ref[...]
        ms = jnp.mean(x * x, axis=-1, keepdims=True)
        h = (x * lax.rsqrt(ms + RMS_EPS) * nw_ref[...]).astype(BF16)
        h_ref[...] = h
        ba_ref[...] = jnp.dot(h, wba_ref[...], preferred_element_type=F32)

    @pl.when(j < N_LEAD_TILES)
    def _():
        o_ref[...] = jnp.dot(h_ref[...], w_ref[...], preferred_element_type=F32).astype(BF16)

    @pl.when(j >= N_LEAD_TILES)
    def _():
        o_ref[...] = jnp.dot(h_ref[...], wg_ref[...], preferred_element_type=F32).astype(BF16)


def _inproj(x2d, norm_w, w_all, w_gate, w_ba, layer, *, tm=2048):
    T = x2d.shape[0]
    tn = D_MODEL
    const = lambda i, j: (0, 0)
    return pl.pallas_call(
        _inproj_body,
        grid=(T // tm, N_MAIN // tn),
        in_specs=[
            pl.BlockSpec((tm, D_MODEL), lambda i, j: (i, 0)),
            pl.BlockSpec((1, D_MODEL), const),
            pl.BlockSpec((None, D_MODEL, tn),
                         lambda i, j: (layer, 0, jnp.minimum(j, N_LEAD_TILES - 1))),
            pl.BlockSpec((None, D_MODEL, tn),
                         lambda i, j: (layer, 0, jnp.maximum(j - N_LEAD_TILES, 0))),
            pl.BlockSpec((None, D_MODEL, LANES), lambda i, j: (layer, 0, 0)),
        ],
        out_specs=[
            pl.BlockSpec((tm, tn), lambda i, j: (i, j)),
            pl.BlockSpec((tm, LANES), lambda i, j: (i, 0)),
        ],
        out_shape=[
            jax.ShapeDtypeStruct((T, N_MAIN), BF16),
            jax.ShapeDtypeStruct((T, LANES), F32),
        ],
        scratch_shapes=[pltpu.VMEM((tm, D_MODEL), BF16)],
        compiler_params=pltpu.CompilerParams(
            dimension_semantics=("parallel", "arbitrary"), vmem_limit_bytes=VMEM_LIMIT),
        name="inproj",
    )(x2d, norm_w, w_all, w_gate, w_ba)


CONV_ROWS = 128
CONV_STRIP = 128


def _conv_body(a_ref, g_ref, z_ref, w_ref, b_ref, lnw_ref, lnb_ref, o_ref, ubuf, cbuf, *, tl):
    @pl.when(pl.program_id(1) == 0)
    def _():
        ubuf[0:CONV_HALO, :] = jnp.zeros((CONV_HALO, CONV_WIDTH), F32)

    a = a_ref[...].astype(F32)
    g = g_ref[...].astype(F32)
    ubuf[CONV_HALO:CONV_HALO + tl, :] = a * (0.5 * jnp.tanh(0.5 * g) + 0.5)

    bias = b_ref[...]
    lnw = lnw_ref[...]
    lnb = lnb_ref[...]

    def step(r, carry):
        r0 = pl.multiple_of(r * CONV_ROWS, CONV_ROWS)
        for c in range(CONV_WIDTH // CONV_STRIP):
            lanes = slice(c * CONV_STRIP, (c + 1) * CONV_STRIP)
            cbuf[:, lanes] = _causal_taps(ubuf, w_ref, r0, CONV_ROWS, lanes, CONV_KERNEL, CONV_HALO)
        u = cbuf[...] + bias
        mu = jnp.mean(u, axis=-1, keepdims=True)
        uc = u - mu
        var = jnp.mean(uc * uc, axis=-1, keepdims=True)
        y = uc * lax.rsqrt(var + LN_EPS) * lnw + lnb
        zz = z_ref[pl.ds(r0, CONV_ROWS), :].astype(F32)
        o_ref[pl.ds(r0, CONV_ROWS), :] = (_silu(y) * _silu(zz)).astype(BF16)
        return carry

    lax.fori_loop(0, tl // CONV_ROWS, step, 0)
    ubuf[0:CONV_HALO, :] = ubuf[tl:tl + CONV_HALO, :]


def _conv_branch(proj3, dw_w, dw_b, ln_w, ln_b, *, tl=512):
    B, L, _ = proj3.shape
    vec = pl.BlockSpec((1, CONV_WIDTH), lambda b, t: (0, 0))
    return pl.pallas_call(
        functools.partial(_conv_body, tl=tl),
        grid=(B, L // tl),
        in_specs=[
            pl.BlockSpec((None, tl, CONV_WIDTH), lambda b, t: (b, t, COL_CA)),
            pl.BlockSpec((None, tl, CONV_WIDTH), lambda b, t: (b, t, COL_CG)),
            pl.BlockSpec((None, tl, CONV_WIDTH), lambda b, t: (b, t, COL_CZ)),
            pl.BlockSpec((CONV_KERNEL, CONV_WIDTH), lambda b, t: (0, 0)),
            vec, vec, vec,
        ],
        out_specs=pl.BlockSpec((None, tl, CONV_WIDTH), lambda b, t: (b, t, 0)),
        out_shape=jax.ShapeDtypeStruct((B, L, CONV_WIDTH), BF16),
        scratch_shapes=[pltpu.VMEM((CONV_HALO + tl, CONV_WIDTH), F32),
                        pltpu.VMEM((CONV_ROWS, CONV_WIDTH), F32)],
        compiler_params=pltpu.CompilerParams(
            dimension_semantics=("parallel", "arbitrary"), vmem_limit_bytes=VMEM_LIMIT),
        name="conv_branch",
    )(proj3, proj3, proj3, dw_w, dw_b, ln_w, ln_b)


def _nt_dot(a, b):
    return lax.dot_general(a, b, (((1,), (1,)), ((), ())), preferred_element_type=F32)


def _rope(t, cos, sin_signed):
    lane = lax.broadcasted_iota(jnp.int32, t.shape, 1)
    first_half = (lane % ATT_QK_DIM) < (ATT_QK_DIM // 2)
    half = ATT_QK_DIM // 2
    swapped = jnp.where(first_half, pltpu.roll(t, t.shape[1] - half, axis=1),
                        pltpu.roll(t, half, axis=1))
    return t * cos + swapped * sin_signed


ATT_HPS = 2
ATT_ZERO_SLOT = 2
ATT_ONES_ROWS = 16


def _attn_body(q_ref, k_ref, v_ref, z_ref, cos_ref, sin_ref, lam_ref, sw_ref, o_ref,
               kr_ref, vt_ref, m_ref, alpha_ref, acc_ref, s_ref, p_ref, qs_ref, *, lambda_init, tq):
    heads = range(ATT_HPS)
    hl = [slice(h * LANES, (h + 1) * LANES) for h in heads]
    n_blocks = vt_ref.shape[1]

    cos_k = jnp.concatenate([cos_ref[...]] * ATT_HPS, axis=1)
    sin_k = jnp.concatenate([sin_ref[...]] * ATT_HPS, axis=1)
    kr_ref[...] = _rope(k_ref[...].astype(F32), cos_k, sin_k).astype(BF16)

    ri = lax.broadcasted_iota(jnp.int32, (ATT_V_DIM, ATT_V_DIM), 0)
    ci = lax.broadcasted_iota(jnp.int32, (ATT_V_DIM, ATT_V_DIM), 1)
    ident = (ri == ci).astype(BF16)
    for jb in range(n_blocks):
        for h in heads:
            vt_ref[h, jb, 0:ATT_V_DIM, :] = _nt_dot(
                ident, v_ref[jb * tq:(jb + 1) * tq, hl[h]]).astype(BF16)
            vt_ref[h, jb, ATT_V_DIM:, :] = jnp.ones((ATT_ONES_ROWS, tq), BF16)
    for h in heads:
        p_ref[h, ATT_ZERO_SLOT] = jnp.zeros(p_ref.shape[2:], BF16)

    lq = lam_ref[...]
    lam = (jnp.exp(jnp.sum(lq[0:1, :] * lq[1:2, :], axis=-1, keepdims=True))
           - jnp.exp(jnp.sum(lq[2:3, :] * lq[3:4, :], axis=-1, keepdims=True)) + lambda_init)

    def block_rows(qi):
        return pl.ds(pl.multiple_of(qi * tq, tq), tq)

    def scores(h, j):
        return _nt_dot(kr_ref[block_rows(j), hl[h]], qs_ref[:, hl[h]])

    def prologue(qi):
        rows = block_rows(qi)
        cos_q = jnp.concatenate([cos_ref[rows, :]] * ATT_HPS, axis=1)
        sin_q = jnp.concatenate([sin_ref[rows, :]] * ATT_HPS, axis=1)
        q = (_rope(q_ref[rows, :].astype(F32), cos_q, sin_q)
             * (ATT_QK_DIM ** -0.5 * math.log2(math.e)))
        lane = lax.broadcasted_iota(jnp.int32, q.shape, 1) % LANES
        qs_ref[...] = jnp.concatenate([jnp.where(lane < ATT_QK_DIM, q, 0.0),
                                       jnp.where(lane >= ATT_QK_DIM, q, 0.0)], axis=0).astype(BF16)
        m_ref[...] = jnp.full(m_ref.shape, -jnp.inf, F32)
        acc_ref[...] = jnp.zeros(acc_ref.shape, F32)
        alpha_ref[...] = jnp.ones(alpha_ref.shape, F32)
        for h in heads:
            s_ref[h, 0] = scores(h, 0)

    def softmax_step(h, s):
        m_prev = m_ref[h]
        m_new = jnp.maximum(m_prev, jnp.max(s, axis=0, keepdims=True))
        alpha = jnp.exp2(m_prev - m_new)
        p = jnp.exp2(s - m_new)
        m_ref[h] = m_new
        return alpha, p.astype(BF16)

    def pv(h, j, p):
        return jnp.dot(vt_ref[h, j], p, preferred_element_type=F32)

    def prev_slot(t):
        return jnp.where(t == 0, ATT_ZERO_SLOT, 1 - t % 2)

    def trip(t, carry):
        slot = t % 2
        pv_prev = [pv(h, jnp.maximum(t - 1, 0), p_ref[h, prev_slot(t)]) for h in heads]
        s_next = [scores(h, t + 1) for h in heads]
        for h in heads:
            alpha, p = softmax_step(h, s_ref[h, slot])
            p_ref[h, slot] = p
            acc_ref[h] = alpha_ref[h] * acc_ref[h] + pv_prev[h]
            alpha_ref[h] = alpha
        for h in heads:
            s_ref[h, 1 - slot] = s_next[h]
        return carry

    def trip_pair(g, carry):
        t = 2 * g
        pv_a = [pv(h, jnp.maximum(t - 1, 0), p_ref[h, prev_slot(t)]) for h in heads]
        s_a = [scores(h, t + 1) for h in heads]
        s_b = [scores(h, t + 2) for h in heads]
        p_a = []
        for h in heads:
            alpha, p = softmax_step(h, s_ref[h, 0])
            p_a.append(p)
            acc_ref[h] = alpha_ref[h] * acc_ref[h] + pv_a[h]
            alpha_ref[h] = alpha
        pv_b = [pv(h, t, p_a[h]) for h in heads]
        for h in heads:
            alpha, p = softmax_step(h, s_a[h])
            p_ref[h, 1] = p
            acc_ref[h] = alpha_ref[h] * acc_ref[h] + pv_b[h]
            alpha_ref[h] = alpha
        for h in heads:
            s_ref[h, 0] = s_b[h]
        return carry

    def epilogue(qi):
        slot = qi % 2
        pv_prev = [pv(h, jnp.maximum(qi - 1, 0), p_ref[h, prev_slot(qi)]) for h in heads]
        key = lax.broadcasted_iota(jnp.int32, (tq, 2 * tq), 0)
        qry = lax.broadcasted_iota(jnp.int32, (tq, 2 * tq), 1) % tq
        last = [softmax_step(h, jnp.where(key <= qry, s_ref[h, slot], -jnp.inf)) for h in heads]
        pv_last = [pv(h, qi, last[h][1]) for h in heads]
        outs = []
        for h in heads:
            acc = last[h][0] * (alpha_ref[h] * acc_ref[h] + pv_prev[h]) + pv_last[h]
            o_all = acc[:ATT_V_DIM, :] / acc[ATT_V_DIM:ATT_V_DIM + 1, :]
            ot = o_all[:, :tq] - lam * o_all[:, tq:]
            ms = jnp.mean(ot * ot, axis=0, keepdims=True)
            outs.append((ot * lax.rsqrt(ms + RMS_EPS) * sw_ref[...] * (1.0 - lambda_init)).T)
        return jnp.concatenate(outs, axis=1)

    prologue(0)

    def q_block(qi, carry):
        lax.fori_loop(0, qi // 2, trip_pair, 0)

        @pl.when(qi % 2 == 1)
        def _():
            trip(qi - 1, 0)

        on = epilogue(qi)
        prologue(jnp.minimum(qi + 1, n_blocks - 1))
        rows = block_rows(qi)
        o_ref[rows, :] = (on * _silu(z_ref[rows, :].astype(F32))).astype(BF16)
        return carry

    lax.fori_loop(0, n_blocks, q_block, 0)


def _attention(proj3, cosf, sinf, lam_qk, subln_w, *, lambda_init, tq=256):
    B, L, _ = proj3.shape
    hw = ATT_HPS * LANES
    hpb = D_MODEL // hw
    full = lambda col: pl.BlockSpec((None, L, hw), lambda b, h: (b, 0, col * hpb + h))
    table = pl.BlockSpec((None, L, LANES), lambda b, h: (b, 0, 0))
    return pl.pallas_call(
        functools.partial(_attn_body, lambda_init=lambda_init, tq=tq),
        grid=(B, ATT_HEADS // ATT_HPS),
        in_specs=[
            full(COL_AQ), full(COL_AK), full(COL_AV), full(COL_AZ), table, table,
            pl.BlockSpec((4, ATT_QK_DIM), lambda b, h: (0, 0)),
            pl.BlockSpec((ATT_V_DIM, 1), lambda b, h: (0, 0)),
        ],
        out_specs=pl.BlockSpec((None, L, hw), lambda b, h: (b, 0, h)),
        out_shape=jax.ShapeDtypeStruct((B, L, ATT_HEADS * ATT_V_DIM), BF16),
        scratch_shapes=[
            pltpu.VMEM((L, hw), BF16),
            pltpu.VMEM((ATT_HPS, L // tq, ATT_V_DIM + ATT_ONES_ROWS, tq), BF16),
            pltpu.VMEM((ATT_HPS, 1, 2 * tq), F32),
            pltpu.VMEM((ATT_HPS, 1, 2 * tq), F32),
            pltpu.VMEM((ATT_HPS, ATT_V_DIM + ATT_ONES_ROWS, 2 * tq), F32),
            pltpu.VMEM((ATT_HPS, 2, tq, 2 * tq), F32),
            pltpu.VMEM((ATT_HPS, 3, tq, 2 * tq), BF16),
            pltpu.VMEM((2 * tq, hw), BF16),
        ],
        compiler_params=pltpu.CompilerParams(
            dimension_semantics=("parallel", "parallel"), vmem_limit_bytes=VMEM_LIMIT),
        name="diff_attention",
    )(proj3, proj3, proj3, proj3, cosf, sinf, lam_qk, subln_w)


DN_HALO = 8
DN_STRIP = 128
DN_CONV_ROWS = 128
DN_B_LANE = 0
DN_A_LANE = DN_HEADS
DN_ONE_LANE = DN_A_LANE + 3 * DN_HEADS
DN_GROUP = 4


def _split3(x):
    hi = x.astype(BF16).astype(F32)
    r1 = x - hi
    mid = r1.astype(BF16).astype(F32)
    lo = (r1 - mid).astype(BF16).astype(F32)
    return hi, mid, lo


def _dn_body(q_ref, k_ref, v_ref, z_ref, ba_ref, cw_ref, gp_ref, nw_ref, o_ref,
             xbuf, qkv, u_s, w_s, qg_s, kd_s, in_s, dec_s, s_ref, *, tb):
    C = DN_CHUNK
    W = DN_WIDTH
    hd = DN_HEAD_DIM

    @pl.when(pl.program_id(1) == 0)
    def _():
        xbuf[0:DN_HALO, :] = jnp.zeros((DN_HALO, 3 * W), F32)
        s_ref[...] = jnp.zeros(s_ref.shape, F32)

    xbuf[DN_HALO:DN_HALO + tb, 0:W] = q_ref[...].astype(F32)
    xbuf[DN_HALO:DN_HALO + tb, W:2 * W] = k_ref[...].astype(F32)
    xbuf[DN_HALO:DN_HALO + tb, 2 * W:3 * W] = v_ref[...].astype(F32)

    def conv_step(i, carry):
        r0 = pl.multiple_of(i * DN_CONV_ROWS, DN_CONV_ROWS)
        for c in range(3 * W // DN_STRIP):
            lanes = slice(c * DN_STRIP, (c + 1) * DN_STRIP)
            qkv[pl.ds(r0, DN_CONV_ROWS), lanes] = _silu(
                _causal_taps(xbuf, cw_ref, r0, DN_CONV_ROWS, lanes, DN_SHORT_CONV, DN_HALO))
        return carry

    lax.fori_loop(0, tb // DN_CONV_ROWS, conv_step, 0)
    xbuf[0:DN_HALO, :] = xbuf[tb:tb + DN_HALO, :]

    n_chunks = tb // C
    G = DN_GROUP
    n_groups = DN_HEADS // G
    gw = G * hd
    gc_w = G * C

    row = lax.broadcasted_iota(jnp.int32, (C, C), 0)
    col = lax.broadcasted_iota(jnp.int32, (C, C), 1)
    tri = (row >= col).astype(BF16)
    lane = lax.broadcasted_iota(jnp.int32, (C, LANES), 1)
    ri = lax.broadcasted_iota(jnp.int32, (C, gc_w), 0)
    cj = lax.broadcasted_iota(jnp.int32, (C, gc_w), 1) % C
    eye_cat = (ri == cj).astype(F32)
    bd_sq = (lax.broadcasted_iota(jnp.int32, (gc_w, gc_w), 0) // C
             == lax.broadcasted_iota(jnp.int32, (gc_w, gc_w), 1) // C)
    bd_wide = (lax.broadcasted_iota(jnp.int32, (gc_w, gw), 0) // C
               == lax.broadcasted_iota(jnp.int32, (gc_w, gw), 1) // hd)
    neg_a = -jnp.exp(gp_ref[0:1, :])
    dt_bias = gp_ref[1:2, :]
    norm_w = nw_ref[...]

    def block_diag(x, mask):
        return jnp.where(mask, jnp.tile(x, (G, 1)), 0.0).astype(BF16)

    def bdot(a, b):
        return jnp.dot(a, b, preferred_element_type=F32)

    probs = [(ci, g) for ci in range(n_chunks) for g in range(n_groups)]

    gd_lhs, gcb_all, kq_lhs, kn_grp, vb_grp, kbe_grp = {}, {}, {}, {}, {}, {}
    for ci in range(n_chunks):
        r0 = ci * C
        ba = ba_ref[r0:r0 + C, :]
        beta_all = jax.nn.sigmoid(ba)
        xa = ba + dt_bias
        softplus = jnp.maximum(xa, 0.0) + jnp.log1p(jnp.exp(-jnp.abs(xa)))
        g = jnp.where((lane >= DN_A_LANE) & (lane < DN_A_LANE + DN_HEADS), neg_a * softplus, 0.0)
        g_hi, g_mid, g_lo = _split3(g)
        gc = (bdot(tri, g_hi.astype(BF16)) + bdot(tri, g_mid.astype(BF16))
              + bdot(tri, g_lo.astype(BF16)))
        c_hi, c_mid, c_lo = _split3(gc)
        gd_lhs[ci] = (c_hi + pltpu.roll(c_mid, DN_HEADS, axis=1) + pltpu.roll(c_lo, 2 * DN_HEADS, axis=1)
                      + jnp.where((lane >= DN_ONE_LANE) & (lane < DN_ONE_LANE + 3), 1.0, 0.0)).astype(BF16)

        beta_b = jnp.concatenate(
            [jnp.broadcast_to(beta_all[:, DN_B_LANE + h:DN_B_LANE + h + 1], (C, hd))
             for h in range(DN_HEADS)], axis=1)
        gcb = jnp.concatenate(
            [jnp.broadcast_to(gc[:, DN_A_LANE + h:DN_A_LANE + h + 1], (C, hd))
             for h in range(DN_HEADS)], axis=1)
        gcb_all[ci] = gcb
        eg = jnp.exp(gcb)
        glast = gcb[C - 1:C, :]
        dec_s[ci] = jnp.exp(glast)

        q_all = qkv[r0:r0 + C, 0:W]
        k_all = qkv[r0:r0 + C, W:2 * W]
        v_all = qkv[r0:r0 + C, 2 * W:3 * W]
        qn = jnp.concatenate(
            [q_all[:, h * hd:(h + 1) * hd] * lax.rsqrt(jnp.sum(
                q_all[:, h * hd:(h + 1) * hd] ** 2, axis=-1, keepdims=True) + L2_EPS)
             for h in range(DN_HEADS)], axis=1) * (hd ** -0.5)
        kn = jnp.concatenate(
            [k_all[:, h * hd:(h + 1) * hd] * lax.rsqrt(jnp.sum(
                k_all[:, h * hd:(h + 1) * hd] ** 2, axis=-1, keepdims=True) + L2_EPS)
             for h in range(DN_HEADS)], axis=1)
        kb = kn * beta_b
        qg_s[r0:r0 + C, :] = (qn * eg).astype(BF16)
        kd_s[r0:r0 + C, :] = (kn * jnp.exp(glast - gcb)).astype(BF16)
        for gi in range(n_groups):
            gl = slice(gi * gw, (gi + 1) * gw)
            kq_lhs[ci, gi] = jnp.concatenate([kb[:, gl], qn[:, gl]], axis=0).astype(BF16)
            kn_grp[ci, gi] = kn[:, gl]
            vb_grp[ci, gi] = (v_all * beta_b)[:, gl]
            kbe_grp[ci, gi] = (kb * eg)[:, gl]

    gdiff, kq = {}, {}
    for ci, gi in probs:
        blocks = []
        for h in range(gi * G, (gi + 1) * G):
            p_hi, p_mid, p_lo = _split3(gcb_all[ci][:, h * hd:(h + 1) * hd])
            onehot = ((lane == DN_A_LANE + h) | (lane == DN_A_LANE + DN_HEADS + h)
                      | (lane == DN_A_LANE + 2 * DN_HEADS + h))
            blocks.append(jnp.where(lane == DN_ONE_LANE, -p_hi, jnp.where(
                lane == DN_ONE_LANE + 1, -p_mid, jnp.where(
                    lane == DN_ONE_LANE + 2, -p_lo, jnp.where(onehot, 1.0, 0.0)))))
        gd_rhs = jnp.concatenate(blocks, axis=0).astype(BF16)
        gdiff[ci, gi] = _nt_dot(gd_lhs[ci], gd_rhs)
        kq[ci, gi] = _nt_dot(kq_lhs[ci, gi], block_diag(kn_grp[ci, gi], bd_wide))

    m, p = {}, {}
    for pr in probs:
        decay = jnp.exp(jnp.where(ri >= cj, gdiff[pr], -jnp.inf))
        m[pr] = -jnp.where(ri > cj, kq[pr][:C, :] * decay, 0.0)
        ci, gi = pr
        in_s[ci * C:(ci + 1) * C, gi * gc_w:(gi + 1) * gc_w] = (kq[pr][C:, :] * decay).astype(BF16)
        p[pr] = eye_cat + m[pr]

    n_levels = int(math.log2(C)) - 1
    for pr in probs:
        m[pr] = bdot(m[pr].astype(BF16), block_diag(m[pr], bd_sq))
    for lvl in range(n_levels):
        last = lvl == n_levels - 1
        res = {}
        for pr in probs:
            lhs = p[pr] if last else jnp.concatenate([p[pr], m[pr]], axis=0)
            res[pr] = bdot(lhs.astype(BF16), block_diag(m[pr], bd_sq))
        for pr in probs:
            p[pr] = p[pr] + res[pr][:C, :]
            if not last:
                m[pr] = res[pr][C:, :]

    for pr in probs:
        ci, gi = pr
        pb = p[pr].astype(BF16)
        rows = slice(ci * C, (ci + 1) * C)
        gl = slice(gi * gw, (gi + 1) * gw)
        u_s[rows, gl] = bdot(pb, block_diag(vb_grp[pr], bd_wide))
        w_s[rows, gl] = bdot(pb, block_diag(kbe_grp[pr], bd_wide)).astype(BF16)

    def chunk_step(ci, carry):
        r0 = pl.multiple_of(ci * C, C)
        rows = pl.ds(r0, C)
        sb = [s_ref[h].astype(BF16) for h in range(DN_HEADS)]
        ws = [bdot(jnp.concatenate([w_s[rows, h * hd:(h + 1) * hd], qg_s[rows, h * hd:(h + 1) * hd]],
                                   axis=0), sb[h]) for h in range(DN_HEADS)]
        v_new = u_s[rows, :] - jnp.concatenate([x[:C, :] for x in ws], axis=1)
        vnb = v_new.astype(BF16)
        o_intra = [bdot(in_s[rows, gi * gc_w:(gi + 1) * gc_w],
                        block_diag(v_new[:, gi * gw:(gi + 1) * gw], bd_wide))
                   for gi in range(n_groups)]
        dec = dec_s[ci]
        for h in range(DN_HEADS):
            hl = slice(h * hd, (h + 1) * hd)
            s_ref[h] = s_ref[h] * dec[:, hl] + lax.dot_general(
                kd_s[rows, hl], vnb[:, hl], (((0,), (0,)), ((), ())), preferred_element_type=F32)
        o = jnp.concatenate([x[C:, :] for x in ws], axis=1) + jnp.concatenate(o_intra, axis=1)
        for h in range(DN_HEADS):
            hl = slice(h * hd, (h + 1) * hd)
            oh = o[:, hl]
            ms = jnp.mean(oh * oh, axis=-1, keepdims=True)
            on = oh * lax.rsqrt(ms + RMS_EPS) * norm_w
            o_ref[rows, hl] = (on * _silu(z_ref[rows, hl].astype(F32))).astype(BF16)
        return carry

    lax.fori_loop(0, n_chunks, chunk_step, 0)


def _deltanet(proj3, ba3, conv_w, gate_params, norm_w, *, tb=256):
    B, L, _ = proj3.shape
    blk = lambda col: pl.BlockSpec((None, tb, DN_WIDTH), lambda b, t: (b, t, col))
    return pl.pallas_call(
        functools.partial(_dn_body, tb=tb),
        grid=(B, L // tb),
        in_specs=[
            blk(COL_DQ), blk(COL_DK), blk(COL_DV), blk(COL_DZ),
            pl.BlockSpec((None, tb, LANES), lambda b, t: (b, t, 0)),
            pl.BlockSpec((DN_SHORT_CONV, 3 * DN_WIDTH), lambda b, t: (0, 0)),
            pl.BlockSpec((2, LANES), lambda b, t: (0, 0)),
            pl.BlockSpec((1, DN_HEAD_DIM), lambda b, t: (0, 0)),
        ],
        out_specs=pl.BlockSpec((None, tb, DN_WIDTH), lambda b, t: (b, t, 0)),
        out_shape=jax.ShapeDtypeStruct((B, L, DN_WIDTH), BF16),
        scratch_shapes=[
            pltpu.VMEM((DN_HALO + tb, 3 * DN_WIDTH), F32),
            pltpu.VMEM((tb, 3 * DN_WIDTH), F32),
            pltpu.VMEM((tb, DN_WIDTH), F32),
            pltpu.VMEM((tb, DN_WIDTH), BF16),
            pltpu.VMEM((tb, DN_WIDTH), BF16),
            pltpu.VMEM((tb, DN_WIDTH), BF16),
            pltpu.VMEM((tb, DN_HEADS * DN_CHUNK), BF16),
            pltpu.VMEM((tb // DN_CHUNK, 1, DN_WIDTH), F32),
            pltpu.VMEM((DN_HEADS, DN_HEAD_DIM, DN_HEAD_DIM), F32),
        ],
        compiler_params=pltpu.CompilerParams(
            dimension_semantics=("parallel", "arbitrary"), vmem_limit_bytes=VMEM_LIMIT),
        name="gated_deltanet",
    )(proj3, proj3, proj3, proj3, ba3, conv_w, gate_params, norm_w)


def _rope_tables(positions):
    half = ATT_QK_DIM // 2
    inv_freq = ROPE_THETA ** (-jnp.arange(0, ATT_QK_DIM, 2, dtype=F32) / ATT_QK_DIM)
    reps = LANES // half
    freq_lanes = jnp.tile(inv_freq, reps)
    sign_lanes = jnp.tile(jnp.repeat(jnp.array([-1.0, 1.0], F32), half), reps // 2)
    ang = positions.astype(F32)[..., None] * freq_lanes
    return jnp.cos(ang), jnp.sin(ang) * sign_lanes


def kernel(x, positions, norm_w, w_in, lam_qk, attn_subln_w, w_attn_out, conv_dw_w, conv_dw_b,
           conv_ln_w, conv_ln_b, w_conv_out, dn_conv_w, dn_a_log, dn_dt_bias, dn_norm_w,
           w_dn_out, w_out, final_norm_w):
    B, L, _ = x.shape
    T = B * L
    x2d = x.reshape(T, D_MODEL)
    depth = norm_w.shape[0]
    cosf, sinf = _rope_tables(positions)
    pad_heads = lambda v: jnp.pad(v, (DN_A_LANE, LANES - DN_A_LANE - DN_HEADS))
    w_all = w_in.astype(BF16)
    w_gate = w_all[:, :, ORIG_GATE:]
    w_ba = jnp.pad(w_all[:, :, ORIG_DB:ORIG_GATE], ((0, 0), (0, 0), (0, LANES - 2 * DN_HEADS)))
    wa, wc, wd, wo = (w.astype(BF16) for w in (w_attn_out, w_conv_out, w_dn_out, w_out))
    for l in range(depth):
        proj, ba = _inproj(x2d, norm_w[l][None, :], w_all, w_gate, w_ba, l)
        proj3 = proj.reshape(B, L, N_MAIN)
        lambda_init = 0.8 - 0.6 * math.exp(-0.3 * l)
        ya = _attention(proj3, cosf, sinf, lam_qk[l], attn_subln_w[l][:, None],
                        lambda_init=lambda_init).reshape(T, ATT_HEADS * ATT_V_DIM)
        yc = _conv_branch(proj3, conv_dw_w[l], conv_dw_b[l][None, :], conv_ln_w[l][None, :],
                          conv_ln_b[l][None, :]).reshape(T, CONV_WIDTH)
        gate_params = jnp.stack([pad_heads(dn_a_log[l]), pad_heads(dn_dt_bias[l])])
        yd = _deltanet(proj3, ba.reshape(B, L, LANES), dn_conv_w[l], gate_params,
                       dn_norm_w[l][None, :]).reshape(T, DN_WIDTH)
        x2d = _merge(x2d, ya, yc, yd, proj, wa, wc, wd, wo, final_norm_w[None, :], l,
                     final_norm=(l == depth - 1))
    return x2d.reshape(B, L, D_MODEL)
```

```python
import functools
import math

import jax
import jax.numpy as jnp
from jax import lax
from jax.experimental import pallas as pl
from jax.experimental.pallas import tpu as pltpu

F32 = jnp.float32
BF16 = jnp.bfloat16

D_MODEL = 1024
ATT_HEADS = 8
ATT_QK_DIM = 64
ATT_V_DIM = 128
ROPE_THETA = 10000.0
CONV_WIDTH = 1024
CONV_KERNEL = 31
DN_HEADS = 8
DN_HEAD_DIM = 128
DN_WIDTH = DN_HEADS * DN_HEAD_DIM
DN_SHORT_CONV = 4
DN_CHUNK = 64
RMS_EPS = 1e-6
LN_EPS = 1e-5
L2_EPS = 1e-6

LANES = 128
SUBLANES = 8
VMEM_LIMIT = 56 * 1024 * 1024
COL_AQ, COL_AK, COL_AV, COL_AZ = 0, 1, 2, 3
COL_CA, COL_CG, COL_CZ = 4, 5, 6
COL_DQ, COL_DK, COL_DV, COL_DZ = 7, 8, 9, 10
COL_GATE = 11
N_LEAD_TILES = 11
N_MAIN = 14 * D_MODEL
ORIG_DB = N_LEAD_TILES * D_MODEL
ORIG_GATE = ORIG_DB + 2 * DN_HEADS


def _silu(x):
    return x * jax.nn.sigmoid(x)


def _merge_body(x_ref, a_ref, c_ref, d_ref, ga_ref, gc_ref, gd_ref,
                wa_ref, wc_ref, wd_ref, wo_ref, fw_ref, o_ref, *, final_norm):
    ya = jnp.dot(a_ref[...], wa_ref[...], preferred_element_type=F32)
    yc = jnp.dot(c_ref[...], wc_ref[...], preferred_element_type=F32)
    yd = jnp.dot(d_ref[...], wd_ref[...], preferred_element_type=F32)
    merged = (jax.nn.sigmoid(ga_ref[...].astype(F32)) * ya
              + jax.nn.sigmoid(gc_ref[...].astype(F32)) * yc
              + jax.nn.sigmoid(gd_ref[...].astype(F32)) * yd)
    y = x_ref[...] + jnp.dot(merged.astype(BF16), wo_ref[...], preferred_element_type=F32)
    if final_norm:
        ms = jnp.mean(y * y, axis=-1, keepdims=True)
        y = y * lax.rsqrt(ms + RMS_EPS) * fw_ref[...]
    o_ref[...] = y


def _merge(x2d, ya, yc, yd, proj, wa, wc, wd, wo, final_w, layer, *, final_norm, tm=512):
    T = x2d.shape[0]
    row = lambda i: (i, 0)
    const = lambda i: (0, 0)
    wspec = pl.BlockSpec((None, D_MODEL, D_MODEL), lambda i: (layer, 0, 0))
    return pl.pallas_call(
        functools.partial(_merge_body, final_norm=final_norm),
        grid=(T // tm,),
        in_specs=[
            pl.BlockSpec((tm, D_MODEL), row),
            pl.BlockSpec((tm, D_MODEL), row),
            pl.BlockSpec((tm, D_MODEL), row),
            pl.BlockSpec((tm, D_MODEL), row),
            pl.BlockSpec((tm, D_MODEL), lambda i: (i, COL_GATE)),
            pl.BlockSpec((tm, D_MODEL), lambda i: (i, COL_GATE + 1)),
            pl.BlockSpec((tm, D_MODEL), lambda i: (i, COL_GATE + 2)),
            wspec, wspec, wspec, wspec,
            pl.BlockSpec((1, D_MODEL), const),
        ],
        out_specs=pl.BlockSpec((tm, D_MODEL), row),
        out_shape=jax.ShapeDtypeStruct((T, D_MODEL), F32),
        compiler_params=pltpu.CompilerParams(
            dimension_semantics=("parallel",), vmem_limit_bytes=VMEM_LIMIT),
        name="merge",
    )(x2d, ya, yc, yd, proj, proj, proj, wa, wc, wd, wo, final_w)


CONV_HALO = 32


def _causal_taps(buf_ref, w_ref, r0, rows, lanes, n_taps, halo):
    assert halo >= SUBLANES * ((n_taps - 1) // SUBLANES + 1)
    acc = None
    for r in range(min(SUBLANES, n_taps)):
        part = None
        for a in range((n_taps - 1 - r) // SUBLANES + 1):
            s = SUBLANES * a + r
            start = r0 + (halo - SUBLANES * (a + 1))
            if not isinstance(start, int):
                start = pl.multiple_of(start, SUBLANES)
            term = (w_ref[n_taps - 1 - s:n_taps - s, lanes]
                    * buf_ref[pl.ds(start, rows + SUBLANES), lanes])
            part = term if part is None else part + term
        if r:
            part = pltpu.roll(part, r, axis=0)
        acc = part if acc is None else acc + part
    return acc[SUBLANES:, :]


def _inproj_body(x_ref, nw_ref, w_ref, wg_ref, wba_ref, o_ref, ba_ref, h_ref):
    j = pl.program_id(1)

    @pl.when(j == 0)
    def _():
        x = x_ref[...]
        ms = jnp.mean(x * x, axis=-1, keepdims=True)
        h = (x * lax.rsqrt(ms + RMS_EPS) * nw_ref[...]).astype(BF16)
        h_ref[...] = h
        ba_ref[...] = jnp.dot(h, wba_ref[...], preferred_element_type=F32)

    @pl.when(j < N_LEAD_TILES)
    def _():
        o_ref[...] = jnp.dot(h_ref[...], w_ref[...], preferred_element_type=F32).astype(BF16)

    @pl.when(j >= N_LEAD_TILES)
    def _():
        o_ref[...] = jnp.dot(h_ref[...], wg_ref[...], preferred_element_type=F32).astype(BF16)


def _inproj(x2d, norm_w, w_all, w_gate, w_ba, layer, *, tm=2048):
    T = x2d.shape[0]
    tn = D_MODEL
    const = lambda i, j: (0, 0)
    return pl.pallas_call(
        _inproj_body,
        grid=(T // tm, N_MAIN // tn),
        in_specs=[
            pl.BlockSpec((tm, D_MODEL), lambda i, j: (i, 0)),
            pl.BlockSpec((1, D_MODEL), const),
            pl.BlockSpec((None, D_MODEL, tn),
                         lambda i, j: (layer, 0, jnp.minimum(j, N_LEAD_TILES - 1))),
            pl.BlockSpec((None, D_MODEL, tn),
                         lambda i, j: (layer, 0, jnp.maximum(j - N_LEAD_TILES, 0))),
            pl.BlockSpec((None, D_MODEL, LANES), lambda i, j: (layer, 0, 0)),
        ],
        out_specs=[
            pl.BlockSpec((tm, tn), lambda i, j: (i, j)),
            pl.BlockSpec((tm, LANES), lambda i, j: (i, 0)),
        ],
        out_shape=[
            jax.ShapeDtypeStruct((T, N_MAIN), BF16),
            jax.ShapeDtypeStruct((T, LANES), F32),
        ],
        scratch_shapes=[pltpu.VMEM((tm, D_MODEL), BF16)],
        compiler_params=pltpu.CompilerParams(
            dimension_semantics=("parallel", "arbitrary"), vmem_limit_bytes=VMEM_LIMIT),
        name="inproj",
    )(x2d, norm_w, w_all, w_gate, w_ba)


CONV_ROWS = 128
CONV_STRIP = 128


def _conv_body(a_ref, g_ref, z_ref, w_ref, b_ref, lnw_ref, lnb_ref, o_ref, ubuf, cbuf, *, tl):
    @pl.when(pl.program_id(1) == 0)
    def _():
        ubuf[0:CONV_HALO, :] = jnp.zeros((CONV_HALO, CONV_WIDTH), F32)

    a = a_ref[...].astype(F32)
    g = g_ref[...].astype(F32)
    ubuf[CONV_HALO:CONV_HALO + tl, :] = a * (0.5 * jnp.tanh(0.5 * g) + 0.5)

    bias = b_ref[...]
    lnw = lnw_ref[...]
    lnb = lnb_ref[...]

    def step(r, carry):
        r0 = pl.multiple_of(r * CONV_ROWS, CONV_ROWS)
        for c in range(CONV_WIDTH // CONV_STRIP):
            lanes = slice(c * CONV_STRIP, (c + 1) * CONV_STRIP)
            cbuf[:, lanes] = _causal_taps(ubuf, w_ref, r0, CONV_ROWS, lanes, CONV_KERNEL, CONV_HALO)
        u = cbuf[...] + bias
        mu = jnp.mean(u, axis=-1, keepdims=True)
        uc = u - mu
        var = jnp.mean(uc * uc, axis=-1, keepdims=True)
        y = uc * lax.rsqrt(var + LN_EPS) * lnw + lnb
        zz = z_ref[pl.ds(r0, CONV_ROWS), :].astype(F32)
        o_ref[pl.ds(r0, CONV_ROWS), :] = (_silu(y) * _silu(zz)).astype(BF16)
        return carry

    lax.fori_loop(0, tl // CONV_ROWS, step, 0)
    ubuf[0:CONV_HALO, :] = ubuf[tl:tl + CONV_HALO, :]


def _conv_branch(proj3, dw_w, dw_b, ln_w, ln_b, *, tl=512):
    B, L, _ = proj3.shape
    vec = pl.BlockSpec((1, CONV_WIDTH), lambda b, t: (0, 0))
    return pl.pallas_call(
        functools.partial(_conv_body, tl=tl),
        grid=(B, L // tl),
        in_specs=[
            pl.BlockSpec((None, tl, CONV_WIDTH), lambda b, t: (b, t, COL_CA)),
            pl.BlockSpec((None, tl, CONV_WIDTH), lambda b, t: (b, t, COL_CG)),
            pl.BlockSpec((None, tl, CONV_WIDTH), lambda b, t: (b, t, COL_CZ)),
            pl.BlockSpec((CONV_KERNEL, CONV_WIDTH), lambda b, t: (0, 0)),
            vec, vec, vec,
        ],
        out_specs=pl.BlockSpec((None, tl, CONV_WIDTH), lambda b, t: (b, t, 0)),
        out_shape=jax.ShapeDtypeStruct((B, L, CONV_WIDTH), BF16),
        scratch_shapes=[pltpu.VMEM((CONV_HALO + tl, CONV_WIDTH), F32),
                        pltpu.VMEM((CONV_ROWS, CONV_WIDTH), F32)],
        compiler_params=pltpu.CompilerParams(
            dimension_semantics=("parallel", "arbitrary"), vmem_limit_bytes=VMEM_LIMIT),
        name="conv_branch",
    )(proj3, proj3, proj3, dw_w, dw_b, ln_w, ln_b)


def _nt_dot(a, b):
    return lax.dot_general(a, b, (((1,), (1,)), ((), ())), preferred_element_type=F32)


def _rope(t, cos, sin_signed):
    lane = lax.broadcasted_iota(jnp.int32, t.shape, 1)
    first_half = (lane % ATT_QK_DIM) < (ATT_QK_DIM // 2)
    half = ATT_QK_DIM // 2
    swapped = jnp.where(first_half, pltpu.roll(t, t.shape[1] - half, axis=1),
                        pltpu.roll(t, half, axis=1))
    return t * cos + swapped * sin_signed


ATT_HPS = 2
ATT_ZERO_SLOT = 2
ATT_ONES_ROWS = 16


def _attn_body(q_ref, k_ref, v_ref, z_ref, cos_ref, sin_ref, lam_ref, sw_ref, o_ref,
               kr_ref, vt_ref, m_ref, alpha_ref, acc_ref, s_ref, p_ref, qs_ref, *, lambda_init, tq):
    heads = range(ATT_HPS)
    hl = [slice(h * LANES, (h + 1) * LANES) for h in heads]
    n_blocks = vt_ref.shape[1]

    cos_k = jnp.concatenate([cos_ref[...]] * ATT_HPS, axis=1)
    sin_k = jnp.concatenate([sin_ref[...]] * ATT_HPS, axis=1)
    kr_ref[...] = _rope(k_ref[...].astype(F32), cos_k, sin_k).astype(BF16)

    ri = lax.broadcasted_iota(jnp.int32, (ATT_V_DIM, ATT_V_DIM), 0)
    ci = lax.broadcasted_iota(jnp.int32, (ATT_V_DIM, ATT_V_DIM), 1)
    ident = (ri == ci).astype(BF16)
    for jb in range(n_blocks):
        for h in heads:
            vt_ref[h, jb, 0:ATT_V_DIM, :] = _nt_dot(
                ident, v_ref[jb * tq:(jb + 1) * tq, hl[h]]).astype(BF16)
            vt_ref[h, jb, ATT_V_DIM:, :] = jnp.ones((ATT_ONES_ROWS, tq), BF16)
    for h in heads:
        p_ref[h, ATT_ZERO_SLOT] = jnp.zeros(p_ref.shape[2:], BF16)

    lq = lam_ref[...]
    lam = (jnp.exp(jnp.sum(lq[0:1, :] * lq[1:2, :], axis=-1, keepdims=True))
           - jnp.exp(jnp.sum(lq[2:3, :] * lq[3:4, :], axis=-1, keepdims=True)) + lambda_init)

    def block_rows(qi):
        return pl.ds(pl.multiple_of(qi * tq, tq), tq)

    def scores(h, j):
        return _nt_dot(kr_ref[block_rows(j), hl[h]], qs_ref[:, hl[h]])

    def prologue(qi):
        rows = block_rows(qi)
        cos_q = jnp.concatenate([cos_ref[rows, :]] * ATT_HPS, axis=1)
        sin_q = jnp.concatenate([sin_ref[rows, :]] * ATT_HPS, axis=1)
        q = (_rope(q_ref[rows, :].astype(F32), cos_q, sin_q)
             * (ATT_QK_DIM ** -0.5 * math.log2(math.e)))
        lane = lax.broadcasted_iota(jnp.int32, q.shape, 1) % LANES
        qs_ref[...] = jnp.concatenate([jnp.where(lane < ATT_QK_DIM, q, 0.0),
                                       jnp.where(lane >= ATT_QK_DIM, q, 0.0)], axis=0).astype(BF16)
        m_ref[...] = jnp.full(m_ref.shape, -jnp.inf, F32)
        acc_ref[...] = jnp.zeros(acc_ref.shape, F32)
        alpha_ref[...] = jnp.ones(alpha_ref.shape, F32)
        for h in heads:
            s_ref[h, 0] = scores(h, 0)

    def softmax_step(h, s):
        m_prev = m_ref[h]
        m_new = jnp.maximum(m_prev, jnp.max(s, axis=0, keepdims=True))
        alpha = jnp.exp2(m_prev - m_new)
        p = jnp.exp2(s - m_new)
        m_ref[h] = m_new
        return alpha, p.astype(BF16)

    def pv(h, j, p):
        return jnp.dot(vt_ref[h, j], p, preferred_element_type=F32)

    def prev_slot(t):
        return jnp.where(t == 0, ATT_ZERO_SLOT, 1 - t % 2)

    def trip(t, carry):
        slot = t % 2
        pv_prev = [pv(h, jnp.maximum(t - 1, 0), p_ref[h, prev_slot(t)]) for h in heads]
        s_next = [scores(h, t + 1) for h in heads]
        for h in heads:
            alpha, p = softmax_step(h, s_ref[h, slot])
            p_ref[h, slot] = p
            acc_ref[h] = alpha_ref[h] * acc_ref[h] + pv_prev[h]
            alpha_ref[h] = alpha
        for h in heads:
            s_ref[h, 1 - slot] = s_next[h]
        return carry

    def trip_pair(g, carry):
        t = 2 * g
        pv_a = [pv(h, jnp.maximum(t - 1, 0), p_ref[h, prev_slot(t)]) for h in heads]
        s_a = [scores(h, t + 1) for h in heads]
        s_b = [scores(h, t + 2) for h in heads]
        p_a = []
        for h in heads:
            alpha, p = softmax_step(h, s_ref[h, 0])
            p_a.append(p)
            acc_ref[h] = alpha_ref[h] * acc_ref[h] + pv_a[h]
            alpha_ref[h] = alpha
        pv_b = [pv(h, t, p_a[h]) for h in heads]
        for h in heads:
            alpha, p = softmax_step(h, s_a[h])
            p_ref[h, 1] = p
            acc_ref[h] = alpha_ref[h] * acc_ref[h] + pv_b[h]
            alpha_ref[h] = alpha
        for h in heads:
            s_ref[h, 0] = s_b[h]
        return carry

    def epilogue(qi):
        slot = qi % 2
        pv_prev = [pv(h, jnp.maximum(qi - 1, 0), p_ref[h, prev_slot(qi)]) for h in heads]
        key = lax.broadcasted_iota(jnp.int32, (tq, 2 * tq), 0)
        qry = lax.broadcasted_iota(jnp.int32, (tq, 2 * tq), 1) % tq
        last = [softmax_step(h, jnp.where(key <= qry, s_ref[h, slot], -jnp.inf)) for h in heads]
        pv_last = [pv(h, qi, last[h][1]) for h in heads]
        outs = []
        for h in heads:
            acc = last[h][0] * (alpha_ref[h] * acc_ref[h] + pv_prev[h]) + pv_last[h]
            o_all = acc[:ATT_V_DIM, :] / acc[ATT_V_DIM:ATT_V_DIM + 1, :]
            ot = o_all[:, :tq] - lam * o_all[:, tq:]
            ms = jnp.mean(ot * ot, axis=0, keepdims=True)
            outs.append((ot * lax.rsqrt(ms + RMS_EPS) * sw_ref[...] * (1.0 - lambda_init)).T)
        return jnp.concatenate(outs, axis=1)

    prologue(0)

    def q_block(qi, carry):
        lax.fori_loop(0, qi // 2, trip_pair, 0)

        @pl.when(qi % 2 == 1)
        def _():
            trip(qi - 1, 0)

        on = epilogue(qi)
        prologue(jnp.minimum(qi + 1, n_blocks - 1))
        rows = block_rows(qi)
        o_ref[rows, :] = (on * _silu(z_ref[rows, :].astype(F32))).astype(BF16)
        return carry

    lax.fori_loop(0, n_blocks, q_block, 0)


def _attention(proj3, cosf, sinf, lam_qk, subln_w, *, lambda_init, tq=256):
    B, L, _ = proj3.shape
    hw = ATT_HPS * LANES
    hpb = D_MODEL // hw
    full = lambda col: pl.BlockSpec((None, L, hw), lambda b, h: (b, 0, col * hpb + h))
    table = pl.BlockSpec((None, L, LANES), lambda b, h: (b, 0, 0))
    return pl.pallas_call(
        functools.partial(_attn_body, lambda_init=lambda_init, tq=tq),
        grid=(B, ATT_HEADS // ATT_HPS),
        in_specs=[
            full(COL_AQ), full(COL_AK), full(COL_AV), full(COL_AZ), table, table,
            pl.BlockSpec((4, ATT_QK_DIM), lambda b, h: (0, 0)),
            pl.BlockSpec((ATT_V_DIM, 1), lambda b, h: (0, 0)),
        ],
        out_specs=pl.BlockSpec((None, L, hw), lambda b, h: (b, 0, h)),
        out_shape=jax.ShapeDtypeStruct((B, L, ATT_HEADS * ATT_V_DIM), BF16),
        scratch_shapes=[
            pltpu.VMEM((L, hw), BF16),
            pltpu.VMEM((ATT_HPS, L // tq, ATT_V_DIM + ATT_ONES_ROWS, tq), BF16),
            pltpu.VMEM((ATT_HPS, 1, 2 * tq), F32),
            pltpu.VMEM((ATT_HPS, 1, 2 * tq), F32),
            pltpu.VMEM((ATT_HPS, ATT_V_DIM + ATT_ONES_ROWS, 2 * tq), F32),
            pltpu.VMEM((ATT_HPS, 2, tq, 2 * tq), F32),
            pltpu.VMEM((ATT_HPS, 3, tq, 2 * tq), BF16),
            pltpu.VMEM((2 * tq, hw), BF16),
        ],
        compiler_params=pltpu.CompilerParams(
            dimension_semantics=("parallel", "parallel"), vmem_limit_bytes=VMEM_LIMIT),
        name="diff_attention",
    )(proj3, proj3, proj3, proj3, cosf, sinf, lam_qk, subln_w)


DN_HALO = 8
DN_STRIP = 128
DN_CONV_ROWS = 128
DN_B_LANE = 0
DN_A_LANE = DN_HEADS
DN_ONE_LANE = DN_A_LANE + 3 * DN_HEADS
DN_GROUP = 4


def _split3(x):
    hi = x.astype(BF16).astype(F32)
    r1 = x - hi
    mid = r1.astype(BF16).astype(F32)
    lo = (r1 - mid).astype(BF16).astype(F32)
    return hi, mid, lo


def _dn_body(q_ref, k_ref, v_ref, z_ref, ba_ref, cw_ref, gp_ref, nw_ref, o_ref,
             xbuf, qkv, u_s, w_s, qg_s, kd_s, in_s, dec_s, s_ref, *, tb):
    C = DN_CHUNK
    W = DN_WIDTH
    hd = DN_HEAD_DIM

    @pl.when(pl.program_id(1) == 0)
    def _():
        xbuf[0:DN_HALO, :] = jnp.zeros((DN_HALO, 3 * W), F32)
        s_ref[...] = jnp.zeros(s_ref.shape, F32)

    xbuf[DN_HALO:DN_HALO + tb, 0:W] = q_ref[...].astype(F32)
    xbuf[DN_HALO:DN_HALO + tb, W:2 * W] = k_ref[...].astype(F32)
    xbuf[DN_HALO:DN_HALO + tb, 2 * W:3 * W] = v_ref[...].astype(F32)

    def conv_step(i, carry):
        r0 = pl.multiple_of(i * DN_CONV_ROWS, DN_CONV_ROWS)
        for c in range(3 * W // DN_STRIP):
            lanes = slice(c * DN_STRIP, (c + 1) * DN_STRIP)
            qkv[pl.ds(r0, DN_CONV_ROWS), lanes] = _silu(
                _causal_taps(xbuf, cw_ref, r0, DN_CONV_ROWS, lanes, DN_SHORT_CONV, DN_HALO))
        return carry

    lax.fori_loop(0, tb // DN_CONV_ROWS, conv_step, 0)
    xbuf[0:DN_HALO, :] = xbuf[tb:tb + DN_HALO, :]

    n_chunks = tb // C
    G = DN_GROUP
    n_groups = DN_HEADS // G
    gw = G * hd
    gc_w = G * C

    row = lax.broadcasted_iota(jnp.int32, (C, C), 0)
    col = lax.broadcasted_iota(jnp.int32, (C, C), 1)
    tri = (row >= col).astype(BF16)
    lane = lax.broadcasted_iota(jnp.int32, (C, LANES), 1)
    ri = lax.broadcasted_iota(jnp.int32, (C, gc_w), 0)
    cj = lax.broadcasted_iota(jnp.int32, (C, gc_w), 1) % C
    eye_cat = (ri == cj).astype(F32)
    bd_sq = (lax.broadcasted_iota(jnp.int32, (gc_w, gc_w), 0) // C
             == lax.broadcasted_iota(jnp.int32, (gc_w, gc_w), 1) // C)
    bd_wide = (lax.broadcasted_iota(jnp.int32, (gc_w, gw), 0) // C
               == lax.broadcasted_iota(jnp.int32, (gc_w, gw), 1) // hd)
    neg_a = -jnp.exp(gp_ref[0:1, :])
    dt_bias = gp_ref[1:2, :]
    norm_w = nw_ref[...]

    def block_diag(x, mask):
        return jnp.where(mask, jnp.tile(x, (G, 1)), 0.0).astype(BF16)

    def bdot(a, b):
        return jnp.dot(a, b, preferred_element_type=F32)

    probs = [(ci, g) for ci in range(n_chunks) for g in range(n_groups)]

    gd_lhs, gcb_all, kq_lhs, kn_grp, vb_grp, kbe_grp = {}, {}, {}, {}, {}, {}
    for ci in range(n_chunks):
        r0 = ci * C
        ba = ba_ref[r0:r0 + C, :]
        beta_all = jax.nn.sigmoid(ba)
        xa = ba + dt_bias
        softplus = jnp.maximum(xa, 0.0) + jnp.log1p(jnp.exp(-jnp.abs(xa)))
        g = jnp.where((lane >= DN_A_LANE) & (lane < DN_A_LANE + DN_HEADS), neg_a * softplus, 0.0)
        g_hi, g_mid, g_lo = _split3(g)
        gc = (bdot(tri, g_hi.astype(BF16)) + bdot(tri, g_mid.astype(BF16))
              + bdot(tri, g_lo.astype(BF16)))
        c_hi, c_mid, c_lo = _split3(gc)
        gd_lhs[ci] = (c_hi + pltpu.roll(c_mid, DN_HEADS, axis=1) + pltpu.roll(c_lo, 2 * DN_HEADS, axis=1)
                      + jnp.where((lane >= DN_ONE_LANE) & (lane < DN_ONE_LANE + 3), 1.0, 0.0)).astype(BF16)

        beta_b = jnp.concatenate(
            [jnp.broadcast_to(beta_all[:, DN_B_LANE + h:DN_B_LANE + h + 1], (C, hd))
             for h in range(DN_HEADS)], axis=1)
        gcb = jnp.concatenate(
            [jnp.broadcast_to(gc[:, DN_A_LANE + h:DN_A_LANE + h + 1], (C, hd))
             for h in range(DN_HEADS)], axis=1)
        gcb_all[ci] = gcb
        eg = jnp.exp(gcb)
        glast = gcb[C - 1:C, :]
        dec_s[ci] = jnp.exp(glast)

        q_all = qkv[r0:r0 + C, 0:W]
        k_all = qkv[r0:r0 + C, W:2 * W]
        v_all = qkv[r0:r0 + C, 2 * W:3 * W]
        qn = jnp.concatenate(
            [q_all[:, h * hd:(h + 1) * hd] * lax.rsqrt(jnp.sum(
                q_all[:, h * hd:(h + 1) * hd] ** 2, axis=-1, keepdims=True) + L2_EPS)
             for h in range(DN_HEADS)], axis=1) * (hd ** -0.5)
        kn = jnp.concatenate(
            [k_all[:, h * hd:(h + 1) * hd] * lax.rsqrt(jnp.sum(
                k_all[:, h * hd:(h + 1) * hd] ** 2, axis=-1, keepdims=True) + L2_EPS)
             for h in range(DN_HEADS)], axis=1)
        kb = kn * beta_b
        qg_s[r0:r0 + C, :] = (qn * eg).astype(BF16)
        kd_s[r0:r0 + C, :] = (kn * jnp.exp(glast - gcb)).astype(BF16)
        for gi in range(n_groups):
            gl = slice(gi * gw, (gi + 1) * gw)
            kq_lhs[ci, gi] = jnp.concatenate([kb[:, gl], qn[:, gl]], axis=0).astype(BF16)
            kn_grp[ci, gi] = kn[:, gl]
            vb_grp[ci, gi] = (v_all * beta_b)[:, gl]
            kbe_grp[ci, gi] = (kb * eg)[:, gl]

    gdiff, kq = {}, {}
    for ci, gi in probs:
        blocks = []
        for h in range(gi * G, (gi + 1) * G):
            p_hi, p_mid, p_lo = _split3(gcb_all[ci][:, h * hd:(h + 1) * hd])
            onehot = ((lane == DN_A_LANE + h) | (lane == DN_A_LANE + DN_HEADS + h)
                      | (lane == DN_A_LANE + 2 * DN_HEADS + h))
            blocks.append(jnp.where(lane == DN_ONE_LANE, -p_hi, jnp.where(
                lane == DN_ONE_LANE + 1, -p_mid, jnp.where(
                    lane == DN_ONE_LANE + 2, -p_lo, jnp.where(onehot, 1.0, 0.0)))))
        gd_rhs = jnp.concatenate(blocks, axis=0).astype(BF16)
        gdiff[ci, gi] = _nt_dot(gd_lhs[ci], gd_rhs)
        kq[ci, gi] = _nt_dot(kq_lhs[ci, gi], block_diag(kn_grp[ci, gi], bd_wide))

    m, p = {}, {}
    for pr in probs:
        decay = jnp.exp(jnp.where(ri >= cj, gdiff[pr], -jnp.inf))
        m[pr] = -jnp.where(ri > cj, kq[pr][:C, :] * decay, 0.0)
        ci, gi = pr
        in_s[ci * C:(ci + 1) * C, gi * gc_w:(gi + 1) * gc_w] = (kq[pr][C:, :] * decay).astype(BF16)
        p[pr] = eye_cat + m[pr]

    n_levels = int(math.log2(C)) - 1
    for pr in probs:
        m[pr] = bdot(m[pr].astype(BF16), block_diag(m[pr], bd_sq))
    for lvl in range(n_levels):
        last = lvl == n_levels - 1
        res = {}
        for pr in probs:
            lhs = p[pr] if last else jnp.concatenate([p[pr], m[pr]], axis=0)
            res[pr] = bdot(lhs.astype(BF16), block_diag(m[pr], bd_sq))
        for pr in probs:
            p[pr] = p[pr] + res[pr][:C, :]
            if not last:
                m[pr] = res[pr][C:, :]

    for pr in probs:
        ci, gi = pr
        pb = p[pr].astype(BF16)
        rows = slice(ci * C, (ci + 1) * C)
        gl = slice(gi * gw, (gi + 1) * gw)
        for hg in range(G):
            hs = slice(hg * hd, (hg + 1) * hd)
            rhs = jnp.concatenate([vb_grp[pr][:, hs], kbe_grp[pr][:, hs]], axis=1).astype(BF16)
            sol = bdot(pb[:, hg * C:(hg + 1) * C], rhs)
            lanes = slice(gi * gw + hg * hd, gi * gw + (hg + 1) * hd)
            u_s[rows, lanes] = sol[:, :hd]
            w_s[rows, lanes] = sol[:, hd:].astype(BF16)

    def chunk_step(ci, carry):
        r0 = pl.multiple_of(ci * C, C)
        rows = pl.ds(r0, C)
        sb = [s_ref[h].astype(BF16) for h in range(DN_HEADS)]
        ws = [bdot(jnp.concatenate([w_s[rows, h * hd:(h + 1) * hd], qg_s[rows, h * hd:(h + 1) * hd]],
                                   axis=0), sb[h]) for h in range(DN_HEADS)]
        v_new = u_s[rows, :] - jnp.concatenate([x[:C, :] for x in ws], axis=1)
        vnb = v_new.astype(BF16)
        o_intra = [bdot(in_s[rows, gi * gc_w:(gi + 1) * gc_w],
                        block_diag(v_new[:, gi * gw:(gi + 1) * gw], bd_wide))
                   for gi in range(n_groups)]
        dec = dec_s[ci]
        for h in range(DN_HEADS):
            hl = slice(h * hd, (h + 1) * hd)
            s_ref[h] = s_ref[h] * dec[:, hl] + lax.dot_general(
                kd_s[rows, hl], vnb[:, hl], (((0,), (0,)), ((), ())), preferred_element_type=F32)
        o = jnp.concatenate([x[C:, :] for x in ws], axis=1) + jnp.concatenate(o_intra, axis=1)
        for h in range(DN_HEADS):
            hl = slice(h * hd, (h + 1) * hd)
            oh = o[:, hl]
            ms = jnp.mean(oh * oh, axis=-1, keepdims=True)
            on = oh * lax.rsqrt(ms + RMS_EPS) * norm_w
            o_ref[rows, hl] = (on * _silu(z_ref[rows, hl].astype(F32))).astype(BF16)
        return carry

    lax.fori_loop(0, n_chunks, chunk_step, 0)


def _deltanet(proj3, ba3, conv_w, gate_params, norm_w, *, tb=256):
    B, L, _ = proj3.shape
    blk = lambda col: pl.BlockSpec((None, tb, DN_WIDTH), lambda b, t: (b, t, col))
    return pl.pallas_call(
        functools.partial(_dn_body, tb=tb),
        grid=(B, L // tb),
        in_specs=[
            blk(COL_DQ), blk(COL_DK), blk(COL_DV), blk(COL_DZ),
            pl.BlockSpec((None, tb, LANES), lambda b, t: (b, t, 0)),
            pl.BlockSpec((DN_SHORT_CONV, 3 * DN_WIDTH), lambda b, t: (0, 0)),
            pl.BlockSpec((2, LANES), lambda b, t: (0, 0)),
            pl.BlockSpec((1, DN_HEAD_DIM), lambda b, t: (0, 0)),
        ],
        out_specs=pl.BlockSpec((None, tb, DN_WIDTH), lambda b, t: (b, t, 0)),
        out_shape=jax.ShapeDtypeStruct((B, L, DN_WIDTH), BF16),
        scratch_shapes=[
            pltpu.VMEM((DN_HALO + tb, 3 * DN_WIDTH), F32),
            pltpu.VMEM((tb, 3 * DN_WIDTH), F32),
            pltpu.VMEM((tb, DN_WIDTH), F32),
            pltpu.VMEM((tb, DN_WIDTH), BF16),
            pltpu.VMEM((tb, DN_WIDTH), BF16),
            pltpu.VMEM((tb, DN_WIDTH), BF16),
            pltpu.VMEM((tb, DN_HEADS * DN_CHUNK), BF16),
            pltpu.VMEM((tb // DN_CHUNK, 1, DN_WIDTH), F32),
            pltpu.VMEM((DN_HEADS, DN_HEAD_DIM, DN_HEAD_DIM), F32),
        ],
        compiler_params=pltpu.CompilerParams(
            dimension_semantics=("parallel", "arbitrary"), vmem_limit_bytes=VMEM_LIMIT),
        name="gated_deltanet",
    )(proj3, proj3, proj3, proj3, ba3, conv_w, gate_params, norm_w)


def _rope_tables(positions):
    half = ATT_QK_DIM // 2
    inv_freq = ROPE_THETA ** (-jnp.arange(0, ATT_QK_DIM, 2, dtype=F32) / ATT_QK_DIM)
    reps = LANES // half
    freq_lanes = jnp.tile(inv_freq, reps)
    sign_lanes = jnp.tile(jnp.repeat(jnp.array([-1.0, 1.0], F32), half), reps // 2)
    ang = positions.astype(F32)[..., None] * freq_lanes
    return jnp.cos(ang), jnp.sin(ang) * sign_lanes


def kernel(x, positions, norm_w, w_in, lam_qk, attn_subln_w, w_attn_out, conv_dw_w, conv_dw_b,
           conv_ln_w, conv_ln_b, w_conv_out, dn_conv_w, dn_a_log, dn_dt_bias, dn_norm_w,
           w_dn_out, w_out, final_norm_w):
    B, L, _ = x.shape
    T = B * L
    x2d = x.reshape(T, D_MODEL)
    depth = norm_w.shape[0]
    cosf, sinf = _rope_tables(positions)
    pad_heads = lambda v: jnp.pad(v, (DN_A_LANE, LANES - DN_A_LANE - DN_HEADS))
    w_all = w_in.astype(BF16)
    w_gate = w_all[:, :, ORIG_GATE:]
    w_ba = jnp.pad(w_all[:, :, ORIG_DB:ORIG_GATE], ((0, 0), (0, 0), (0, LANES - 2 * DN_HEADS)))
    wa, wc, wd, wo = (w.astype(BF16) for w in (w_attn_out, w_conv_out, w_dn_out, w_out))
    for l in range(depth):
        proj, ba = _inproj(x2d, norm_w[l][None, :], w_all, w_gate, w_ba, l)
        proj3 = proj.reshape(B, L, N_MAIN)
        lambda_init = 0.8 - 0.6 * math.exp(-0.3 * l)
        ya = _attention(proj3, cosf, sinf, lam_qk[l], attn_subln_w[l][:, None],
                        lambda_init=lambda_init).reshape(T, ATT_HEADS * ATT_V_DIM)
        yc = _conv_branch(proj3, conv_dw_w[l], conv_dw_b[l][None, :], conv_ln_w[l][None, :],
                          conv_ln_b[l][None, :]).reshape(T, CONV_WIDTH)
        gate_params = jnp.stack([pad_heads(dn_a_log[l]), pad_heads(dn_dt_bias[l])])
        yd = _deltanet(proj3, ba.reshape(B, L, LANES), dn_conv_w[l], gate_params,
                       dn_norm_w[l][None, :]).reshape(T, DN_WIDTH)
        x2d = _merge(x2d, ya, yc, yd, proj, wa, wc, wd, wo, final_norm_w[None, :], l,
                     final_norm=(l == depth - 1))
    return x2d.reshape(B, L, D_MODEL)
```

```python
import functools
import math

import jax
import jax.numpy as jnp
from jax import lax
from jax.experimental import pallas as pl
from jax.experimental.pallas import tpu as pltpu

F32 = jnp.float32
BF16 = jnp.bfloat16

D_MODEL = 1024
ATT_HEADS = 8
ATT_QK_DIM = 64
ATT_V_DIM = 128
ROPE_THETA = 10000.0
CONV_WIDTH = 1024
CONV_KERNEL = 31
DN_HEADS = 8
DN_HEAD_DIM = 128
DN_WIDTH = DN_HEADS * DN_HEAD_DIM
DN_SHORT_CONV = 4
DN_CHUNK = 64
RMS_EPS = 1e-6
LN_EPS = 1e-5
L2_EPS = 1e-6

LANES = 128
SUBLANES = 8
VMEM_LIMIT = 56 * 1024 * 1024
COL_AQ, COL_AK, COL_AV, COL_AZ = 0, 1, 2, 3
COL_CA, COL_CG, COL_CZ = 4, 5, 6
COL_DQ, COL_DK, COL_DV, COL_DZ = 7, 8, 9, 10
COL_GATE = 11
N_LEAD_TILES = 11
N_MAIN = 14 * D_MODEL
ORIG_DB = N_LEAD_TILES * D_MODEL
ORIG_GATE = ORIG_DB + 2 * DN_HEADS


def _silu(x):
    return x * jax.nn.sigmoid(x)


def _merge_body(x_ref, a_ref, c_ref, d_ref, ga_ref, gc_ref, gd_ref,
                wa_ref, wc_ref, wd_ref, wo_ref, fw_ref, o_ref, *, final_norm):
    ya = jnp.dot(a_ref[...], wa_ref[...], preferred_element_type=F32)
    yc = jnp.dot(c_ref[...], wc_ref[...], preferred_element_type=F32)
    yd = jnp.dot(d_ref[...], wd_ref[...], preferred_element_type=F32)
    merged = (jax.nn.sigmoid(ga_ref[...].astype(F32)) * ya
              + jax.nn.sigmoid(gc_ref[...].astype(F32)) * yc
              + jax.nn.sigmoid(gd_ref[...].astype(F32)) * yd)
    y = x_ref[...] + jnp.dot(merged.astype(BF16), wo_ref[...], preferred_element_type=F32)
    if final_norm:
        ms = jnp.mean(y * y, axis=-1, keepdims=True)
        y = y * lax.rsqrt(ms + RMS_EPS) * fw_ref[...]
    o_ref[...] = y


def _merge(x2d, ya, yc, yd, proj, wa, wc, wd, wo, final_w, layer, *, final_norm, tm=512):
    T = x2d.shape[0]
    row = lambda i: (i, 0)
    const = lambda i: (0, 0)
    wspec = pl.BlockSpec((None, D_MODEL, D_MODEL), lambda i: (layer, 0, 0))
    return pl.pallas_call(
        functools.partial(_merge_body, final_norm=final_norm),
        grid=(T // tm,),
        in_specs=[
            pl.BlockSpec((tm, D_MODEL), row),
            pl.BlockSpec((tm, D_MODEL), row),
            pl.BlockSpec((tm, D_MODEL), row),
            pl.BlockSpec((tm, D_MODEL), row),
            pl.BlockSpec((tm, D_MODEL), lambda i: (i, COL_GATE)),
            pl.BlockSpec((tm, D_MODEL), lambda i: (i, COL_GATE + 1)),
            pl.BlockSpec((tm, D_MODEL), lambda i: (i, COL_GATE + 2)),
            wspec, wspec, wspec, wspec,
            pl.BlockSpec((1, D_MODEL), const),
        ],
        out_specs=pl.BlockSpec((tm, D_MODEL), row),
        out_shape=jax.ShapeDtypeStruct((T, D_MODEL), F32),
        compiler_params=pltpu.CompilerParams(
            dimension_semantics=("parallel",), vmem_limit_bytes=VMEM_LIMIT),
        name="merge",
    )(x2d, ya, yc, yd, proj, proj, proj, wa, wc, wd, wo, final_w)


CONV_HALO = 32


def _causal_taps(buf_ref, w_ref, r0, rows, lanes, n_taps, halo):
    assert halo >= SUBLANES * ((n_taps - 1) // SUBLANES + 1)
    acc = None
    for r in range(min(SUBLANES, n_taps)):
        part = None
        for a in range((n_taps - 1 - r) // SUBLANES + 1):
            s = SUBLANES * a + r
            start = r0 + (halo - SUBLANES * (a + 1))
            if not isinstance(start, int):
                start = pl.multiple_of(start, SUBLANES)
            term = (w_ref[n_taps - 1 - s:n_taps - s, lanes]
                    * buf_ref[pl.ds(start, rows + SUBLANES), lanes])
            part = term if part is None else part + term
        if r:
            part = pltpu.roll(part, r, axis=0)
        acc = part if acc is None else acc + part
    return acc[SUBLANES:, :]


def _inproj_body(x_ref, nw_ref, w_ref, wg_ref, wba_ref, o_ref, ba_ref, h_ref):
    j = pl.program_id(1)

    @pl.when(j == 0)
    def _():
        x = x_ref[...]
        ms = jnp.mean(x * x, axis=-1, keepdims=True)
        h = (x * lax.rsqrt(ms + RMS_EPS) * nw_ref[...]).astype(BF16)
        h_ref[...] = h
        ba_ref[...] = jnp.dot(h, wba_ref[...], preferred_element_type=F32)

    @pl.when(j < N_LEAD_TILES)
    def _():
        o_ref[...] = jnp.dot(h_ref[...], w_ref[...], preferred_element_type=F32).astype(BF16)

    @pl.when(j >= N_LEAD_TILES)
    def _():
        o_ref[...] = jnp.dot(h_ref[...], wg_ref[...], preferred_element_type=F32).astype(BF16)


def _inproj(x2d, norm_w, w_all, w_gate, w_ba, layer, *, tm=2048):
    T = x2d.shape[0]
    tn = D_MODEL
    const = lambda i, j: (0, 0)
    return pl.pallas_call(
        _inproj_body,
        grid=(T // tm, N_MAIN // tn),
        in_specs=[
            pl.BlockSpec((tm, D_MODEL), lambda i, j: (i, 0)),
            pl.BlockSpec((1, D_MODEL), const),
            pl.BlockSpec((None, D_MODEL, tn),
                         lambda i, j: (layer, 0, jnp.minimum(j, N_LEAD_TILES - 1))),
            pl.BlockSpec((None, D_MODEL, tn),
                         lambda i, j: (layer, 0, jnp.maximum(j - N_LEAD_TILES, 0))),
            pl.BlockSpec((None, D_MODEL, LANES), lambda i, j: (layer, 0, 0)),
        ],
        out_specs=[
            pl.BlockSpec((tm, tn), lambda i, j: (i, j)),
            pl.BlockSpec((tm, LANES), lambda i, j: (i, 0)),
        ],
        out_shape=[
            jax.ShapeDtypeStruct((T, N_MAIN), BF16),
            jax.ShapeDtypeStruct((T, LANES), F32),
        ],
        scratch_shapes=[pltpu.VMEM((tm, D_MODEL), BF16)],
        compiler_params=pltpu.CompilerParams(
            dimension_semantics=("parallel", "arbitrary"), vmem_limit_bytes=VMEM_LIMIT),
        name="inproj",
    )(x2d, norm_w, w_all, w_gate, w_ba)


CONV_ROWS = 128
CONV_STRIP = 128


def _conv_body(a_ref, g_ref, z_ref, w_ref, b_ref, lnw_ref, lnb_ref, o_ref, ubuf, cbuf, *, tl):
    @pl.when(pl.program_id(1) == 0)
    def _():
        ubuf[0:CONV_HALO, :] = jnp.zeros((CONV_HALO, CONV_WIDTH), F32)

    a = a_ref[...].astype(F32)
    g = g_ref[...].astype(F32)
    ubuf[CONV_HALO:CONV_HALO + tl, :] = a * (0.5 * jnp.tanh(0.5 * g) + 0.5)

    bias = b_ref[...]
    lnw = lnw_ref[...]
    lnb = lnb_ref[...]

    def step(r, carry):
        r0 = pl.multiple_of(r * CONV_ROWS, CONV_ROWS)
        for c in range(CONV_WIDTH // CONV_STRIP):
            lanes = slice(c * CONV_STRIP, (c + 1) * CONV_STRIP)
            cbuf[:, lanes] = _causal_taps(ubuf, w_ref, r0, CONV_ROWS, lanes, CONV_KERNEL, CONV_HALO)
        u = cbuf[...] + bias
        mu = jnp.mean(u, axis=-1, keepdims=True)
        uc = u - mu
        var = jnp.mean(uc * uc, axis=-1, keepdims=True)
        y = uc * lax.rsqrt(var + LN_EPS) * lnw + lnb
        zz = z_ref[pl.ds(r0, CONV_ROWS), :].astype(F32)
        o_ref[pl.ds(r0, CONV_ROWS), :] = (_silu(y) * _silu(zz)).astype(BF16)
        return carry

    lax.fori_loop(0, tl // CONV_ROWS, step, 0)
    ubuf[0:CONV_HALO, :] = ubuf[tl:tl + CONV_HALO, :]


def _conv_branch(proj3, dw_w, dw_b, ln_w, ln_b, *, tl=512):
    B, L, _ = proj3.shape
    vec = pl.BlockSpec((1, CONV_WIDTH), lambda b, t: (0, 0))
    return pl.pallas_call(
        functools.partial(_conv_body, tl=tl),
        grid=(B, L // tl),
        in_specs=[
            pl.BlockSpec((None, tl, CONV_WIDTH), lambda b, t: (b, t, COL_CA)),
            pl.BlockSpec((None, tl, CONV_WIDTH), lambda b, t: (b, t, COL_CG)),
            pl.BlockSpec((None, tl, CONV_WIDTH), lambda b, t: (b, t, COL_CZ)),
            pl.BlockSpec((CONV_KERNEL, CONV_WIDTH), lambda b, t: (0, 0)),
            vec, vec, vec,
        ],
        out_specs=pl.BlockSpec((None, tl, CONV_WIDTH), lambda b, t: (b, t, 0)),
        out_shape=jax.ShapeDtypeStruct((B, L, CONV_WIDTH), BF16),
        scratch_shapes=[pltpu.VMEM((CONV_HALO + tl, CONV_WIDTH), F32),
                        pltpu.VMEM((CONV_ROWS, CONV_WIDTH), F32)],
        compiler_params=pltpu.CompilerParams(
            dimension_semantics=("parallel", "arbitrary"), vmem_limit_bytes=VMEM_LIMIT),
        name="conv_branch",
    )(proj3, proj3, proj3, dw_w, dw_b, ln_w, ln_b)


def _nt_dot(a, b):
    return lax.dot_general(a, b, (((1,), (1,)), ((), ())), preferred_element_type=F32)


def _rope(t, cos, sin_signed):
    lane = lax.broadcasted_iota(jnp.int32, t.shape, 1)
    first_half = (lane % ATT_QK_DIM) < (ATT_QK_DIM // 2)
    half = ATT_QK_DIM // 2
    swapped = jnp.where(first_half, pltpu.roll(t, t.shape[1] - half, axis=1),
                        pltpu.roll(t, half, axis=1))
    return t * cos + swapped * sin_signed


ATT_HPS = 2
ATT_ZERO_SLOT = 2
ATT_ONES_ROWS = 16


def _attn_body(q_ref, k_ref, v_ref, z_ref, cos_ref, sin_ref, lam_ref, sw_ref, o_ref,
               kr_ref, vt_ref, m_ref, alpha_ref, acc_ref, s_ref, p_ref, qs_ref, *, lambda_init, tq):
    heads = range(ATT_HPS)
    hl = [slice(h * LANES, (h + 1) * LANES) for h in heads]
    n_blocks = vt_ref.shape[1]

    cos_k = jnp.concatenate([cos_ref[...]] * ATT_HPS, axis=1)
    sin_k = jnp.concatenate([sin_ref[...]] * ATT_HPS, axis=1)
    kr_ref[...] = _rope(k_ref[...].astype(F32), cos_k, sin_k).astype(BF16)

    ri = lax.broadcasted_iota(jnp.int32, (ATT_V_DIM, ATT_V_DIM), 0)
    ci = lax.broadcasted_iota(jnp.int32, (ATT_V_DIM, ATT_V_DIM), 1)
    ident = (ri == ci).astype(BF16)
    for jb in range(n_blocks):
        for h in heads:
            vt_ref[h, jb, 0:ATT_V_DIM, :] = _nt_dot(
                ident, v_ref[jb * tq:(jb + 1) * tq, hl[h]]).astype(BF16)
            vt_ref[h, jb, ATT_V_DIM:, :] = jnp.ones((ATT_ONES_ROWS, tq), BF16)
    for h in heads:
        p_ref[h, ATT_ZERO_SLOT] = jnp.zeros(p_ref.shape[2:], BF16)

    lq = lam_ref[...]
    lam = (jnp.exp(jnp.sum(lq[0:1, :] * lq[1:2, :], axis=-1, keepdims=True))
           - jnp.exp(jnp.sum(lq[2:3, :] * lq[3:4, :], axis=-1, keepdims=True)) + lambda_init)

    def block_rows(qi):
        return pl.ds(pl.multiple_of(qi * tq, tq), tq)

    def scores(h, j):
        return _nt_dot(kr_ref[block_rows(j), hl[h]], qs_ref[:, hl[h]])

    def prologue(qi):
        rows = block_rows(qi)
        cos_q = jnp.concatenate([cos_ref[rows, :]] * ATT_HPS, axis=1)
        sin_q = jnp.concatenate([sin_ref[rows, :]] * ATT_HPS, axis=1)
        q = (_rope(q_ref[rows, :].astype(F32), cos_q, sin_q)
             * (ATT_QK_DIM ** -0.5 * math.log2(math.e)))
        lane = lax.broadcasted_iota(jnp.int32, q.shape, 1) % LANES
        qs_ref[...] = jnp.concatenate([jnp.where(lane < ATT_QK_DIM, q, 0.0),
                                       jnp.where(lane >= ATT_QK_DIM, q, 0.0)], axis=0).astype(BF16)
        m_ref[...] = jnp.full(m_ref.shape, -jnp.inf, F32)
        acc_ref[...] = jnp.zeros(acc_ref.shape, F32)
        alpha_ref[...] = jnp.ones(alpha_ref.shape, F32)
        for h in heads:
            s_ref[h, 0] = scores(h, 0)

    def softmax_step(h, s):
        m_prev = m_ref[h]
        m_new = jnp.maximum(m_prev, jnp.max(s, axis=0, keepdims=True))
        alpha = jnp.exp2(m_prev - m_new)
        p = jnp.exp2(s - m_new)
        m_ref[h] = m_new
        return alpha, p.astype(BF16)

    def pv(h, j, p):
        return jnp.dot(vt_ref[h, j], p, preferred_element_type=F32)

    def prev_slot(t):
        return jnp.where(t == 0, ATT_ZERO_SLOT, 1 - t % 2)

    def trip(t, carry):
        slot = t % 2
        pv_prev = [pv(h, jnp.maximum(t - 1, 0), p_ref[h, prev_slot(t)]) for h in heads]
        s_next = [scores(h, t + 1) for h in heads]
        for h in heads:
            alpha, p = softmax_step(h, s_ref[h, slot])
            p_ref[h, slot] = p
            acc_ref[h] = alpha_ref[h] * acc_ref[h] + pv_prev[h]
            alpha_ref[h] = alpha
        for h in heads:
            s_ref[h, 1 - slot] = s_next[h]
        return carry

    def trip_pair(g, carry):
        t = 2 * g
        pv_a = [pv(h, jnp.maximum(t - 1, 0), p_ref[h, prev_slot(t)]) for h in heads]
        s_a = [scores(h, t + 1) for h in heads]
        s_b = [scores(h, t + 2) for h in heads]
        p_a = []
        for h in heads:
            alpha, p = softmax_step(h, s_ref[h, 0])
            p_a.append(p)
            acc_ref[h] = alpha_ref[h] * acc_ref[h] + pv_a[h]
            alpha_ref[h] = alpha
        pv_b = [pv(h, t, p_a[h]) for h in heads]
        for h in heads:
            alpha, p = softmax_step(h, s_a[h])
            p_ref[h, 1] = p
            acc_ref[h] = alpha_ref[h] * acc_ref[h] + pv_b[h]
            alpha_ref[h] = alpha
        for h in heads:
            s_ref[h, 0] = s_b[h]
        return carry

    def epilogue(qi):
        slot = qi % 2
        pv_prev = [pv(h, jnp.maximum(qi - 1, 0), p_ref[h, prev_slot(qi)]) for h in heads]
        key = lax.broadcasted_iota(jnp.int32, (tq, 2 * tq), 0)
        qry = lax.broadcasted_iota(jnp.int32, (tq, 2 * tq), 1) % tq
        last = [softmax_step(h, jnp.where(key <= qry, s_ref[h, slot], -jnp.inf)) for h in heads]
        pv_last = [pv(h, qi, last[h][1]) for h in heads]
        outs = []
        for h in heads:
            acc = last[h][0] * (alpha_ref[h] * acc_ref[h] + pv_prev[h]) + pv_last[h]
            o_all = acc[:ATT_V_DIM, :] / acc[ATT_V_DIM:ATT_V_DIM + 1, :]
            ot = o_all[:, :tq] - lam * o_all[:, tq:]
            ms = jnp.mean(ot * ot, axis=0, keepdims=True)
            outs.append((ot * lax.rsqrt(ms + RMS_EPS) * sw_ref[...] * (1.0 - lambda_init)).T)
        return jnp.concatenate(outs, axis=1)

    prologue(0)

    def q_block(qi, carry):
        lax.fori_loop(0, qi // 2, trip_pair, 0)

        @pl.when(qi % 2 == 1)
        def _():
            trip(qi - 1, 0)

        on = epilogue(qi)
        prologue(jnp.minimum(qi + 1, n_blocks - 1))
        rows = block_rows(qi)
        o_ref[rows, :] = (on * _silu(z_ref[rows, :].astype(F32))).astype(BF16)
        return carry

    lax.fori_loop(0, n_blocks, q_block, 0)


def _attention(proj3, cosf, sinf, lam_qk, subln_w, *, lambda_init, tq=256):
    B, L, _ = proj3.shape
    hw = ATT_HPS * LANES
    hpb = D_MODEL // hw
    full = lambda col: pl.BlockSpec((None, L, hw), lambda b, h: (b, 0, col * hpb + h))
    table = pl.BlockSpec((None, L, LANES), lambda b, h: (b, 0, 0))
    return pl.pallas_call(
        functools.partial(_attn_body, lambda_init=lambda_init, tq=tq),
        grid=(B, ATT_HEADS // ATT_HPS),
        in_specs=[
            full(COL_AQ), full(COL_AK), full(COL_AV), full(COL_AZ), table, table,
            pl.BlockSpec((4, ATT_QK_DIM), lambda b, h: (0, 0)),
            pl.BlockSpec((ATT_V_DIM, 1), lambda b, h: (0, 0)),
        ],
        out_specs=pl.BlockSpec((None, L, hw), lambda b, h: (b, 0, h)),
        out_shape=jax.ShapeDtypeStruct((B, L, ATT_HEADS * ATT_V_DIM), BF16),
        scratch_shapes=[
            pltpu.VMEM((L, hw), BF16),
            pltpu.VMEM((ATT_HPS, L // tq, ATT_V_DIM + ATT_ONES_ROWS, tq), BF16),
            pltpu.VMEM((ATT_HPS, 1, 2 * tq), F32),
            pltpu.VMEM((ATT_HPS, 1, 2 * tq), F32),
            pltpu.VMEM((ATT_HPS, ATT_V_DIM + ATT_ONES_ROWS, 2 * tq), F32),
            pltpu.VMEM((ATT_HPS, 2, tq, 2 * tq), F32),
            pltpu.VMEM((ATT_HPS, 3, tq, 2 * tq), BF16),
            pltpu.VMEM((2 * tq, hw), BF16),
        ],
        compiler_params=pltpu.CompilerParams(
            dimension_semantics=("parallel", "parallel"), vmem_limit_bytes=VMEM_LIMIT),
        name="diff_attention",
    )(proj3, proj3, proj3, proj3, cosf, sinf, lam_qk, subln_w)


DN_HALO = 8
DN_STRIP = 128
DN_CONV_ROWS = 128
DN_B_LANE = 0
DN_A_LANE = DN_HEADS
DN_ONE_LANE = DN_A_LANE + 3 * DN_HEADS
DN_GROUP = 4


def _split3(x):
    hi = x.astype(BF16).astype(F32)
    r1 = x - hi
    mid = r1.astype(BF16).astype(F32)
    lo = (r1 - mid).astype(BF16).astype(F32)
    return hi, mid, lo


def _dn_body(q_ref, k_ref, v_ref, z_ref, ba_ref, cw_ref, gp_ref, nw_ref, o_ref,
             xbuf, qkv, u_s, w_s, qg_s, kd_s, in_s, dec_s, s_ref, *, tb):
    C = DN_CHUNK
    W = DN_WIDTH
    hd = DN_HEAD_DIM

    @pl.when(pl.program_id(1) == 0)
    def _():
        xbuf[0:DN_HALO, :] = jnp.zeros((DN_HALO, 3 * W), F32)
        s_ref[...] = jnp.zeros(s_ref.shape, F32)

    xbuf[DN_HALO:DN_HALO + tb, 0:W] = q_ref[...].astype(F32)
    xbuf[DN_HALO:DN_HALO + tb, W:2 * W] = k_ref[...].astype(F32)
    xbuf[DN_HALO:DN_HALO + tb, 2 * W:3 * W] = v_ref[...].astype(F32)

    def conv_step(i, carry):
        r0 = pl.multiple_of(i * DN_CONV_ROWS, DN_CONV_ROWS)
        for c in range(3 * W // DN_STRIP):
            lanes = slice(c * DN_STRIP, (c + 1) * DN_STRIP)
            qkv[pl.ds(r0, DN_CONV_ROWS), lanes] = _silu(
                _causal_taps(xbuf, cw_ref, r0, DN_CONV_ROWS, lanes, DN_SHORT_CONV, DN_HALO))
        return carry

    lax.fori_loop(0, tb // DN_CONV_ROWS, conv_step, 0)
    xbuf[0:DN_HALO, :] = xbuf[tb:tb + DN_HALO, :]

    n_chunks = tb // C
    G = DN_GROUP
    n_groups = DN_HEADS // G
    gw = G * hd
    gc_w = G * C

    row = lax.broadcasted_iota(jnp.int32, (C, C), 0)
    col = lax.broadcasted_iota(jnp.int32, (C, C), 1)
    tri = (row >= col).astype(BF16)
    lane = lax.broadcasted_iota(jnp.int32, (C, LANES), 1)
    ri = lax.broadcasted_iota(jnp.int32, (C, gc_w), 0)
    cj = lax.broadcasted_iota(jnp.int32, (C, gc_w), 1) % C
    eye_cat = (ri == cj).astype(F32)
    bd_sq = (lax.broadcasted_iota(jnp.int32, (gc_w, gc_w), 0) // C
             == lax.broadcasted_iota(jnp.int32, (gc_w, gc_w), 1) // C)
    bd_wide = (lax.broadcasted_iota(jnp.int32, (gc_w, gw), 0) // C
               == lax.broadcasted_iota(jnp.int32, (gc_w, gw), 1) // hd)
    neg_a = -jnp.exp(gp_ref[0:1, :])
    dt_bias = gp_ref[1:2, :]
    norm_w = nw_ref[...]

    def block_diag(x, mask):
        return jnp.where(mask, jnp.tile(x, (G, 1)), 0.0).astype(BF16)

    def bdot(a, b):
        return jnp.dot(a, b, preferred_element_type=F32)

    probs = [(ci, g) for ci in range(n_chunks) for g in range(n_groups)]

    gd_lhs, gcb_all, kq_lhs, kn_grp, vb_grp, kbe_grp = {}, {}, {}, {}, {}, {}
    for ci in range(n_chunks):
        r0 = ci * C
        ba = ba_ref[r0:r0 + C, :]
        beta_all = jax.nn.sigmoid(ba)
        xa = ba + dt_bias
        softplus = jnp.maximum(xa, 0.0) + jnp.log1p(jnp.exp(-jnp.abs(xa)))
        g = jnp.where((lane >= DN_A_LANE) & (lane < DN_A_LANE + DN_HEADS), neg_a * softplus, 0.0)
        g_hi, g_mid, g_lo = _split3(g)
        gc = (bdot(tri, g_hi.astype(BF16)) + bdot(tri, g_mid.astype(BF16))
              + bdot(tri, g_lo.astype(BF16)))
        c_hi, c_mid, c_lo = _split3(gc)
        gd_lhs[ci] = (c_hi + pltpu.roll(c_mid, DN_HEADS, axis=1) + pltpu.roll(c_lo, 2 * DN_HEADS, axis=1)
                      + jnp.where((lane >= DN_ONE_LANE) & (lane < DN_ONE_LANE + 3), 1.0, 0.0)).astype(BF16)

        beta_b = jnp.concatenate(
            [jnp.broadcast_to(beta_all[:, DN_B_LANE + h:DN_B_LANE + h + 1], (C, hd))
             for h in range(DN_HEADS)], axis=1)
        gcb = jnp.concatenate(
            [jnp.broadcast_to(gc[:, DN_A_LANE + h:DN_A_LANE + h + 1], (C, hd))
             for h in range(DN_HEADS)], axis=1)
        gcb_all[ci] = gcb
        eg = jnp.exp(gcb)
        glast = gcb[C - 1:C, :]
        dec_s[ci] = jnp.exp(glast)

        q_all = qkv[r0:r0 + C, 0:W]
        k_all = qkv[r0:r0 + C, W:2 * W]
        v_all = qkv[r0:r0 + C, 2 * W:3 * W]
        qn = jnp.concatenate(
            [q_all[:, h * hd:(h + 1) * hd] * lax.rsqrt(jnp.sum(
                q_all[:, h * hd:(h + 1) * hd] ** 2, axis=-1, keepdims=True) + L2_EPS)
             for h in range(DN_HEADS)], axis=1) * (hd ** -0.5)
        kn = jnp.concatenate(
            [k_all[:, h * hd:(h + 1) * hd] * lax.rsqrt(jnp.sum(
                k_all[:, h * hd:(h + 1) * hd] ** 2, axis=-1, keepdims=True) + L2_EPS)
             for h in range(DN_HEADS)], axis=1)
        kb = kn * beta_b
        qg_s[r0:r0 + C, :] = (qn * eg).astype(BF16)
        kd_s[r0:r0 + C, :] = (kn * jnp.exp(glast - gcb)).astype(BF16)
        for gi in range(n_groups):
            gl = slice(gi * gw, (gi + 1) * gw)
            kq_lhs[ci, gi] = jnp.concatenate([kb[:, gl], qn[:, gl]], axis=0).astype(BF16)
            kn_grp[ci, gi] = kn[:, gl]
            vb_grp[ci, gi] = (v_all * beta_b)[:, gl]
            kbe_grp[ci, gi] = (kb * eg)[:, gl]

    gdiff, kq = {}, {}
    for ci, gi in probs:
        blocks = []
        for h in range(gi * G, (gi + 1) * G):
            p_hi, p_mid, p_lo = _split3(gcb_all[ci][:, h * hd:(h + 1) * hd])
            onehot = ((lane == DN_A_LANE + h) | (lane == DN_A_LANE + DN_HEADS + h)
                      | (lane == DN_A_LANE + 2 * DN_HEADS + h))
            blocks.append(jnp.where(lane == DN_ONE_LANE, -p_hi, jnp.where(
                lane == DN_ONE_LANE + 1, -p_mid, jnp.where(
                    lane == DN_ONE_LANE + 2, -p_lo, jnp.where(onehot, 1.0, 0.0)))))
        gd_rhs = jnp.concatenate(blocks, axis=0).astype(BF16)
        gdiff[ci, gi] = _nt_dot(gd_lhs[ci], gd_rhs)
        kq[ci, gi] = _nt_dot(kq_lhs[ci, gi], block_diag(kn_grp[ci, gi], bd_wide))

    m, p = {}, {}
    for pr in probs:
        decay = jnp.exp(jnp.where(ri >= cj, gdiff[pr], -jnp.inf))
        m[pr] = -jnp.where(ri > cj, kq[pr][:C, :] * decay, 0.0)
        ci, gi = pr
        in_s[ci * C:(ci + 1) * C, gi * gc_w:(gi + 1) * gc_w] = (kq[pr][C:, :] * decay).astype(BF16)
        p[pr] = eye_cat + m[pr]

    n_levels = int(math.log2(C)) - 1
    for pr in probs:
        m[pr] = bdot(m[pr].astype(BF16), block_diag(m[pr], bd_sq))
    for lvl in range(n_levels):
        last = lvl == n_levels - 1
        res = {}
        for pr in probs:
            lhs = p[pr] if last else jnp.concatenate([p[pr], m[pr]], axis=0)
            res[pr] = bdot(lhs.astype(BF16), block_diag(m[pr], bd_sq))
        for pr in probs:
            p[pr] = p[pr] + res[pr][:C, :]
            if not last:
                m[pr] = res[pr][C:, :]

    for pr in probs:
        ci, gi = pr
        pb = p[pr].astype(BF16)
        rows = slice(ci * C, (ci + 1) * C)
        gl = slice(gi * gw, (gi + 1) * gw)
        for hg in range(G):
            hs = slice(hg * hd, (hg + 1) * hd)
            rhs = jnp.concatenate([vb_grp[pr][:, hs], kbe_grp[pr][:, hs]], axis=1).astype(BF16)
            sol = bdot(pb[:, hg * C:(hg + 1) * C], rhs)
            lanes = slice(gi * gw + hg * hd, gi * gw + (hg + 1) * hd)
            u_s[rows, lanes] = sol[:, :hd]
            w_s[rows, lanes] = sol[:, hd:].astype(BF16)

    def chunk_step(ci, carry):
        r0 = pl.multiple_of(ci * C, C)
        rows = pl.ds(r0, C)
        sb = [s_ref[h].astype(BF16) for h in range(DN_HEADS)]
        ws = [bdot(jnp.concatenate([w_s[rows, h * hd:(h + 1) * hd], qg_s[rows, h * hd:(h + 1) * hd]],
                                   axis=0), sb[h]) for h in range(DN_HEADS)]
        v_new = u_s[rows, :] - jnp.concatenate([x[:C, :] for x in ws], axis=1)
        vnb = v_new.astype(BF16)
        o_intra = [bdot(in_s[rows, gi * gc_w:(gi + 1) * gc_w],
                        block_diag(v_new[:, gi * gw:(gi + 1) * gw], bd_wide))
                   for gi in range(n_groups)]
        dec = dec_s[ci]
        for h in range(DN_HEADS):
            hl = slice(h * hd, (h + 1) * hd)
            s_ref[h] = s_ref[h] * dec[:, hl] + lax.dot_general(
                kd_s[rows, hl], vnb[:, hl], (((0,), (0,)), ((), ())), preferred_element_type=F32)
        o = jnp.concatenate([x[C:, :] for x in ws], axis=1) + jnp.concatenate(o_intra, axis=1)
        for h in range(DN_HEADS):
            hl = slice(h * hd, (h + 1) * hd)
            oh = o[:, hl]
            ms = jnp.mean(oh * oh, axis=-1, keepdims=True)
            on = oh * lax.rsqrt(ms + RMS_EPS) * norm_w
            o_ref[rows, hl] = (on * _silu(z_ref[rows, hl].astype(F32))).astype(BF16)
        return carry

    lax.fori_loop(0, n_chunks, chunk_step, 0)


def _deltanet(proj3, ba3, conv_w, gate_params, norm_w, *, tb=512):
    B, L, _ = proj3.shape
    blk = lambda col: pl.BlockSpec((None, tb, DN_WIDTH), lambda b, t: (b, t, col))
    return pl.pallas_call(
        functools.partial(_dn_body, tb=tb),
        grid=(B, L // tb),
        in_specs=[
            blk(COL_DQ), blk(COL_DK), blk(COL_DV), blk(COL_DZ),
            pl.BlockSpec((None, tb, LANES), lambda b, t: (b, t, 0)),
            pl.BlockSpec((DN_SHORT_CONV, 3 * DN_WIDTH), lambda b, t: (0, 0)),
            pl.BlockSpec((2, LANES), lambda b, t: (0, 0)),
            pl.BlockSpec((1, DN_HEAD_DIM), lambda b, t: (0, 0)),
        ],
        out_specs=pl.BlockSpec((None, tb, DN_WIDTH), lambda b, t: (b, t, 0)),
        out_shape=jax.ShapeDtypeStruct((B, L, DN_WIDTH), BF16),
        scratch_shapes=[
            pltpu.VMEM((DN_HALO + tb, 3 * DN_WIDTH), F32),
            pltpu.VMEM((tb, 3 * DN_WIDTH), F32),
            pltpu.VMEM((tb, DN_WIDTH), F32),
            pltpu.VMEM((tb, DN_WIDTH), BF16),
            pltpu.VMEM((tb, DN_WIDTH), BF16),
            pltpu.VMEM((tb, DN_WIDTH), BF16),
            pltpu.VMEM((tb, DN_HEADS * DN_CHUNK), BF16),
            pltpu.VMEM((tb // DN_CHUNK, 1, DN_WIDTH), F32),
            pltpu.VMEM((DN_HEADS, DN_HEAD_DIM, DN_HEAD_DIM), F32),
        ],
        compiler_params=pltpu.CompilerParams(
            dimension_semantics=("parallel", "arbitrary"), vmem_limit_bytes=VMEM_LIMIT),
        name="gated_deltanet",
    )(proj3, proj3, proj3, proj3, ba3, conv_w, gate_params, norm_w)


def _rope_tables(positions):
    half = ATT_QK_DIM // 2
    inv_freq = ROPE_THETA ** (-jnp.arange(0, ATT_QK_DIM, 2, dtype=F32) / ATT_QK_DIM)
    reps = LANES // half
    freq_lanes = jnp.tile(inv_freq, reps)
    sign_lanes = jnp.tile(jnp.repeat(jnp.array([-1.0, 1.0], F32), half), reps // 2)
    ang = positions.astype(F32)[..., None] * freq_lanes
    return jnp.cos(ang), jnp.sin(ang) * sign_lanes


def kernel(x, positions, norm_w, w_in, lam_qk, attn_subln_w, w_attn_out, conv_dw_w, conv_dw_b,
           conv_ln_w, conv_ln_b, w_conv_out, dn_conv_w, dn_a_log, dn_dt_bias, dn_norm_w,
           w_dn_out, w_out, final_norm_w):
    B, L, _ = x.shape
    T = B * L
    x2d = x.reshape(T, D_MODEL)
    depth = norm_w.shape[0]
    cosf, sinf = _rope_tables(positions)
    pad_heads = lambda v: jnp.pad(v, (DN_A_LANE, LANES - DN_A_LANE - DN_HEADS))
    w_all = w_in.astype(BF16)
    w_gate = w_all[:, :, ORIG_GATE:]
    w_ba = jnp.pad(w_all[:, :, ORIG_DB:ORIG_GATE], ((0, 0), (0, 0), (0, LANES - 2 * DN_HEADS)))
    wa, wc, wd, wo = (w.astype(BF16) for w in (w_attn_out, w_conv_out, w_dn_out, w_out))
    for l in range(depth):
        proj, ba = _inproj(x2d, norm_w[l][None, :], w_all, w_gate, w_ba, l)
        proj3 = proj.reshape(B, L, N_MAIN)
        lambda_init = 0.8 - 0.6 * math.exp(-0.3 * l)
        ya = _attention(proj3, cosf, sinf, lam_qk[l], attn_subln_w[l][:, None],
                        lambda_init=lambda_init).reshape(T, ATT_HEADS * ATT_V_DIM)
        yc = _conv_branch(proj3, conv_dw_w[l], conv_dw_b[l][None, :], conv_ln_w[l][None, :],
                          conv_ln_b[l][None, :]).reshape(T, CONV_WIDTH)
        gate_params = jnp.stack([pad_heads(dn_a_log[l]), pad_heads(dn_dt_bias[l])])
        yd = _deltanet(proj3, ba.reshape(B, L, LANES), dn_conv_w[l], gate_params,
                       dn_norm_w[l][None, :]).reshape(T, DN_WIDTH)
        x2d = _merge(x2d, ya, yc, yd, proj, wa, wc, wd, wo, final_norm_w[None, :], l,
                     final_norm=(l == depth - 1))
    return x2d.reshape(B, L, D_MODEL)
```
